```python
import math
import jax
import jax.numpy as jnp
from jax import lax
import numpy as np

D_MODEL = 2048
BATCH = 1
SEQ = 8192
DEPTH = 1
DEC_BATCH = 128
DEC_SEQ = 4
PAST_LEN = 2048
PAGE_SIZE = 128

SB_HEADS = 8
SB_HD = 128
SB_WIDTH = SB_HEADS * SB_HD
SB_BIAS_INIT = -6.0
HG_HEADS = 8
HG_HD = 128
HG_WIDTH = HG_HEADS * HG_HD
MIX_WIDTH = SB_WIDTH + HG_WIDTH
IN_COLS = 3 * SB_WIDTH + 4 * HG_WIDTH
XA_HEADS = 4
XA_HD = 128
XA_WIDTH = XA_HEADS * XA_HD
N_MEM = 256
N_GROUPS = 4
EXP_PER_GROUP = 8
N_EXPERTS = N_GROUPS * EXP_PER_GROUP
EXP_FF = 512
TOP_K_INNER = 2
Q_BLOCK = 128
HG_CHUNK = 32
EPS = 1e-6

kernel_name = "hybrid_sb_hgrn2_hmoe_step"


def rmsnorm(x, g):
    xf = x.astype(jnp.float32)
    y = xf * lax.rsqrt(jnp.mean(xf * xf, axis=-1, keepdims=True) + EPS)
    return (y * g.astype(jnp.float32)).astype(x.dtype)


def head_rmsnorm(o, g):
    of = o.astype(jnp.float32)
    return of * lax.rsqrt(jnp.mean(of * of, axis=-1, keepdims=True) + EPS) * g.astype(jnp.float32)


def split_proj(xn, w_in):
    B, T, _ = xn.shape
    proj = jnp.einsum('btd,dc->btc', xn, w_in)
    offs = [SB_WIDTH, 2 * SB_WIDTH, 3 * SB_WIDTH,
            3 * SB_WIDTH + HG_WIDTH, 3 * SB_WIDTH + 2 * HG_WIDTH, 3 * SB_WIDTH + 3 * HG_WIDTH]
    sq, sk, sv, hq, hf, hi, hg = jnp.split(proj, offs, axis=-1)
    sb = [a.reshape(B, T, SB_HEADS, SB_HD) for a in (sq, sk, sv)]
    hgp = [a.reshape(B, T, HG_HEADS, HG_HD) for a in (hq, hf, hi, hg)]
    return sb[0], sb[1], sb[2], hgp[0], hgp[1], hgp[2], hgp[3]


def sb_core(q, k, v, q_pos, k_pos, bias):
    z = jnp.einsum('bqhd,bkhd->bhqk', q, k, preferred_element_type=jnp.float32) * (1.0 / math.sqrt(q.shape[-1]))
    z = z + bias.astype(jnp.float32)[None, :, None, None]
    mask = k_pos[None, :] < q_pos[:, None]
    log_1m = jnp.where(mask, jax.nn.log_sigmoid(-z), 0.0)
    stick = lax.cumsum(log_1m, axis=3, reverse=True) - log_1m
    w = jnp.where(mask, jnp.exp(jax.nn.log_sigmoid(z) + stick), 0.0)
    return jnp.einsum('bhqk,bkhd->bqhd', w.astype(v.dtype), v, preferred_element_type=jnp.float32)


def sb_prompt(q, k, v, bias):
    B, T, H, Dh = q.shape
    nb = T // Q_BLOCK
    q_blocks = jnp.moveaxis(q.reshape(B, nb, Q_BLOCK, H, Dh), 1, 0)
    k_pos = jnp.arange(T)

    def block(args):
        qb, i = args
        return sb_core(qb, k, v, i * Q_BLOCK + jnp.arange(Q_BLOCK), k_pos, bias)

    o = lax.map(block, (q_blocks, jnp.arange(nb)))
    return jnp.moveaxis(o, 0, 1).reshape(B, T, H, Dh)


def hgrn2_chunked(q, k, v, log_f, s0, chunk):
    B, T, H, Dk = q.shape
    n = T // chunk

    def to_chunks(a):
        return a.reshape(B, n, chunk, H, a.shape[-1]).transpose(1, 0, 3, 2, 4)

    qc, kc, vc, gc = to_chunks(q), to_chunks(k), to_chunks(v), to_chunks(log_f)
    tri = jnp.tril(jnp.ones((chunk, chunk), dtype=bool))

    def step(S, inp):
        qi, ki, vi, gi = inp
        b = jnp.cumsum(gi, axis=2)
        q_dec = qi * jnp.exp(b)
        k_inv = ki * jnp.exp(-b)
        a = jnp.where(tri, jnp.einsum('bhtk,bhsk->bhts', q_dec, k_inv), 0.0)
        o = jnp.einsum('bhts,bhsv->bhtv', a, vi) + jnp.einsum('bhtk,bhkv->bhtv', q_dec, S)
        b_last = b[:, :, -1:, :]
        S_new = (jnp.exp(b_last)[:, :, 0, :, None] * S
                 + jnp.einsum('bhsk,bhsv->bhkv', ki * jnp.exp(b_last - b), vi))
        return S_new, o

    S_fin, oc = lax.scan(step, s0, (qc, kc, vc, gc))
    o = oc.transpose(1, 0, 3, 2, 4).reshape(B, T, H, v.shape[-1])
    return o, S_fin


def hgrn_branch(hq, hf, hi, lb, s0):
    T = hq.shape[1]
    lbh = lb.reshape(HG_HEADS, HG_HD).astype(jnp.float32)
    fp = hf.astype(jnp.float32)
    f = lbh + (1.0 - lbh) * jax.nn.sigmoid(fp)
    k = (1.0 - lbh) * jax.nn.sigmoid(-fp)
    q = jax.nn.silu(hq.astype(jnp.float32))
    chunk = HG_CHUNK if T % HG_CHUNK == 0 else T
    return hgrn2_chunked(q, k, hi.astype(jnp.float32), jnp.log(f), s0.astype(jnp.float32), chunk)


def merge_groups(o_sb, o_hg, hg_gate, g_sb, g_hg, w_out):
    B, T = o_sb.shape[:2]
    a = head_rmsnorm(o_sb, g_sb)
    b = head_rmsnorm(o_hg, g_hg) * jax.nn.silu(hg_gate.astype(jnp.float32))
    cat = jnp.concatenate([a.reshape(B, T, SB_WIDTH), b.reshape(B, T, HG_WIDTH)], axis=-1)
    return jnp.einsum('btc,cd->btd', cat.astype(w_out.dtype), w_out)


def mixer_prompt(xn, w_in, sb_bias, lb, g_sb, g_hg, w_out):
    B = xn.shape[0]
    sq, sk, sv, hq, hf, hi, hg = split_proj(xn, w_in)
    o_sb = sb_prompt(sq, sk, sv, sb_bias)
    s0 = jnp.zeros((B, HG_HEADS, HG_HD, HG_HD), jnp.float32)
    o_hg, S = hgrn_branch(hq, hf, hi, lb, s0)
    y = merge_groups(o_sb, o_hg, hg, g_sb, g_hg, w_out)
    return y, sk, sv, S.astype(xn.dtype)


def mixer_sample(xn, k_cache, v_cache, page_table, s0, w_in, sb_bias, lb, g_sb, g_hg, w_out):
    B, T, _ = xn.shape
    sq, sk, sv, hq, hf, hi, hg = split_proj(xn, w_in)
    past = page_table.shape[1] * k_cache.shape[1]
    k_past = k_cache[page_table].reshape(B, past, SB_HEADS, SB_HD)
    v_past = v_cache[page_table].reshape(B, past, SB_HEADS, SB_HD)
    k_all = jnp.concatenate([k_past, sk.astype(k_past.dtype)], axis=1)
    v_all = jnp.concatenate([v_past, sv.astype(v_past.dtype)], axis=1)
    o_sb = sb_core(sq, k_all, v_all, past + jnp.arange(T), jnp.arange(past + T), sb_bias)
    o_hg, S = hgrn_branch(hq, hf, hi, lb, s0)
    y = merge_groups(o_sb, o_hg, hg, g_sb, g_hg, w_out)
    return y, sk, sv, S.astype(s0.dtype)


def memory_kv(mem, g_mem, w_xkv):
    B, M, _ = mem.shape
    kv = jnp.einsum('bmd,dc->bmc', rmsnorm(mem, g_mem), w_xkv)
    mk, mv = jnp.split(kv, 2, axis=-1)
    return mk.reshape(B, M, XA_HEADS, XA_HD), mv.reshape(B, M, XA_HEADS, XA_HD)


def cross_attn(hn, mk, mv, w_xq, w_xo):
    B, T, _ = hn.shape
    q = jnp.einsum('btd,dc->btc', hn, w_xq).reshape(B, T, XA_HEADS, XA_HD)
    s = jnp.einsum('bthd,bmhd->bhtm', q, mk, preferred_element_type=jnp.float32) * (1.0 / math.sqrt(XA_HD))
    p = jax.nn.softmax(s, axis=-1).astype(mv.dtype)
    o = jnp.einsum('bhtm,bmhd->bthd', p, mv).reshape(B, T, XA_WIDTH)
    return jnp.einsum('btc,cd->btd', o.astype(w_xo.dtype), w_xo)


def hier_moe(hn, w_rg, w_re, w_gate, w_up, w_down):
    B, T, D = hn.shape
    x = hn.reshape(B * T, D)
    p_group = jax.nn.softmax(jnp.einsum('nd,dg->ng', x, w_rg, preferred_element_type=jnp.float32), axis=-1)
    g_idx = jnp.argmax(p_group, axis=-1)
    g_w = jnp.max(p_group, axis=-1)
    le = jnp.einsum('nd,gde->nge', x, w_re, preferred_element_type=jnp.float32)
    le = jnp.take_along_axis(le, g_idx[:, None, None], axis=1)[:, 0]
    top_v, top_i = lax.top_k(le, TOP_K_INNER)
    w_pair = jax.nn.softmax(top_v, axis=-1) * g_w[:, None]
    e_id = g_idx[:, None] * EXP_PER_GROUP + top_i
    gates = jnp.einsum('nk,nke->ne', w_pair, jax.nn.one_hot(e_id, N_EXPERTS, dtype=jnp.float32))
    a = jnp.einsum('nd,edf->nef', x, w_gate)
    u = jnp.einsum('nd,edf->nef', x, w_up)
    hmid = jax.nn.silu(a) * u * gates[:, :, None].astype(a.dtype)
    y = jnp.einsum('nef,efd->nd', hmid, w_down)
    return y.reshape(B, T, D)


def setup_inputs(seed: int = 0) -> dict:
    key = jax.random.key(seed)
    ks = jax.random.split(key, 32)
    f32 = jnp.float32
    n_pages = PAST_LEN // PAGE_SIZE
    n_used = DEC_BATCH * n_pages
    n_phys = n_used + max(1, n_used // 4)

    def nrm(k, shape, scale=1.0):
        return jax.random.normal(k, shape, f32) * scale

    def gain(k, shape):
        return 1.0 + 0.02 * jax.random.normal(k, shape, f32)

    page_table = jax.random.permutation(ks[9], n_phys)[:n_used].reshape(DEC_BATCH, n_pages).astype(jnp.int32)
    return {
        'x_prompt': nrm(ks[0], (BATCH, SEQ, D_MODEL)),
        'x_sample': nrm(ks[1], (DEC_BATCH, DEC_SEQ, D_MODEL)),
        'mem_prompt': nrm(ks[2], (BATCH, N_MEM, D_MODEL)),
        'cache_sb_k': nrm(ks[3], (DEPTH, n_phys, PAGE_SIZE, SB_HEADS, SB_HD)),
        'cache_sb_v': nrm(ks[4], (DEPTH, n_phys, PAGE_SIZE, SB_HEADS, SB_HD)),
        'state_hgrn': nrm(ks[5], (DEPTH, DEC_BATCH, HG_HEADS, HG_HD, HG_HD), 0.4),
        'cache_mem_k': nrm(ks[6], (DEPTH, DEC_BATCH, N_MEM, XA_HEADS, XA_HD)),
        'cache_mem_v': nrm(ks[7], (DEPTH, DEC_BATCH, N_MEM, XA_HEADS, XA_HD)),
        'page_table': page_table,
        'norm_mix': gain(ks[10], (DEPTH, D_MODEL)),
        'w_in': nrm(ks[11], (DEPTH, D_MODEL, IN_COLS), D_MODEL ** -0.5),
        'sb_logit_bias': SB_BIAS_INIT + 0.1 * jax.random.normal(ks[28], (DEPTH, SB_HEADS), f32),
        'lb_param': nrm(ks[12], (DEPTH + 1, HG_WIDTH), 0.1),
        'g_sb_out': gain(ks[13], (DEPTH, SB_HD)),
        'g_hg_out': gain(ks[14], (DEPTH, HG_HD)),
        'w_out': nrm(ks[15], (DEPTH, MIX_WIDTH, D_MODEL), MIX_WIDTH ** -0.5),
        'norm_xa': gain(ks[16], (DEPTH, D_MODEL)),
        'norm_mem': gain(ks[17], (DEPTH, D_MODEL)),
        'w_xq': nrm(ks[18], (DEPTH, D_MODEL, XA_WIDTH), D_MODEL ** -0.5),
        'w_xkv': nrm(ks[19], (DEPTH, D_MODEL, 2 * XA_WIDTH), D_MODEL ** -0.5),
        'w_xo': nrm(ks[20], (DEPTH, XA_WIDTH, D_MODEL), XA_WIDTH ** -0.5),
        'norm_ffn': gain(ks[21], (DEPTH, D_MODEL)),
        'w_route_group': nrm(ks[22], (DEPTH, D_MODEL, N_GROUPS), D_MODEL ** -0.5),
        'w_route_expert': nrm(ks[23], (DEPTH, N_GROUPS, D_MODEL, EXP_PER_GROUP), D_MODEL ** -0.5),
        'w_e_gate': nrm(ks[24], (DEPTH, N_EXPERTS, D_MODEL, EXP_FF), D_MODEL ** -0.5),
        'w_e_up': nrm(ks[25], (DEPTH, N_EXPERTS, D_MODEL, EXP_FF), D_MODEL ** -0.5),
        'w_e_down': nrm(ks[26], (DEPTH, N_EXPERTS, EXP_FF, D_MODEL), EXP_FF ** -0.5),
        'norm_final': gain(ks[27], (D_MODEL,)),
    }


def reference(x_prompt, x_sample, mem_prompt, cache_sb_k, cache_sb_v, state_hgrn,
              cache_mem_k, cache_mem_v, page_table, norm_mix, w_in, sb_logit_bias, lb_param,
              g_sb_out, g_hg_out, w_out, norm_xa, norm_mem, w_xq, w_xkv, w_xo,
              norm_ffn, w_route_group, w_route_expert, w_e_gate, w_e_up, w_e_down,
              norm_final):
    lb_all = jnp.cumsum(jax.nn.softmax(lb_param.astype(jnp.float32), axis=0), axis=0)
    hp, hs = x_prompt, x_sample
    sbk_p, sbv_p, hg_p, mk_p, mv_p = [], [], [], [], []
    sbk_s, sbv_s, hg_s = [], [], []
    for l in range(DEPTH):
        y, k_new, v_new, s_new = mixer_prompt(rmsnorm(hp, norm_mix[l]), w_in[l], sb_logit_bias[l],
                                              lb_all[l], g_sb_out[l], g_hg_out[l], w_out[l])
        hp = hp + y
        mk, mv = memory_kv(mem_prompt, norm_mem[l], w_xkv[l])
        hp = hp + cross_attn(rmsnorm(hp, norm_xa[l]), mk, mv, w_xq[l], w_xo[l])
        hp = hp + hier_moe(rmsnorm(hp, norm_ffn[l]), w_route_group[l], w_route_expert[l],
                           w_e_gate[l], w_e_up[l], w_e_down[l])
        sbk_p.append(k_new)
        sbv_p.append(v_new)
        hg_p.append(s_new)
        mk_p.append(mk)
        mv_p.append(mv)
        y, k_new, v_new, s_new = mixer_sample(rmsnorm(hs, norm_mix[l]), cache_sb_k[l], cache_sb_v[l],
                                              page_table, state_hgrn[l], w_in[l], sb_logit_bias[l],
                                              lb_all[l], g_sb_out[l], g_hg_out[l], w_out[l])
        hs = hs + y
        hs = hs + cross_attn(rmsnorm(hs, norm_xa[l]), cache_mem_k[l], cache_mem_v[l], w_xq[l], w_xo[l])
        hs = hs + hier_moe(rmsnorm(hs, norm_ffn[l]), w_route_group[l], w_route_expert[l],
                           w_e_gate[l], w_e_up[l], w_e_down[l])
        sbk_s.append(k_new)
        sbv_s.append(v_new)
        hg_s.append(s_new)
    y_prompt = rmsnorm(hp, norm_final)
    y_sample = rmsnorm(hs, norm_final)
    return (y_prompt, y_sample,
            jnp.stack(sbk_p), jnp.stack(sbv_p), jnp.stack(hg_p), jnp.stack(mk_p), jnp.stack(mv_p),
            jnp.stack(sbk_s), jnp.stack(sbv_s), jnp.stack(hg_s))
```

```python
import functools
import math

import jax
import jax.numpy as jnp
from jax import lax
from jax.experimental import pallas as pl
from jax.experimental.pallas import tpu as pltpu

F32 = jnp.float32
BF16 = jnp.bfloat16
I32 = jnp.int32

D_MODEL = 2048
SEQ = 8192
DEC_BATCH = 128
DEC_SEQ = 4
N_SAMPLE = DEC_BATCH * DEC_SEQ
N_TOK = SEQ + N_SAMPLE
PAGE = 128
N_PAGES = 16
HEADS = 8
HD = 128
WIDTH = HEADS * HD
XA_HEADS = 4
XA_WIDTH = XA_HEADS * HD
N_MEM = 256
N_GROUPS = 4
EXP_PER_GROUP = 8
N_EXPERTS = 32
EXP_FF = 512
HG_CHUNK = 32
EPS = 1e-6
ATT_SCALE = 1.0 / math.sqrt(HD)

LANES = 128
SUBLANES = 8
MIB = 1024 * 1024

TM_IN = 512
TQ = 256
TM_TOK = 256
HG_ROWS = 128
TS = 256
SAMPLE_PAD = 8
PAGES_PER_STEP = 8
N_SLOTS = ((2 * N_TOK + N_EXPERTS * (TS - 1)) + TS - 1) // TS * TS
N_TILES = N_SLOTS // TS


def _cparams(n_axes, vmem_mib):
    return pltpu.CompilerParams(dimension_semantics=("arbitrary",) * n_axes,
                                vmem_limit_bytes=vmem_mib * MIB)


def _dot(a, b):
    return jnp.dot(a, b, preferred_element_type=F32)


def _dot_nt(a, b):
    return lax.dot_general(a, b, (((1,), (1,)), ((), ())), preferred_element_type=F32)


def _rms(x, g):
    ms = jnp.mean(x * x, axis=-1, keepdims=True)
    return x * lax.rsqrt(ms + EPS) * g


def _sigmoid(x):
    return 1.0 / (1.0 + jnp.exp(-x))


def _softplus(z):
    return jnp.maximum(z, 0.0) + jnp.log1p(jnp.exp(-jnp.abs(z)))


def _suffix_matrix(n):
    r = lax.broadcasted_iota(I32, (n, n), 0)
    c = lax.broadcasted_iota(I32, (n, n), 1)
    return jnp.where(r >= c, 1.0, 0.0).astype(BF16)


def _suffix_sum(x, u):
    hi = x.astype(BF16)
    lo = (x - hi.astype(F32)).astype(BF16)
    return _dot(hi, u) + _dot(lo, u)


def _inproj_kernel(xp_ref, xs_ref, g_ref, w_ref,
                   q_ref, kb_ref, vb_ref, kfp_ref, vfp_ref, kfs_ref, vfs_ref, hg_ref,
                   xn_ref, *, n_prompt_blocks):
    i = pl.program_id(0)
    j = pl.program_id(1)
    is_prompt = i < n_prompt_blocks

    @pl.when((j == 0) & is_prompt)
    def _():
        xn_ref[...] = _rms(xp_ref[...], g_ref[...]).astype(BF16)

    @pl.when((j == 0) & jnp.logical_not(is_prompt))
    def _():
        xn_ref[...] = _rms(xs_ref[...], g_ref[...]).astype(BF16)

    p = _dot(xn_ref[...], w_ref[...])

    @pl.when(j == 0)
    def _():
        q_ref[...] = (p * ATT_SCALE).astype(BF16)

    @pl.when(j == 1)
    def _():
        kb_ref[...] = p.astype(BF16)

        @pl.when(is_prompt)
        def _():
            kfp_ref[...] = p

        @pl.when(jnp.logical_not(is_prompt))
        def _():
            kfs_ref[...] = p

    @pl.when(j == 2)
    def _():
        vb_ref[...] = p.astype(BF16)

        @pl.when(is_prompt)
        def _():
            vfp_ref[...] = p

        @pl.when(jnp.logical_not(is_prompt))
        def _():
            vfs_ref[...] = p

    @pl.when(j >= 3)
    def _():
        hg_ref[...] = p


def _inproj(xp, xs, g, w_bf):
    npb = SEQ // TM_IN
    nblk = N_TOK // TM_IN
    ncol = w_bf.shape[1] // WIDTH
    tok_spec = pl.BlockSpec((TM_IN, WIDTH), lambda i, j: (i, 0))
    return pl.pallas_call(
        functools.partial(_inproj_kernel, n_prompt_blocks=npb),
        grid=(nblk, ncol),
        in_specs=[
            pl.BlockSpec((TM_IN, D_MODEL), lambda i, j: (jnp.minimum(i, npb - 1), 0)),
            pl.BlockSpec((TM_IN, D_MODEL), lambda i, j: (jnp.maximum(i - npb, 0), 0)),
            pl.BlockSpec((1, D_MODEL), lambda i, j: (0, 0)),
            pl.BlockSpec((D_MODEL, WIDTH), lambda i, j: (0, j)),
        ],
        out_specs=[
            tok_spec, tok_spec, tok_spec,
            pl.BlockSpec((TM_IN, WIDTH), lambda i, j: (jnp.minimum(i, npb - 1), 0)),
            pl.BlockSpec((TM_IN, WIDTH), lambda i, j: (jnp.minimum(i, npb - 1), 0)),
            pl.BlockSpec((TM_IN, WIDTH), lambda i, j: (jnp.maximum(i - npb, 0), 0)),
            pl.BlockSpec((TM_IN, WIDTH), lambda i, j: (jnp.maximum(i - npb, 0), 0)),
            pl.BlockSpec((TM_IN, WIDTH), lambda i, j: (i, jnp.clip(j - 3, 0, 3))),
        ],
        out_shape=[
            jax.ShapeDtypeStruct((N_TOK, WIDTH), BF16),
            jax.ShapeDtypeStruct((N_TOK, WIDTH), BF16),
            jax.ShapeDtypeStruct((N_TOK, WIDTH), BF16),
            jax.ShapeDtypeStruct((SEQ, WIDTH), F32),
            jax.ShapeDtypeStruct((SEQ, WIDTH), F32),
            jax.ShapeDtypeStruct((N_SAMPLE, WIDTH), F32),
            jax.ShapeDtypeStruct((N_SAMPLE, WIDTH), F32),
            jax.ShapeDtypeStruct((N_TOK, 4 * WIDTH), F32),
        ],
        scratch_shapes=[pltpu.VMEM((TM_IN, D_MODEL), BF16)],
        compiler_params=_cparams(2, 48),
        name="inproj",
    )(xp, xs, g, w_bf)


def _sb_tile(q, kj, vj, bias, u, r_ref, acc_ref, valid):
    z = _dot_nt(q, kj) + bias
    sp = _softplus(z)
    if valid is not None:
        sp = jnp.where(valid, sp, 0.0)
    s_incl = _suffix_sum(sp, u)
    w = jnp.exp(z - s_incl - r_ref[...])
    if valid is not None:
        w = jnp.where(valid, w, 0.0)
    acc_ref[...] += _dot(w.astype(BF16), vj)
    r_ref[...] += s_incl[:, 0:1]


def _head_rms(o, g):
    return o * lax.rsqrt(jnp.mean(o * o, axis=-1, keepdims=True) + EPS) * g


def _sb_prompt_kernel(bias_ref, q_ref, k_ref, v_ref, g_ref, o_ref, acc_ref, r_ref):
    h = pl.program_id(0)
    i = pl.program_id(1)
    bias = bias_ref[h]
    q = q_ref[...]
    u = _suffix_matrix(TQ)
    acc_ref[...] = jnp.zeros_like(acc_ref)
    r_ref[...] = jnp.zeros_like(r_ref)

    def kv(j):
        start = pl.multiple_of(j * TQ, TQ)
        return k_ref[pl.ds(start, TQ), :], v_ref[pl.ds(start, TQ), :]

    row = lax.broadcasted_iota(I32, (TQ, TQ), 0)
    col = lax.broadcasted_iota(I32, (TQ, TQ), 1)
    kd, vd = kv(i)
    _sb_tile(q, kd, vd, bias, u, r_ref, acc_ref, col < row)

    def body(it, carry):
        kj, vj = kv(i - 1 - it)
        _sb_tile(q, kj, vj, bias, u, r_ref, acc_ref, None)
        return carry

    lax.fori_loop(0, i, body, 0)
    o_ref[...] = _head_rms(acc_ref[...], g_ref[...]).astype(BF16)


def _sb_prompt(bias, q_bf, k_bf, v_bf, g):
    return pl.pallas_call(
        _sb_prompt_kernel,
        grid=(HEADS, SEQ // TQ),
        in_specs=[
            pl.BlockSpec(memory_space=pltpu.SMEM),
            pl.BlockSpec((TQ, HD), lambda h, i: (i, h)),
            pl.BlockSpec((SEQ, HD), lambda h, i: (0, h)),
            pl.BlockSpec((SEQ, HD), lambda h, i: (0, h)),
            pl.BlockSpec((1, HD), lambda h, i: (0, 0)),
        ],
        out_specs=pl.BlockSpec((TQ, HD), lambda h, i: (i, h)),
        out_shape=jax.ShapeDtypeStruct((SEQ, WIDTH), BF16),
        scratch_shapes=[pltpu.VMEM((TQ, HD), F32), pltpu.VMEM((TQ, 1), F32)],
        compiler_params=_cparams(2, 32),
        name="sb_prompt",
    )(bias, q_bf, k_bf, v_bf, g)


def _sb_sample_kernel(pt_ref, qrep_ref, brow_ref, knew_ref, vnew_ref, g_ref, *rest):
    k_refs = rest[:PAGES_PER_STEP]
    v_refs = rest[PAGES_PER_STEP:2 * PAGES_PER_STEP]
    o_ref, acc_ref, r_ref, kn_ref, vn_ref = rest[2 * PAGES_PER_STEP:]
    c = pl.program_id(1)
    rows = HEADS * SAMPLE_PAD
    row = lax.broadcasted_iota(I32, (rows, WIDTH), 0)
    col = lax.broadcasted_iota(I32, (rows, WIDTH), 1)
    diag = (row // SAMPLE_PAD) == (col // HD)
    qbd = jnp.where(diag, qrep_ref[0], jnp.zeros((), BF16))
    bias = brow_ref[...]
    u = _suffix_matrix(PAGE)

    @pl.when(c == 0)
    def _():
        acc_ref[...] = jnp.zeros_like(acc_ref)
        r_ref[...] = jnp.zeros_like(r_ref)
        kn_ref[...] = jnp.zeros_like(kn_ref)
        vn_ref[...] = jnp.zeros_like(vn_ref)
        kn_ref[0:DEC_SEQ, :] = knew_ref[0]
        vn_ref[0:DEC_SEQ, :] = vnew_ref[0]
        t = lax.broadcasted_iota(I32, (rows, PAGE), 0) % SAMPLE_PAD
        s = lax.broadcasted_iota(I32, (rows, PAGE), 1)
        _sb_tile(qbd, kn_ref[...].astype(BF16), vn_ref[...].astype(BF16), bias, u, r_ref, acc_ref, s < t)

    def page(ref):
        return jnp.concatenate([ref[0, :, h, :] for h in range(HEADS)], axis=-1).astype(BF16)

    for p in range(PAGES_PER_STEP - 1, -1, -1):
        _sb_tile(qbd, page(k_refs[p]), page(v_refs[p]), bias, u, r_ref, acc_ref, None)

    @pl.when(c == pl.num_programs(1) - 1)
    def _():
        o = jnp.where(diag, acc_ref[...], 0.0)
        ms = jnp.sum(o * o, axis=-1, keepdims=True) * (1.0 / HD)
        o = o * lax.rsqrt(ms + EPS) * g_ref[...]
        out = o[0:SAMPLE_PAD]
        for h in range(1, HEADS):
            out = out + o[h * SAMPLE_PAD:(h + 1) * SAMPLE_PAD]
        o_ref[0] = out.astype(BF16).astype(F32)


def _sb_sample(page_table, qrep, brow, knew, vnew, g_tiled, kcache, vcache):
    n_steps = N_PAGES // PAGES_PER_STEP
    rows = HEADS * SAMPLE_PAD

    def page_spec(p):
        def imap(b, c, pt):
            return (pt[b * N_PAGES + (n_steps - 1 - c) * PAGES_PER_STEP + p], 0, 0, 0)
        return pl.BlockSpec((1, PAGE, HEADS, HD), imap)

    grid_spec = pltpu.PrefetchScalarGridSpec(
        num_scalar_prefetch=1,
        grid=(DEC_BATCH, n_steps),
        in_specs=[
            pl.BlockSpec((1, rows, WIDTH), lambda b, c, pt: (b, 0, 0)),
            pl.BlockSpec((rows, PAGE), lambda b, c, pt: (0, 0)),
            pl.BlockSpec((1, DEC_SEQ, WIDTH), lambda b, c, pt: (b, 0, 0)),
            pl.BlockSpec((1, DEC_SEQ, WIDTH), lambda b, c, pt: (b, 0, 0)),
            pl.BlockSpec((1, WIDTH), lambda b, c, pt: (0, 0)),
        ] + [page_spec(p) for p in range(PAGES_PER_STEP)] * 2,
        out_specs=pl.BlockSpec((1, SAMPLE_PAD, WIDTH), lambda b, c, pt: (b, 0, 0)),
        scratch_shapes=[
            pltpu.VMEM((rows, WIDTH), F32),
            pltpu.VMEM((rows, 1), F32),
            pltpu.VMEM((PAGE, WIDTH), F32),
            pltpu.VMEM((PAGE, WIDTH), F32),
        ],
    )
    return pl.pallas_call(
        _sb_sample_kernel,
        grid_spec=grid_spec,
        out_shape=jax.ShapeDtypeStruct((DEC_BATCH, SAMPLE_PAD, WIDTH), F32),
        compiler_params=_cparams(2, 40),
        name="sb_sample",
    )(page_table, qrep, brow, knew, vnew, g_tiled,
      *([kcache] * PAGES_PER_STEP), *([vcache] * PAGES_PER_STEP))


def _hgrn_block(hq, hf, hi, hgate, lb, g_out, state_in, state_out, *, chunk, last, sequential):
    n_chunks = HG_ROWS // chunk
    one_m_lb = 1.0 - lb
    e = jnp.exp(-jnp.abs(hf))
    inv = 1.0 / (1.0 + e)
    sig_pos = jnp.where(hf >= 0, inv, e * inv)
    sig_neg = jnp.where(hf >= 0, e * inv, inv)
    f = lb + one_m_lb * sig_pos
    k = one_m_lb * sig_neg
    g = jnp.log(f)
    q = hq * _sigmoid(hq)

    rin = lax.broadcasted_iota(I32, (HG_ROWS, WIDTH), 0) % chunk
    b = g
    step = 1
    while step < chunk:
        b = b + jnp.where(rin >= step, pltpu.roll(b, step, axis=0), 0.0)
        step *= 2
    cid = lax.broadcasted_iota(I32, (HG_ROWS, WIDTH), 0) // chunk
    b_last_rows = jnp.zeros_like(b)
    b_last = []
    for c in range(n_chunks):
        bl = b[c * chunk + last:c * chunk + last + 1, :]
        b_last.append(bl)
        b_last_rows = jnp.where(cid == c, bl, b_last_rows)

    q_dec = q * jnp.exp(b)
    k_inv = k * jnp.exp(-b)
    k_dec = k * jnp.exp(b_last_rows - b)

    r2 = lax.broadcasted_iota(I32, (HG_ROWS, HG_ROWS), 0)
    c2 = lax.broadcasted_iota(I32, (HG_ROWS, HG_ROWS), 1)
    intra = ((r2 // chunk) == (c2 // chunk)) & (c2 <= r2)
    eye = r2 == c2
    outs = []
    for h in range(HEADS):
        sl = slice(h * HD, (h + 1) * HD)
        qd_h = q_dec[:, sl]
        v_h = hi[:, sl].astype(BF16)
        a = jnp.where(intra, _dot_nt(qd_h.astype(BF16), k_inv[:, sl].astype(BF16)), 0.0)
        o_h = _dot(a.astype(BF16), v_h)
        kdec_t = jnp.transpose(k_dec[:, sl])
        s_cur = state_in(0, h) if sequential else None
        for c in range(n_chunks):
            if not sequential:
                s_cur = state_in(c, h)
            in_rows = (r2 // chunk) == c
            in_cols = (c2 // chunk) == c
            o_h = o_h + _dot(jnp.where(in_rows, qd_h, 0.0).astype(BF16), s_cur.astype(BF16))
            dl = jnp.exp(b_last[c][:, sl])
            dcol = jnp.sum(jnp.where(eye, dl, 0.0), axis=1, keepdims=True)
            s_new = dcol * s_cur + _dot(jnp.where(in_cols, kdec_t, 0.0).astype(BF16), v_h)
            if sequential:
                s_cur = s_new
            else:
                state_out(c, h, s_new)
        if sequential:
            state_out(0, h, s_cur)
        gate = hgate[:, sl]
        outs.append((_head_rms(o_h, g_out) * (gate * _sigmoid(gate))).astype(BF16))
    return jnp.concatenate(outs, axis=-1)


def _hgrn_prompt_kernel(hq_ref, hf_ref, hi_ref, hg_ref, lb_ref, g_ref, o_ref, sfin_ref, s_ref):
    i = pl.program_id(0)

    @pl.when(i == 0)
    def _():
        s_ref[...] = jnp.zeros_like(s_ref)

    def state_in(c, h):
        return s_ref[h]

    def state_out(c, h, s):
        s_ref[h] = s

    o_ref[...] = _hgrn_block(hq_ref[...], hf_ref[...], hi_ref[...], hg_ref[...], lb_ref[...], g_ref[...],
                             state_in, state_out, chunk=HG_CHUNK, last=HG_CHUNK - 1, sequential=True)

    @pl.when(i == pl.num_programs(0) - 1)
    def _():
        sfin_ref[...] = s_ref[...]


def _hgrn_prompt(hgin, lb, g):
    def col(j):
        return pl.BlockSpec((HG_ROWS, WIDTH), lambda i: (i, j))
    return pl.pallas_call(
        _hgrn_prompt_kernel,
        grid=(SEQ // HG_ROWS,),
        in_specs=[col(0), col(1), col(2), col(3),
                  pl.BlockSpec((1, WIDTH), lambda i: (0, 0)),
                  pl.BlockSpec((1, HD), lambda i: (0, 0))],
        out_specs=[pl.BlockSpec((HG_ROWS, WIDTH), lambda i: (i, 0)),
                   pl.BlockSpec((HEADS, HD, HD), lambda i: (0, 0, 0))],
        out_shape=[jax.ShapeDtypeStruct((SEQ, WIDTH), BF16),
                   jax.ShapeDtypeStruct((HEADS, HD, HD), F32)],
        scratch_shapes=[pltpu.VMEM((HEADS, HD, HD), F32)],
        compiler_params=_cparams(1, 32),
        name="hgrn_prompt",
    )(hgin, hgin, hgin, hgin, lb, g)


def _hgrn_sample_kernel(hq_ref, hf_ref, hi_ref, hg_ref, lb_ref, g_ref, s0_ref, o_ref, sout_ref):
    def state_in(c, h):
        return s0_ref[c, h]

    def state_out(c, h, s):
        sout_ref[c, h] = s

    o_ref[...] = _hgrn_block(hq_ref[...], hf_ref[...], hi_ref[...], hg_ref[...], lb_ref[...], g_ref[...],
                             state_in, state_out, chunk=SAMPLE_PAD, last=DEC_SEQ - 1, sequential=False)


def _hgrn_sample(hgin_pad, lb, g, s0):
    bpb = HG_ROWS // SAMPLE_PAD

    def col(j):
        return pl.BlockSpec((HG_ROWS, WIDTH), lambda i: (i, j))
    return pl.pallas_call(
        _hgrn_sample_kernel,
        grid=(DEC_BATCH // bpb,),
        in_specs=[col(0), col(1), col(2), col(3),
                  pl.BlockSpec((1, WIDTH), lambda i: (0, 0)),
                  pl.BlockSpec((1, HD), lambda i: (0, 0)),
                  pl.BlockSpec((bpb, HEADS, HD, HD), lambda i: (i, 0, 0, 0))],
        out_specs=[pl.BlockSpec((HG_ROWS, WIDTH), lambda i: (i, 0)),
                   pl.BlockSpec((bpb, HEADS, HD, HD), lambda i: (i, 0, 0, 0))],
        out_shape=[jax.ShapeDtypeStruct((DEC_BATCH * SAMPLE_PAD, WIDTH), BF16),
                   jax.ShapeDtypeStruct((DEC_BATCH, HEADS, HD, HD), F32)],
        compiler_params=_cparams(1, 48),
        name="hgrn_sample",
    )(hgin_pad, hgin_pad, hgin_pad, hgin_pad, lb, g, s0)


def _post_mixer_kernel(a_ref, b_ref, xp_ref, xs_ref, wo_ref, gx_ref, wq_ref, h1_ref, qx_ref, *, n_prompt_blocks):
    i = pl.program_id(0)
    y = _dot(a_ref[...], wo_ref[0:WIDTH, :]) + _dot(b_ref[...], wo_ref[WIDTH:2 * WIDTH, :])

    def finish(x):
        h1 = x + y
        h1_ref[...] = h1
        qx_ref[...] = (_dot(_rms(h1, gx_ref[...]).astype(BF16), wq_ref[...]) * ATT_SCALE).astype(BF16)

    @pl.when(i < n_prompt_blocks)
    def _():
        finish(xp_ref[...])

    @pl.when(i >= n_prompt_blocks)
    def _():
        finish(xs_ref[...])


def _post_mixer(a, b, xp, xs, wo_bf, gx, wq_bf):
    npb = SEQ // TM_TOK
    return pl.pallas_call(
        functools.partial(_post_mixer_kernel, n_prompt_blocks=npb),
        grid=(N_TOK // TM_TOK,),
        in_specs=[
            pl.BlockSpec((TM_TOK, WIDTH), lambda i: (i, 0)),
            pl.BlockSpec((TM_TOK, WIDTH), lambda i: (i, 0)),
            pl.BlockSpec((TM_TOK, D_MODEL), lambda i: (jnp.minimum(i, npb - 1), 0)),
            pl.BlockSpec((TM_TOK, D_MODEL), lambda i: (jnp.maximum(i - npb, 0), 0)),
            pl.BlockSpec((2 * WIDTH, D_MODEL), lambda i: (0, 0)),
            pl.BlockSpec((1, D_MODEL), lambda i: (0, 0)),
            pl.BlockSpec((D_MODEL, XA_WIDTH), lambda i: (0, 0)),
        ],
        out_specs=[pl.BlockSpec((TM_TOK, D_MODEL), lambda i: (i, 0)),
                   pl.BlockSpec((TM_TOK, XA_WIDTH), lambda i: (i, 0))],
        out_shape=[jax.ShapeDtypeStruct((N_TOK, D_MODEL), F32),
                   jax.ShapeDtypeStruct((N_TOK, XA_WIDTH), BF16)],
        compiler_params=_cparams(1, 48),
        name="post_mixer",
    )(a, b, xp, xs, wo_bf, gx, wq_bf)


def _mem_kv_kernel(m_ref, g_ref, w_ref, mk_ref, mv_ref):
    kv = _dot(_rms(m_ref[...], g_ref[...]).astype(BF16), w_ref[...])
    mk_ref[...] = kv[:, 0:XA_WIDTH]
    mv_ref[...] = kv[:, XA_WIDTH:2 * XA_WIDTH]


def _mem_kv(mem, g, w_bf):
    return pl.pallas_call(
        _mem_kv_kernel,
        out_shape=[jax.ShapeDtypeStruct((N_MEM, XA_WIDTH), F32)] * 2,
        compiler_params=pltpu.CompilerParams(vmem_limit_bytes=32 * MIB),
        name="mem_kv",
    )(mem, g, w_bf)


def _softmax_rows(s):
    m = jnp.max(s, axis=-1, keepdims=True)
    e = jnp.exp(s - m)
    return e / jnp.sum(e, axis=-1, keepdims=True)


def _xattn_prompt_kernel(q_ref, mk_ref, mv_ref, o_ref):
    mk = mk_ref[...].astype(BF16)
    mv = mv_ref[...].astype(BF16)
    outs = []
    for h in range(XA_HEADS):
        sl = slice(h * HD, (h + 1) * HD)
        p = _softmax_rows(_dot_nt(q_ref[:, sl], mk[:, sl]))
        outs.append(_dot(p.astype(BF16), mv[:, sl]))
    o_ref[...] = jnp.concatenate(outs, axis=-1).astype(BF16)


def _xattn_prompt(qx, mk, mv):
    return pl.pallas_call(
        _xattn_prompt_kernel,
        grid=(SEQ // TM_TOK,),
        in_specs=[pl.BlockSpec((TM_TOK, XA_WIDTH), lambda i: (i, 0)),
                  pl.BlockSpec((N_MEM, XA_WIDTH), lambda i: (0, 0)),
                  pl.BlockSpec((N_MEM, XA_WIDTH), lambda i: (0, 0))],
        out_specs=pl.BlockSpec((TM_TOK, XA_WIDTH), lambda i: (i, 0)),
        out_shape=jax.ShapeDtypeStruct((SEQ, XA_WIDTH), BF16),
        compiler_params=_cparams(1, 32),
        name="xattn_prompt",
    )(qx, mk, mv)


XA_BATCHES_PER_STEP = 8


def _xattn_sample_kernel(qrep_ref, mk_ref, mv_ref, o_ref):
    rows = XA_HEADS * SAMPLE_PAD
    row = lax.broadcasted_iota(I32, (rows, XA_WIDTH), 0)
    col = lax.broadcasted_iota(I32, (rows, XA_WIDTH), 1)
    diag = (row // SAMPLE_PAD) == (col // HD)
    for bb in range(XA_BATCHES_PER_STEP):
        qbd = jnp.where(diag, qrep_ref[bb], jnp.zeros((), BF16))
        mk = jnp.concatenate([mk_ref[bb, :, h, :] for h in range(XA_HEADS)], axis=-1).astype(BF16)
        mv = jnp.concatenate([mv_ref[bb, :, h, :] for h in range(XA_HEADS)], axis=-1).astype(BF16)
        p = _softmax_rows(_dot_nt(qbd, mk))
        o = jnp.where(diag, _dot(p.astype(BF16), mv), 0.0)
        out = o[0:SAMPLE_PAD]
        for h in range(1, XA_HEADS):
            out = out + o[h * SAMPLE_PAD:(h + 1) * SAMPLE_PAD]
        o_ref[bb] = out.astype(BF16).astype(F32)


def _xattn_sample(qrep, mk, mv):
    rows = XA_HEADS * SAMPLE_PAD
    nb = XA_BATCHES_PER_STEP
    return pl.pallas_call(
        _xattn_sample_kernel,
        grid=(DEC_BATCH // nb,),
        in_specs=[pl.BlockSpec((nb, rows, XA_WIDTH), lambda i: (i, 0, 0)),
                  pl.BlockSpec((nb, N_MEM, XA_HEADS, HD), lambda i: (i, 0, 0, 0)),
                  pl.BlockSpec((nb, N_MEM, XA_HEADS, HD), lambda i: (i, 0, 0, 0))],
        out_specs=pl.BlockSpec((nb, SAMPLE_PAD, XA_WIDTH), lambda i: (i, 0, 0)),
        out_shape=jax.ShapeDtypeStruct((DEC_BATCH, SAMPLE_PAD, XA_WIDTH), F32),
        compiler_params=_cparams(1, 40),
        name="xattn_sample",
    )(qrep, mk, mv)


ROUTE_LANES = LANES


def _router_kernel(ox_ref, h1_ref, wxo_ref, gf_ref, wr_hi_ref, wr_lo_ref, h2_ref, xn_ref, route_ref):
    h2 = h1_ref[...] + _dot(ox_ref[...], wxo_ref[...])
    h2_ref[...] = h2
    xn = _rms(h2, gf_ref[...])
    xn_ref[...] = xn
    x_hi = xn.astype(BF16)
    x_lo = (xn - x_hi.astype(F32)).astype(BF16)
    logits = _dot(x_hi, wr_hi_ref[...]) + (_dot(x_lo, wr_hi_ref[...]) + _dot(x_hi, wr_lo_ref[...]))

    lane = lax.broadcasted_iota(I32, logits.shape, 1).astype(F32)
    neg = jnp.float32(-jnp.inf)
    big = jnp.float32(ROUTE_LANES)
    lg = jnp.where(lane < N_GROUPS, logits, neg)
    gmax = jnp.max(lg, axis=-1, keepdims=True)
    g_idx = jnp.min(jnp.where(lg == gmax, lane, big), axis=-1, keepdims=True)
    g_w = 1.0 / jnp.sum(jnp.exp(lg - gmax), axis=-1, keepdims=True)
    lo_lane = N_GROUPS + g_idx * EXP_PER_GROUP
    le = jnp.where((lane >= lo_lane) & (lane < lo_lane + EXP_PER_GROUP), logits, neg)
    v1 = jnp.max(le, axis=-1, keepdims=True)
    i1 = jnp.min(jnp.where(le == v1, lane, big), axis=-1, keepdims=True)
    le2 = jnp.where(lane == i1, neg, le)
    v2 = jnp.max(le2, axis=-1, keepdims=True)
    i2 = jnp.min(jnp.where(le2 == v2, lane, big), axis=-1, keepdims=True)
    e2 = jnp.exp(v2 - v1)
    w1 = g_w / (1.0 + e2)
    w2 = g_w * e2 / (1.0 + e2)
    route = jnp.where(lane == 0, (i1 - N_GROUPS).astype(F32),
                      jnp.where(lane == 1, (i2 - N_GROUPS).astype(F32),
                                jnp.where(lane == 2, w1, jnp.where(lane == 3, w2, 0.0))))
    route_ref[...] = route


def _router(ox, h1, wxo_bf, gf, wr_hi, wr_lo):
    return pl.pallas_call(
        _router_kernel,
        grid=(N_TOK // TM_TOK,),
        in_specs=[pl.BlockSpec((TM_TOK, XA_WIDTH), lambda i: (i, 0)),
                  pl.BlockSpec((TM_TOK, D_MODEL), lambda i: (i, 0)),
                  pl.BlockSpec((XA_WIDTH, D_MODEL), lambda i: (0, 0)),
                  pl.BlockSpec((1, D_MODEL), lambda i: (0, 0)),
                  pl.BlockSpec((D_MODEL, ROUTE_LANES), lambda i: (0, 0)),
                  pl.BlockSpec((D_MODEL, ROUTE_LANES), lambda i: (0, 0))],
        out_specs=[pl.BlockSpec((TM_TOK, D_MODEL), lambda i: (i, 0)),
                   pl.BlockSpec((TM_TOK, D_MODEL), lambda i: (i, 0)),
                   pl.BlockSpec((TM_TOK, ROUTE_LANES), lambda i: (i, 0))],
        out_shape=[jax.ShapeDtypeStruct((N_TOK, D_MODEL), F32),
                   jax.ShapeDtypeStruct((N_TOK, D_MODEL), F32),
                   jax.ShapeDtypeStruct((N_TOK, ROUTE_LANES), F32)],
        compiler_params=_cparams(1, 40),
        name="router",
    )(ox, h1, wxo_bf, gf, wr_hi, wr_lo)


def _moe_kernel(te_ref, nused_ref, tv_ref, src_ref, dst_ref,
                x_hbm, sw_ref, wg_ref, wu_ref, wd_ref,
                y_hbm,
                xbuf, ybuf, wg_bf, wu_bf, wd_bf, gsem, ssem):
    i = pl.program_id(0)
    n_used = nused_ref[0]

    def gather_rows(tile, slot):
        base = tile * TS

        def body(r, carry):
            tok = src_ref[base + r]
            pltpu.make_async_copy(x_hbm.at[pl.ds(tok, 1)], xbuf.at[slot, pl.ds(r, 1)], gsem.at[slot]).start()
            return carry
        lax.fori_loop(0, TS, body, 0)

    def scatter_rows(tile, slot):
        base = tile * TS

        def body(r, carry):
            row = dst_ref[base + r]
            pltpu.make_async_copy(ybuf.at[slot, pl.ds(r, 1)], y_hbm.at[pl.ds(row, 1)], ssem.at[slot]).start()
            return carry
        lax.fori_loop(0, tv_ref[tile], body, 0)

    def wait_gather(slot):
        pltpu.make_async_copy(x_hbm.at[pl.ds(0, TS)], xbuf.at[slot], gsem.at[slot]).wait()

    def wait_scatter(tile, slot):
        def body(r, carry):
            pltpu.make_async_copy(ybuf.at[slot, pl.ds(0, 1)], y_hbm.at[pl.ds(0, 1)], ssem.at[slot]).wait()
            return carry
        lax.fori_loop(0, tv_ref[tile], body, 0)

    @pl.when(i < n_used)
    def _():
        slot = i % 2

        @pl.when(i == 0)
        def _():
            gather_rows(0, 0)

        @pl.when(i + 1 < n_used)
        def _():
            gather_rows(i + 1, 1 - slot)

        new_expert = jnp.logical_or(i == 0, te_ref[i] != te_ref[jnp.maximum(i - 1, 0)])

        @pl.when(new_expert)
        def _():
            wg_bf[...] = wg_ref[0].astype(BF16)
            wu_bf[...] = wu_ref[0].astype(BF16)
            wd_bf[...] = wd_ref[0].astype(BF16)

        wait_gather(slot)

        @pl.when(i >= 2)
        def _():
            wait_scatter(i - 2, slot)

        x = xbuf[slot].astype(BF16)
        a = _dot(x, wg_bf[...])
        u = _dot(x, wu_bf[...])
        hm = (a * _sigmoid(a)) * u * sw_ref[...]
        ybuf[slot] = _dot(hm.astype(BF16), wd_bf[...])
        scatter_rows(i, slot)

        @pl.when(i == n_used - 1)
        def _():
            @pl.when(i >= 1)
            def _():
                wait_scatter(i - 1, 1 - slot)
            wait_scatter(i, slot)


def _moe(tile_expert, n_used, tile_valid, src_tok, dst_row, xn, slot_w, wg, wu, wd):
    grid_spec = pltpu.PrefetchScalarGridSpec(
        num_scalar_prefetch=5,
        grid=(N_TILES,),
        in_specs=[
            pl.BlockSpec(memory_space=pl.ANY),
            pl.BlockSpec((TS, 1), lambda i, te, nu, tv, s, d: (i, 0)),
            pl.BlockSpec((1, D_MODEL, EXP_FF), lambda i, te, nu, tv, s, d: (te[i], 0, 0)),
            pl.BlockSpec((1, D_MODEL, EXP_FF), lambda i, te, nu, tv, s, d: (te[i], 0, 0)),
            pl.BlockSpec((1, EXP_FF, D_MODEL), lambda i, te, nu, tv, s, d: (te[i], 0, 0)),
        ],
        out_specs=pl.BlockSpec(memory_space=pl.ANY),
        scratch_shapes=[
            pltpu.VMEM((2, TS, D_MODEL), F32),
            pltpu.VMEM((2, TS, D_MODEL), F32),
            pltpu.VMEM((D_MODEL, EXP_FF), BF16),
            pltpu.VMEM((D_MODEL, EXP_FF), BF16),
            pltpu.VMEM((EXP_FF, D_MODEL), BF16),
            pltpu.SemaphoreType.DMA((2,)),
            pltpu.SemaphoreType.DMA((2,)),
        ],
    )
    return pl.pallas_call(
        _moe_kernel,
        grid_spec=grid_spec,
        out_shape=jax.ShapeDtypeStruct((2 * N_TOK, D_MODEL), F32),
        compiler_params=_cparams(1, 52),
        name="moe",
    )(tile_expert, n_used, tile_valid, src_tok, dst_row, xn, slot_w, wg, wu, wd)


def _final_kernel(h2_ref, y0_ref, y1_ref, g_ref, yp_ref, ys_ref, *, n_prompt_blocks):
    i = pl.program_id(0)
    y = _rms(h2_ref[...] + (y0_ref[0] + y1_ref[0]), g_ref[...])

    @pl.when(i < n_prompt_blocks)
    def _():
        yp_ref[...] = y

    @pl.when(i >= n_prompt_blocks)
    def _():
        ys_ref[...] = y


def _final(h2, ypair, g):
    npb = SEQ // TM_TOK
    return pl.pallas_call(
        functools.partial(_final_kernel, n_prompt_blocks=npb),
        grid=(N_TOK // TM_TOK,),
        in_specs=[pl.BlockSpec((TM_TOK, D_MODEL), lambda i: (i, 0)),
                  pl.BlockSpec((1, TM_TOK, D_MODEL), lambda i: (0, i, 0)),
                  pl.BlockSpec((1, TM_TOK, D_MODEL), lambda i: (1, i, 0)),
                  pl.BlockSpec((1, D_MODEL), lambda i: (0, 0))],
        out_specs=[pl.BlockSpec((TM_TOK, D_MODEL), lambda i: (jnp.minimum(i, npb - 1), 0)),
                   pl.BlockSpec((TM_TOK, D_MODEL), lambda i: (jnp.maximum(i - npb, 0), 0))],
        out_shape=[jax.ShapeDtypeStruct((SEQ, D_MODEL), F32),
                   jax.ShapeDtypeStruct((N_SAMPLE, D_MODEL), F32)],
        compiler_params=_cparams(1, 40),
        name="final",
    )(h2, ypair, ypair, g)


def _routing_tables(route):
    eid = route[:, 0:2].astype(I32).reshape(-1)
    wts = route[:, 2:4].reshape(-1)
    onehot = (eid[:, None] == jnp.arange(N_EXPERTS, dtype=I32)[None, :]).astype(I32)
    csum = jnp.cumsum(onehot, axis=0)
    rank = jnp.take_along_axis(csum - onehot, eid[:, None], axis=1)[:, 0]
    counts = csum[-1]
    padded = (counts + TS - 1) // TS * TS
    seg_end = jnp.cumsum(padded)
    seg_start = seg_end - padded
    slot = seg_start[eid] + rank
    a = jnp.arange(2 * N_TOK, dtype=I32)
    src_tok = jnp.zeros((N_SLOTS,), I32).at[slot].set(a // 2)
    slot_w = jnp.zeros((N_SLOTS,), F32).at[slot].set(wts)
    dst_row = jnp.zeros((N_SLOTS,), I32).at[slot].set((a % 2) * N_TOK + a // 2)
    n_used = (seg_end[-1] // TS).astype(I32)
    tile_start = jnp.arange(N_TILES, dtype=I32) * TS
    tile_expert = jnp.minimum(jnp.searchsorted(seg_end, tile_start, side="right"), N_EXPERTS - 1).astype(I32)
    last_expert = tile_expert[jnp.maximum(n_used - 1, 0)]
    tile_expert = jnp.where(jnp.arange(N_TILES) < n_used, tile_expert, last_expert)
    tile_valid = jnp.clip(seg_start[tile_expert] + counts[tile_expert] - tile_start, 0, TS).astype(I32)
    tile_valid = jnp.where(jnp.arange(N_TILES) < n_used, tile_valid, 0)
    return tile_expert, n_used.reshape(1), tile_valid, src_tok, dst_row, slot_w.reshape(N_SLOTS, 1)


def _rep_heads(x, n_heads):
    xpad = jnp.pad(x, ((0, 0), (0, SAMPLE_PAD - DEC_SEQ), (0, 0)))
    return jnp.tile(xpad, (1, n_heads, 1))


def kernel(x_prompt, x_sample, mem_prompt, cache_sb_k, cache_sb_v, state_hgrn, cache_mem_k, cache_mem_v,
           page_table, norm_mix, w_in, sb_logit_bias, lb_param, g_sb_out, g_hg_out, w_out, norm_xa, norm_mem,
           w_xq, w_xkv, w_xo, norm_ffn, w_route_group, w_route_expert, w_e_gate, w_e_up, w_e_down, norm_final):
    xp = x_prompt.reshape(SEQ, D_MODEL)
    xs = x_sample.reshape(N_SAMPLE, D_MODEL)

    lb = jnp.cumsum(jax.nn.softmax(lb_param.astype(F32), axis=0), axis=0)[0].reshape(1, WIDTH)
    w_in_bf = w_in[0].astype(BF16)
    w_out_bf = w_out[0].astype(BF16)
    w_xq_bf = w_xq[0].astype(BF16)
    w_xkv_bf = w_xkv[0].astype(BF16)
    w_xo_bf = w_xo[0].astype(BF16)
    w_r = jnp.concatenate([w_route_group[0],
                           jnp.transpose(w_route_expert[0], (1, 0, 2)).reshape(D_MODEL, N_EXPERTS)], axis=1)
    w_r = jnp.pad(w_r, ((0, 0), (0, ROUTE_LANES - w_r.shape[1])))
    w_r_hi = w_r.astype(BF16)
    w_r_lo = (w_r - w_r_hi.astype(F32)).astype(BF16)
    bias = sb_logit_bias[0].astype(F32)

    q_bf, k_bf, v_bf, kf_p, vf_p, kf_s, vf_s, hgin = _inproj(xp, xs, norm_mix[0].reshape(1, D_MODEL), w_in_bf)
    a_p = _sb_prompt(bias, q_bf, k_bf, v_bf, g_sb_out[0].reshape(1, HD))
    qrep = _rep_heads(q_bf[SEQ:].reshape(DEC_BATCH, DEC_SEQ, WIDTH), HEADS)
    brow = jnp.broadcast_to(jnp.repeat(bias, SAMPLE_PAD)[:, None], (HEADS * SAMPLE_PAD, PAGE))
    a_s = _sb_sample(page_table.reshape(-1).astype(I32), qrep, brow,
                     kf_s.reshape(DEC_BATCH, DEC_SEQ, WIDTH), vf_s.reshape(DEC_BATCH, DEC_SEQ, WIDTH),
                     jnp.tile(g_sb_out[0], HEADS).reshape(1, WIDTH),
                     cache_sb_k[0], cache_sb_v[0])
    g_hg = g_hg_out[0].reshape(1, HD)
    b_p, s_p = _hgrn_prompt(hgin, lb, g_hg)
    hgin_s = jnp.pad(hgin[SEQ:].reshape(DEC_BATCH, DEC_SEQ, 4 * WIDTH),
                     ((0, 0), (0, SAMPLE_PAD - DEC_SEQ), (0, 0))).reshape(DEC_BATCH * SAMPLE_PAD, 4 * WIDTH)
    b_s, s_s = _hgrn_sample(hgin_s, lb, g_hg, state_hgrn[0])
    a_all = jnp.concatenate([a_p, a_s[:, :DEC_SEQ].reshape(N_SAMPLE, WIDTH).astype(BF16)], axis=0)
    b_all = jnp.concatenate([b_p, b_s.reshape(DEC_BATCH, SAMPLE_PAD, WIDTH)[:, :DEC_SEQ].reshape(N_SAMPLE, WIDTH)],
                            axis=0)
    h1, qx = _post_mixer(a_all, b_all, xp, xs, w_out_bf, norm_xa[0].reshape(1, D_MODEL), w_xq_bf)

    mk_p, mv_p = _mem_kv(mem_prompt[0], norm_mem[0].reshape(1, D_MODEL), w_xkv_bf)
    ox_p = _xattn_prompt(qx, mk_p, mv_p)
    qxrep = _rep_heads(qx[SEQ:].reshape(DEC_BATCH, DEC_SEQ, XA_WIDTH), XA_HEADS)
    ox_s = _xattn_sample(qxrep, cache_mem_k[0], cache_mem_v[0])
    ox = jnp.concatenate([ox_p, ox_s[:, :DEC_SEQ].reshape(N_SAMPLE, XA_WIDTH).astype(BF16)], axis=0)
    h2, xn2, route = _router(ox, h1, w_xo_bf, norm_ffn[0].reshape(1, D_MODEL), w_r_hi, w_r_lo)

    tile_expert, n_used, tile_valid, src_tok, dst_row, slot_w = _routing_tables(route)
    ypair = _moe(tile_expert, n_used, tile_valid, src_tok, dst_row, xn2, slot_w,
                 w_e_gate[0], w_e_up[0], w_e_down[0])
    y_p, y_s = _final(h2, ypair.reshape(2, N_TOK, D_MODEL), norm_final.reshape(1, D_MODEL))

    sb_shape_p = (1, 1, SEQ, HEADS, HD)
    sb_shape_s = (1, DEC_BATCH, DEC_SEQ, HEADS, HD)
    return (y_p.reshape(1, SEQ, D_MODEL), y_s.reshape(DEC_BATCH, DEC_SEQ, D_MODEL),
            kf_p.reshape(sb_shape_p), vf_p.reshape(sb_shape_p), s_p.reshape(1, 1, HEADS, HD, HD),
            mk_p.reshape(1, 1, N_MEM, XA_HEADS, HD), mv_p.reshape(1, 1, N_MEM, XA_HEADS, HD),
            kf_s.reshape(sb_shape_s), vf_s.reshape(sb_shape_s), s_s.reshape(1, DEC_BATCH, HEADS, HD, HD))
```

```python
import functools
import math

import jax
import jax.numpy as jnp
from jax import lax
from jax.experimental import pallas as pl
from jax.experimental.pallas import tpu as pltpu

F32 = jnp.float32
BF16 = jnp.bfloat16
I32 = jnp.int32

D_MODEL = 2048
SEQ = 8192
DEC_BATCH = 128
DEC_SEQ = 4
N_SAMPLE = DEC_BATCH * DEC_SEQ
N_TOK = SEQ + N_SAMPLE
PAGE = 128
N_PAGES = 16
HEADS = 8
HD = 128
WIDTH = HEADS * HD
XA_HEADS = 4
XA_WIDTH = XA_HEADS * HD
N_MEM = 256
N_GROUPS = 4
EXP_PER_GROUP = 8
N_EXPERTS = 32
EXP_FF = 512
HG_CHUNK = 32
EPS = 1e-6
ATT_SCALE = 1.0 / math.sqrt(HD)

LANES = 128
SUBLANES = 8
MIB = 1024 * 1024

TM_IN = 512
TQ = 256
SB_HEADS_PER_STEP = 4
TM_TOK = 256
HG_ROWS = 128
TS = 256
SAMPLE_PAD = 8
PAGES_PER_STEP = 8
N_SLOTS = ((2 * N_TOK + N_EXPERTS * (TS - 1)) + TS - 1) // TS * TS
N_TILES = N_SLOTS // TS


def _cparams(n_axes, vmem_mib):
    return pltpu.CompilerParams(dimension_semantics=("arbitrary",) * n_axes,
                                vmem_limit_bytes=vmem_mib * MIB)


def _dot(a, b):
    return jnp.dot(a, b, preferred_element_type=F32)


def _dot_nt(a, b):
    return lax.dot_general(a, b, (((1,), (1,)), ((), ())), preferred_element_type=F32)


def _rms(x, g):
    ms = jnp.mean(x * x, axis=-1, keepdims=True)
    return x * lax.rsqrt(ms + EPS) * g


def _sigmoid(x):
    return 1.0 / (1.0 + jnp.exp(-x))


def _softplus(z):
    return jnp.maximum(z, 0.0) + jnp.log(1.0 + jnp.exp(-jnp.abs(z)))


def _suffix_matrix(n):
    r = lax.broadcasted_iota(I32, (n, n), 0)
    c = lax.broadcasted_iota(I32, (n, n), 1)
    return jnp.where(r >= c, 1.0, 0.0).astype(BF16)


def _split_bf16(x):
    hi = x.astype(BF16)
    lo = (x - hi.astype(F32)).astype(BF16)
    return hi, lo


def _inproj_kernel(xp_ref, xs_ref, g_ref, w_ref,
                   q_ref, kb_ref, vb_ref, kfp_ref, vfp_ref, kfs_ref, vfs_ref, hg_ref,
                   xn_ref, *, n_prompt_blocks):
    i = pl.program_id(0)
    j = pl.program_id(1)
    is_prompt = i < n_prompt_blocks

    @pl.when((j == 0) & is_prompt)
    def _():
        xn_ref[...] = _rms(xp_ref[...], g_ref[...]).astype(BF16)

    @pl.when((j == 0) & jnp.logical_not(is_prompt))
    def _():
        xn_ref[...] = _rms(xs_ref[...], g_ref[...]).astype(BF16)

    p = _dot(xn_ref[...], w_ref[...])

    @pl.when(j == 0)
    def _():
        q_ref[...] = (p * ATT_SCALE).astype(BF16)

    @pl.when(j == 1)
    def _():
        kb_ref[...] = p.astype(BF16)

        @pl.when(is_prompt)
        def _():
            kfp_ref[...] = p

        @pl.when(jnp.logical_not(is_prompt))
        def _():
            kfs_ref[...] = p

    @pl.when(j == 2)
    def _():
        vb_ref[...] = p.astype(BF16)

        @pl.when(is_prompt)
        def _():
            vfp_ref[...] = p

        @pl.when(jnp.logical_not(is_prompt))
        def _():
            vfs_ref[...] = p

    @pl.when(j >= 3)
    def _():
        hg_ref[...] = p


def _inproj(xp, xs, g, w_bf):
    npb = SEQ // TM_IN
    nblk = N_TOK // TM_IN
    ncol = w_bf.shape[1] // WIDTH
    tok_spec = pl.BlockSpec((TM_IN, WIDTH), lambda i, j: (i, 0))
    return pl.pallas_call(
        functools.partial(_inproj_kernel, n_prompt_blocks=npb),
        grid=(nblk, ncol),
        in_specs=[
            pl.BlockSpec((TM_IN, D_MODEL), lambda i, j: (jnp.minimum(i, npb - 1), 0)),
            pl.BlockSpec((TM_IN, D_MODEL), lambda i, j: (jnp.maximum(i - npb, 0), 0)),
            pl.BlockSpec((1, D_MODEL), lambda i, j: (0, 0)),
            pl.BlockSpec((D_MODEL, WIDTH), lambda i, j: (0, j)),
        ],
        out_specs=[
            tok_spec, tok_spec, tok_spec,
            pl.BlockSpec((TM_IN, WIDTH), lambda i, j: (jnp.minimum(i, npb - 1), 0)),
            pl.BlockSpec((TM_IN, WIDTH), lambda i, j: (jnp.minimum(i, npb - 1), 0)),
            pl.BlockSpec((TM_IN, WIDTH), lambda i, j: (jnp.maximum(i - npb, 0), 0)),
            pl.BlockSpec((TM_IN, WIDTH), lambda i, j: (jnp.maximum(i - npb, 0), 0)),
            pl.BlockSpec((TM_IN, WIDTH), lambda i, j: (i, jnp.clip(j - 3, 0, 3))),
        ],
        out_shape=[
            jax.ShapeDtypeStruct((N_TOK, WIDTH), BF16),
            jax.ShapeDtypeStruct((N_TOK, WIDTH), BF16),
            jax.ShapeDtypeStruct((N_TOK, WIDTH), BF16),
            jax.ShapeDtypeStruct((SEQ, WIDTH), F32),
            jax.ShapeDtypeStruct((SEQ, WIDTH), F32),
            jax.ShapeDtypeStruct((N_SAMPLE, WIDTH), F32),
            jax.ShapeDtypeStruct((N_SAMPLE, WIDTH), F32),
            jax.ShapeDtypeStruct((N_TOK, 4 * WIDTH), F32),
        ],
        scratch_shapes=[pltpu.VMEM((TM_IN, D_MODEL), BF16)],
        compiler_params=_cparams(2, 48),
        name="inproj",
    )(xp, xs, g, w_bf)


def _sb_tiles(qs, ks, vs, biases, valids, u, r_refs, acc_refs):
    n = u.shape[0]
    zs = [_dot_nt(q, k) + b for q, k, b in zip(qs, ks, biases)]
    sps = []
    for z, valid in zip(zs, valids):
        sp = _softplus(z)
        sps.append(sp if valid is None else jnp.where(valid, sp, 0.0))
    pieces = [[_split_bf16(sp[:, c:c + n]) for c in range(0, sp.shape[1], n)] for sp in sps]
    sums = [[_dot(hi, u) + _dot(lo, u) for hi, lo in blocks] for blocks in pieces]

    carried = {}
    ws = []
    for z, valid, blocks, r_ref in zip(zs, valids, sums, r_refs):
        r = carried[id(r_ref)][1] if id(r_ref) in carried else r_ref[...]
        parts = [None] * len(blocks)
        for blk in range(len(blocks) - 1, -1, -1):
            parts[blk] = blocks[blk] + r
            r = r + blocks[blk][:, 0:1]
        carried[id(r_ref)] = (r_ref, r)
        s_all = parts[0] if len(parts) == 1 else jnp.concatenate(parts, axis=-1)
        w = jnp.exp(z - s_all)
        ws.append(w if valid is None else jnp.where(valid, w, 0.0))
    for r_ref, r in carried.values():
        r_ref[...] = r

    totals = {}
    for w, v, acc_ref in zip(ws, vs, acc_refs):
        d = _dot(w.astype(BF16), v)
        totals[id(acc_ref)] = (acc_ref, d + totals[id(acc_ref)][1] if id(acc_ref) in totals else d)
    for acc_ref, d in totals.values():
        acc_ref[...] += d


def _head_rms(o, g):
    return o * lax.rsqrt(jnp.mean(o * o, axis=-1, keepdims=True) + EPS) * g


def _sb_prompt_kernel(bias_ref, q_ref, k_ref, v_ref, g_ref, o_ref, acc_ref, r_ref):
    hg = pl.program_id(0)
    i = pl.program_id(1)
    u = _suffix_matrix(TQ)
    acc_ref[...] = jnp.zeros_like(acc_ref)
    r_ref[...] = jnp.zeros_like(r_ref)

    heads = range(SB_HEADS_PER_STEP)
    lanes = [slice(g * HD, (g + 1) * HD) for g in heads]
    biases = [bias_ref[hg * SB_HEADS_PER_STEP + g] for g in heads]
    r_refs = [r_ref.at[g] for g in heads]
    acc_refs = [acc_ref.at[g] for g in heads]

    def tiles(j, valid):
        start = pl.multiple_of(j * TQ, TQ)
        _sb_tiles([q_ref[:, sl] for sl in lanes],
                  [k_ref[pl.ds(start, TQ), sl] for sl in lanes],
                  [v_ref[pl.ds(start, TQ), sl] for sl in lanes],
                  biases, [valid] * SB_HEADS_PER_STEP, u, r_refs, acc_refs)

    row = lax.broadcasted_iota(I32, (TQ, TQ), 0)
    col = lax.broadcasted_iota(I32, (TQ, TQ), 1)
    tiles(i, col < row)

    def body(it, carry):
        tiles(i - 1 - it, None)
        return carry

    lax.fori_loop(0, i, body, 0)
    for g in range(SB_HEADS_PER_STEP):
        o_ref[:, g * HD:(g + 1) * HD] = _head_rms(acc_ref[g], g_ref[...]).astype(BF16)


def _sb_prompt(bias, q_bf, k_bf, v_bf, g):
    gw = SB_HEADS_PER_STEP * HD
    return pl.pallas_call(
        _sb_prompt_kernel,
        grid=(HEADS // SB_HEADS_PER_STEP, SEQ // TQ),
        in_specs=[
            pl.BlockSpec(memory_space=pltpu.SMEM),
            pl.BlockSpec((TQ, gw), lambda h, i: (i, h)),
            pl.BlockSpec((SEQ, gw), lambda h, i: (0, h)),
            pl.BlockSpec((SEQ, gw), lambda h, i: (0, h)),
            pl.BlockSpec((1, HD), lambda h, i: (0, 0)),
        ],
        out_specs=pl.BlockSpec((TQ, gw), lambda h, i: (i, h)),
        out_shape=jax.ShapeDtypeStruct((SEQ, WIDTH), BF16),
        scratch_shapes=[pltpu.VMEM((SB_HEADS_PER_STEP, TQ, HD), F32),
                        pltpu.VMEM((SB_HEADS_PER_STEP, TQ, 1), F32)],
        compiler_params=_cparams(2, 48),
        name="sb_prompt",
    )(bias, q_bf, k_bf, v_bf, g)


SB_ROWS = HEADS * DEC_SEQ
PAGE_COLS = PAGE * HEADS
NEW_COLS = LANES
SUFFIX_BLOCK = 256


def _sb_sample_kernel(pt_ref, q_ref, brow_ref, knew_ref, vnew_ref, g_ref, *rest):
    k_refs = rest[:PAGES_PER_STEP]
    v_refs = rest[PAGES_PER_STEP:2 * PAGES_PER_STEP]
    o_ref, acc_ref, r_ref, kn_ref, vn_ref = rest[2 * PAGES_PER_STEP:]
    c = pl.program_id(1)
    q = q_ref[0]
    row = lax.broadcasted_iota(I32, (SB_ROWS, PAGE_COLS), 0)
    col = lax.broadcasted_iota(I32, (SB_ROWS, PAGE_COLS), 1)
    own = (col % HEADS) == (row // DEC_SEQ)
    bias = brow_ref[...]

    @pl.when(c == 0)
    def _():
        acc_ref[...] = jnp.zeros_like(acc_ref)
        r_ref[...] = jnp.zeros_like(r_ref)
        kn_ref[...] = jnp.zeros_like(kn_ref)
        vn_ref[...] = jnp.zeros_like(vn_ref)
        kn_ref[0:SB_ROWS, :] = knew_ref[0]
        vn_ref[0:SB_ROWS, :] = vnew_ref[0]
        rn = lax.broadcasted_iota(I32, (SB_ROWS, NEW_COLS), 0)
        cn = lax.broadcasted_iota(I32, (SB_ROWS, NEW_COLS), 1)
        valid = ((cn % HEADS) == (rn // DEC_SEQ)) & ((cn // HEADS) < (rn % DEC_SEQ))
        _sb_tiles([q], [kn_ref[...].astype(BF16)], [vn_ref[...].astype(BF16)], [bias[:, 0:NEW_COLS]], [valid],
                  _suffix_matrix(NEW_COLS), [r_ref], [acc_ref])

    order = range(PAGES_PER_STEP - 1, -1, -1)
    _sb_tiles([q] * PAGES_PER_STEP,
              [k_refs[p][0].reshape(PAGE_COLS, HD).astype(BF16) for p in order],
              [v_refs[p][0].reshape(PAGE_COLS, HD).astype(BF16) for p in order],
              [bias] * PAGES_PER_STEP, [own] * PAGES_PER_STEP, _suffix_matrix(SUFFIX_BLOCK),
              [r_ref] * PAGES_PER_STEP, [acc_ref] * PAGES_PER_STEP)

    @pl.when(c == pl.num_programs(1) - 1)
    def _():
        o_ref[0] = _head_rms(acc_ref[...], g_ref[...]).astype(BF16).astype(F32)


def _sb_sample(page_table, q_rows, brow, knew, vnew, g, kcache, vcache):
    n_steps = N_PAGES // PAGES_PER_STEP

    def page_spec(p):
        def imap(b, c, pt):
            return (pt[b * N_PAGES + (n_steps - 1 - c) * PAGES_PER_STEP + p], 0, 0, 0)
        return pl.BlockSpec((1, PAGE, HEADS, HD), imap)

    grid_spec = pltpu.PrefetchScalarGridSpec(
        num_scalar_prefetch=1,
        grid=(DEC_BATCH, n_steps),
        in_specs=[
            pl.BlockSpec((1, SB_ROWS, HD), lambda b, c, pt: (b, 0, 0)),
            pl.BlockSpec((SB_ROWS, PAGE_COLS), lambda b, c, pt: (0, 0)),
            pl.BlockSpec((1, SB_ROWS, HD), lambda b, c, pt: (b, 0, 0)),
            pl.BlockSpec((1, SB_ROWS, HD), lambda b, c, pt: (b, 0, 0)),
            pl.BlockSpec((1, HD), lambda b, c, pt: (0, 0)),
        ] + [page_spec(p) for p in range(PAGES_PER_STEP)] * 2,
        out_specs=pl.BlockSpec((1, SB_ROWS, HD), lambda b, c, pt: (b, 0, 0)),
        scratch_shapes=[
            pltpu.VMEM((SB_ROWS, HD), F32),
            pltpu.VMEM((SB_ROWS, 1), F32),
            pltpu.VMEM((NEW_COLS, HD), F32),
            pltpu.VMEM((NEW_COLS, HD), F32),
        ],
    )
    return pl.pallas_call(
        _sb_sample_kernel,
        grid_spec=grid_spec,
        out_shape=jax.ShapeDtypeStruct((DEC_BATCH, SB_ROWS, HD), F32),
        compiler_params=_cparams(2, 40),
        name="sb_sample",
    )(page_table, q_rows, brow, knew, vnew, g,
      *([kcache] * PAGES_PER_STEP), *([vcache] * PAGES_PER_STEP))


def _hgrn_block(hq, hf, hi, hgate, lb, g_out, state_in, state_out, *, chunk, last, sequential):
    n_chunks = HG_ROWS // chunk
    one_m_lb = 1.0 - lb
    e = jnp.exp(-jnp.abs(hf))
    inv = 1.0 / (1.0 + e)
    sig_pos = jnp.where(hf >= 0, inv, e * inv)
    sig_neg = jnp.where(hf >= 0, e * inv, inv)
    f = lb + one_m_lb * sig_pos
    k = one_m_lb * sig_neg
    g = jnp.log(f)
    q = hq * _sigmoid(hq)

    rin = lax.broadcasted_iota(I32, (HG_ROWS, WIDTH), 0) % chunk
    b = g
    step = 1
    while step < chunk:
        b = b + jnp.where(rin >= step, pltpu.roll(b, step, axis=0), 0.0)
        step *= 2
    cid = lax.broadcasted_iota(I32, (HG_ROWS, WIDTH), 0) // chunk
    b_last_rows = jnp.zeros_like(b)
    b_last = []
    for c in range(n_chunks):
        bl = b[c * chunk + last:c * chunk + last + 1, :]
        b_last.append(bl)
        b_last_rows = jnp.where(cid == c, bl, b_last_rows)

    q_dec = q * jnp.exp(b)
    k_inv = k * jnp.exp(-b)
    k_dec = k * jnp.exp(b_last_rows - b)

    r2 = lax.broadcasted_iota(I32, (HG_ROWS, HG_ROWS), 0)
    c2 = lax.broadcasted_iota(I32, (HG_ROWS, HG_ROWS), 1)
    intra = ((r2 // chunk) == (c2 // chunk)) & (c2 <= r2)
    eye = r2 == c2
    outs = []
    for h in range(HEADS):
        sl = slice(h * HD, (h + 1) * HD)
        qd_h = q_dec[:, sl]
        v_h = hi[:, sl].astype(BF16)
        a = jnp.where(intra, _dot_nt(qd_h.astype(BF16), k_inv[:, sl].astype(BF16)), 0.0)
        o_h = _dot(a.astype(BF16), v_h)
        kdec_t = jnp.transpose(k_dec[:, sl])
        s_cur = state_in(0, h) if sequential else None
        for c in range(n_chunks):
            if not sequential:
                s_cur = state_in(c, h)
            in_rows = (r2 // chunk) == c
            in_cols = (c2 // chunk) == c
            o_h = o_h + _dot(jnp.where(in_rows, qd_h, 0.0).astype(BF16), s_cur.astype(BF16))
            dl = jnp.exp(b_last[c][:, sl])
            dcol = jnp.sum(jnp.where(eye, dl, 0.0), axis=1, keepdims=True)
            s_new = dcol * s_cur + _dot(jnp.where(in_cols, kdec_t, 0.0).astype(BF16), v_h)
            if sequential:
                s_cur = s_new
            else:
                state_out(c, h, s_new)
        if sequential:
            state_out(0, h, s_cur)
        gate = hgate[:, sl]
        outs.append((_head_rms(o_h, g_out) * (gate * _sigmoid(gate))).astype(BF16))
    return jnp.concatenate(outs, axis=-1)


def _hgrn_prompt_kernel(hq_ref, hf_ref, hi_ref, hg_ref, lb_ref, g_ref, o_ref, sfin_ref, s_ref):
    i = pl.program_id(0)

    @pl.when(i == 0)
    def _():
        s_ref[...] = jnp.zeros_like(s_ref)

    def state_in(c, h):
        return s_ref[h]

    def state_out(c, h, s):
        s_ref[h] = s

    o_ref[...] = _hgrn_block(hq_ref[...], hf_ref[...], hi_ref[...], hg_ref[...], lb_ref[...], g_ref[...],
                             state_in, state_out, chunk=HG_CHUNK, last=HG_CHUNK - 1, sequential=True)

    @pl.when(i == pl.num_programs(0) - 1)
    def _():
        sfin_ref[...] = s_ref[...]


def _hgrn_prompt(hgin, lb, g):
    def col(j):
        return pl.BlockSpec((HG_ROWS, WIDTH), lambda i: (i, j))
    return pl.pallas_call(
        _hgrn_prompt_kernel,
        grid=(SEQ // HG_ROWS,),
        in_specs=[col(0), col(1), col(2), col(3),
                  pl.BlockSpec((1, WIDTH), lambda i: (0, 0)),
                  pl.BlockSpec((1, HD), lambda i: (0, 0))],
        out_specs=[pl.BlockSpec((HG_ROWS, WIDTH), lambda i: (i, 0)),
                   pl.BlockSpec((HEADS, HD, HD), lambda i: (0, 0, 0))],
        out_shape=[jax.ShapeDtypeStruct((SEQ, WIDTH), BF16),
                   jax.ShapeDtypeStruct((HEADS, HD, HD), F32)],
        scratch_shapes=[pltpu.VMEM((HEADS, HD, HD), F32)],
        compiler_params=_cparams(1, 32),
        name="hgrn_prompt",
    )(hgin, hgin, hgin, hgin, lb, g)


def _hgrn_sample_kernel(hq_ref, hf_ref, hi_ref, hg_ref, lb_ref, g_ref, s0_ref, o_ref, sout_ref):
    def state_in(c, h):
        return s0_ref[c, h]

    def state_out(c, h, s):
        sout_ref[c, h] = s

    o_ref[...] = _hgrn_block(hq_ref[...], hf_ref[...], hi_ref[...], hg_ref[...], lb_ref[...], g_ref[...],
                             state_in, state_out, chunk=SAMPLE_PAD, last=DEC_SEQ - 1, sequential=False)


def _hgrn_sample(hgin_pad, lb, g, s0):
    bpb = HG_ROWS // SAMPLE_PAD

    def col(j):
        return pl.BlockSpec((HG_ROWS, WIDTH), lambda i: (i, j))
    return pl.pallas_call(
        _hgrn_sample_kernel,
        grid=(DEC_BATCH // bpb,),
        in_specs=[col(0), col(1), col(2), col(3),
                  pl.BlockSpec((1, WIDTH), lambda i: (0, 0)),
                  pl.BlockSpec((1, HD), lambda i: (0, 0)),
                  pl.BlockSpec((bpb, HEADS, HD, HD), lambda i: (i, 0, 0, 0))],
        out_specs=[pl.BlockSpec((HG_ROWS, WIDTH), lambda i: (i, 0)),
                   pl.BlockSpec((bpb, HEADS, HD, HD), lambda i: (i, 0, 0, 0))],
        out_shape=[jax.ShapeDtypeStruct((DEC_BATCH * SAMPLE_PAD, WIDTH), BF16),
                   jax.ShapeDtypeStruct((DEC_BATCH, HEADS, HD, HD), F32)],
        compiler_params=_cparams(1, 48),
        name="hgrn_sample",
    )(hgin_pad, hgin_pad, hgin_pad, hgin_pad, lb, g, s0)


def _post_mixer_kernel(a_ref, b_ref, xp_ref, xs_ref, wo_ref, gx_ref, wq_ref, h1_ref, qx_ref, *, n_prompt_blocks):
    i = pl.program_id(0)
    y = _dot(a_ref[...], wo_ref[0:WIDTH, :]) + _dot(b_ref[...], wo_ref[WIDTH:2 * WIDTH, :])

    def finish(x):
        h1 = x + y
        h1_ref[...] = h1
        qx_ref[...] = (_dot(_rms(h1, gx_ref[...]).astype(BF16), wq_ref[...]) * ATT_SCALE).astype(BF16)

    @pl.when(i < n_prompt_blocks)
    def _():
        finish(xp_ref[...])

    @pl.when(i >= n_prompt_blocks)
    def _():
        finish(xs_ref[...])


def _post_mixer(a, b, xp, xs, wo_bf, gx, wq_bf):
    npb = SEQ // TM_TOK
    return pl.pallas_call(
        functools.partial(_post_mixer_kernel, n_prompt_blocks=npb),
        grid=(N_TOK // TM_TOK,),
        in_specs=[
            pl.BlockSpec((TM_TOK, WIDTH), lambda i: (i, 0)),
            pl.BlockSpec((TM_TOK, WIDTH), lambda i: (i, 0)),
            pl.BlockSpec((TM_TOK, D_MODEL), lambda i: (jnp.minimum(i, npb - 1), 0)),
            pl.BlockSpec((TM_TOK, D_MODEL), lambda i: (jnp.maximum(i - npb, 0), 0)),
            pl.BlockSpec((2 * WIDTH, D_MODEL), lambda i: (0, 0)),
            pl.BlockSpec((1, D_MODEL), lambda i: (0, 0)),
            pl.BlockSpec((D_MODEL, XA_WIDTH), lambda i: (0, 0)),
        ],
        out_specs=[pl.BlockSpec((TM_TOK, D_MODEL), lambda i: (i, 0)),
                   pl.BlockSpec((TM_TOK, XA_WIDTH), lambda i: (i, 0))],
        out_shape=[jax.ShapeDtypeStruct((N_TOK, D_MODEL), F32),
                   jax.ShapeDtypeStruct((N_TOK, XA_WIDTH), BF16)],
        compiler_params=_cparams(1, 48),
        name="post_mixer",
    )(a, b, xp, xs, wo_bf, gx, wq_bf)


def _mem_kv_kernel(m_ref, g_ref, w_ref, mk_ref, mv_ref):
    kv = _dot(_rms(m_ref[...], g_ref[...]).astype(BF16), w_ref[...])
    mk_ref[...] = kv[:, 0:XA_WIDTH]
    mv_ref[...] = kv[:, XA_WIDTH:2 * XA_WIDTH]


def _mem_kv(mem, g, w_bf):
    return pl.pallas_call(
        _mem_kv_kernel,
        out_shape=[jax.ShapeDtypeStruct((N_MEM, XA_WIDTH), F32)] * 2,
        compiler_params=pltpu.CompilerParams(vmem_limit_bytes=32 * MIB),
        name="mem_kv",
    )(mem, g, w_bf)


def _softmax_rows(s):
    m = jnp.max(s, axis=-1, keepdims=True)
    e = jnp.exp(s - m)
    return e / jnp.sum(e, axis=-1, keepdims=True)


def _xattn_prompt_kernel(q_ref, mk_ref, mv_ref, o_ref):
    mk = mk_ref[...].astype(BF16)
    mv = mv_ref[...].astype(BF16)
    outs = []
    for h in range(XA_HEADS):
        sl = slice(h * HD, (h + 1) * HD)
        p = _softmax_rows(_dot_nt(q_ref[:, sl], mk[:, sl]))
        outs.append(_dot(p.astype(BF16), mv[:, sl]))
    o_ref[...] = jnp.concatenate(outs, axis=-1).astype(BF16)


def _xattn_prompt(qx, mk, mv):
    return pl.pallas_call(
        _xattn_prompt_kernel,
        grid=(SEQ // TM_TOK,),
        in_specs=[pl.BlockSpec((TM_TOK, XA_WIDTH), lambda i: (i, 0)),
                  pl.BlockSpec((N_MEM, XA_WIDTH), lambda i: (0, 0)),
                  pl.BlockSpec((N_MEM, XA_WIDTH), lambda i: (0, 0))],
        out_specs=pl.BlockSpec((TM_TOK, XA_WIDTH), lambda i: (i, 0)),
        out_shape=jax.ShapeDtypeStruct((SEQ, XA_WIDTH), BF16),
        compiler_params=_cparams(1, 32),
        name="xattn_prompt",
    )(qx, mk, mv)


XA_BATCHES_PER_STEP = 8


def _xattn_sample_kernel(qrep_ref, mk_ref, mv_ref, o_ref):
    rows = XA_HEADS * SAMPLE_PAD
    row = lax.broadcasted_iota(I32, (rows, XA_WIDTH), 0)
    col = lax.broadcasted_iota(I32, (rows, XA_WIDTH), 1)
    diag = (row // SAMPLE_PAD) == (col // HD)
    for bb in range(XA_BATCHES_PER_STEP):
        qbd = jnp.where(diag, qrep_ref[bb], jnp.zeros((), BF16))
        mk = jnp.concatenate([mk_ref[bb, :, h, :] for h in range(XA_HEADS)], axis=-1).astype(BF16)
        mv = jnp.concatenate([mv_ref[bb, :, h, :] for h in range(XA_HEADS)], axis=-1).astype(BF16)
        p = _softmax_rows(_dot_nt(qbd, mk))
        o = jnp.where(diag, _dot(p.astype(BF16), mv), 0.0)
        out = o[0:SAMPLE_PAD]
        for h in range(1, XA_HEADS):
            out = out + o[h * SAMPLE_PAD:(h + 1) * SAMPLE_PAD]
        o_ref[bb] = out.astype(BF16).astype(F32)


def _xattn_sample(qrep, mk, mv):
    rows = XA_HEADS * SAMPLE_PAD
    nb = XA_BATCHES_PER_STEP
    return pl.pallas_call(
        _xattn_sample_kernel,
        grid=(DEC_BATCH // nb,),
        in_specs=[pl.BlockSpec((nb, rows, XA_WIDTH), lambda i: (i, 0, 0)),
                  pl.BlockSpec((nb, N_MEM, XA_HEADS, HD), lambda i: (i, 0, 0, 0)),
                  pl.BlockSpec((nb, N_MEM, XA_HEADS, HD), lambda i: (i, 0, 0, 0))],
        out_specs=pl.BlockSpec((nb, SAMPLE_PAD, XA_WIDTH), lambda i: (i, 0, 0)),
        out_shape=jax.ShapeDtypeStruct((DEC_BATCH, SAMPLE_PAD, XA_WIDTH), F32),
        compiler_params=_cparams(1, 40),
        name="xattn_sample",
    )(qrep, mk, mv)


ROUTE_LANES = LANES


def _router_kernel(ox_ref, h1_ref, wxo_ref, gf_ref, wr_hi_ref, wr_lo_ref, h2_ref, xn_ref, route_ref):
    h2 = h1_ref[...] + _dot(ox_ref[...], wxo_ref[...])
    h2_ref[...] = h2
    xn = _rms(h2, gf_ref[...])
    xn_ref[...] = xn
    x_hi = xn.astype(BF16)
    x_lo = (xn - x_hi.astype(F32)).astype(BF16)
    logits = _dot(x_hi, wr_hi_ref[...]) + (_dot(x_lo, wr_hi_ref[...]) + _dot(x_hi, wr_lo_ref[...]))

    lane = lax.broadcasted_iota(I32, logits.shape, 1).astype(F32)
    neg = jnp.float32(-jnp.inf)
    big = jnp.float32(ROUTE_LANES)
    lg = jnp.where(lane < N_GROUPS, logits, neg)
    gmax = jnp.max(lg, axis=-1, keepdims=True)
    g_idx = jnp.min(jnp.where(lg == gmax, lane, big), axis=-1, keepdims=True)
    g_w = 1.0 / jnp.sum(jnp.exp(lg - gmax), axis=-1, keepdims=True)
    lo_lane = N_GROUPS + g_idx * EXP_PER_GROUP
    le = jnp.where((lane >= lo_lane) & (lane < lo_lane + EXP_PER_GROUP), logits, neg)
    v1 = jnp.max(le, axis=-1, keepdims=True)
    i1 = jnp.min(jnp.where(le == v1, lane, big), axis=-1, keepdims=True)
    le2 = jnp.where(lane == i1, neg, le)
    v2 = jnp.max(le2, axis=-1, keepdims=True)
    i2 = jnp.min(jnp.where(le2 == v2, lane, big), axis=-1, keepdims=True)
    e2 = jnp.exp(v2 - v1)
    w1 = g_w / (1.0 + e2)
    w2 = g_w * e2 / (1.0 + e2)
    route = jnp.where(lane == 0, (i1 - N_GROUPS).astype(F32),
                      jnp.where(lane == 1, (i2 - N_GROUPS).astype(F32),
                                jnp.where(lane == 2, w1, jnp.where(lane == 3, w2, 0.0))))
    route_ref[...] = route


def _router(ox, h1, wxo_bf, gf, wr_hi, wr_lo):
    return pl.pallas_call(
        _router_kernel,
        grid=(N_TOK // TM_TOK,),
        in_specs=[pl.BlockSpec((TM_TOK, XA_WIDTH), lambda i: (i, 0)),
                  pl.BlockSpec((TM_TOK, D_MODEL), lambda i: (i, 0)),
                  pl.BlockSpec((XA_WIDTH, D_MODEL), lambda i: (0, 0)),
                  pl.BlockSpec((1, D_MODEL), lambda i: (0, 0)),
                  pl.BlockSpec((D_MODEL, ROUTE_LANES), lambda i: (0, 0)),
                  pl.BlockSpec((D_MODEL, ROUTE_LANES), lambda i: (0, 0))],
        out_specs=[pl.BlockSpec((TM_TOK, D_MODEL), lambda i: (i, 0)),
                   pl.BlockSpec((TM_TOK, D_MODEL), lambda i: (i, 0)),
                   pl.BlockSpec((TM_TOK, ROUTE_LANES), lambda i: (i, 0))],
        out_shape=[jax.ShapeDtypeStruct((N_TOK, D_MODEL), F32),
                   jax.ShapeDtypeStruct((N_TOK, D_MODEL), F32),
                   jax.ShapeDtypeStruct((N_TOK, ROUTE_LANES), F32)],
        compiler_params=_cparams(1, 40),
        name="router",
    )(ox, h1, wxo_bf, gf, wr_hi, wr_lo)


DISPATCH_ROWS = 512
N_ASSIGN = 2 * N_TOK


def _dispatch_kernel(slot_ref, cnt_ref, pad_ref, start_ref, nused_ref,
                     x_hbm, xs_hbm, zrow, sems):
    i = pl.program_id(0)
    n = pl.num_programs(0)
    sem = sems.at[0]
    zsem = sems.at[1]

    def wait_chunk():
        pltpu.make_async_copy(x_hbm.at[pl.ds(0, DISPATCH_ROWS)], xs_hbm.at[pl.ds(0, DISPATCH_ROWS)], sem).wait()

    @pl.when(i == 0)
    def _():
        zrow[...] = jnp.zeros_like(zrow)

        def pad_rows(fn):
            def per_expert(e, carry):
                base = start_ref[e]

                def body(r, c2):
                    fn(base + r)
                    return c2
                lax.fori_loop(cnt_ref[e], pad_ref[e], body, 0)
                return carry
            lax.fori_loop(0, N_EXPERTS, per_expert, 0)

        pad_rows(lambda row: pltpu.make_async_copy(zrow.at[pl.ds(0, 1)], xs_hbm.at[pl.ds(row, 1)], zsem).start())
        pad_rows(lambda row: pltpu.make_async_copy(zrow.at[pl.ds(0, 1)], xs_hbm.at[pl.ds(0, 1)], zsem).wait())

        def unused_tiles(fn):
            def body(t, carry):
                fn(pl.multiple_of(t * TS, TS))
                return carry
            lax.fori_loop(nused_ref[0], N_TILES, body, 0)

        unused_tiles(lambda r0: pltpu.make_async_copy(zrow, xs_hbm.at[pl.ds(r0, TS)], zsem).start())
        unused_tiles(lambda r0: pltpu.make_async_copy(zrow, xs_hbm.at[pl.ds(0, TS)], zsem).wait())

    base = i * DISPATCH_ROWS

    def body(r, carry):
        a = base + r
        tok = lax.shift_right_logical(a, 1)
        pltpu.make_async_copy(x_hbm.at[pl.ds(tok, 1)], xs_hbm.at[pl.ds(slot_ref[a], 1)], sem).start()
        return carry
    lax.fori_loop(0, DISPATCH_ROWS, body, 0, unroll=8)

    @pl.when(i >= 1)
    def _():
        wait_chunk()

    @pl.when(i == n - 1)
    def _():
        wait_chunk()


def _dispatch(slot, counts, padded, seg_start, n_used, xn):
    grid_spec = pltpu.PrefetchScalarGridSpec(
        num_scalar_prefetch=5,
        grid=(N_ASSIGN // DISPATCH_ROWS,),
        in_specs=[pl.BlockSpec(memory_space=pl.ANY)],
        out_specs=pl.BlockSpec(memory_space=pl.ANY),
        scratch_shapes=[pltpu.VMEM((TS, D_MODEL), F32),
                        pltpu.SemaphoreType.DMA((2,))],
    )
    return pl.pallas_call(
        _dispatch_kernel,
        grid_spec=grid_spec,
        out_shape=jax.ShapeDtypeStruct((N_SLOTS, D_MODEL), F32),
        compiler_params=_cparams(1, 16),
        name="dispatch",
    )(slot, counts, padded, seg_start, n_used, xn)


def _moe_kernel(te_ref, nused_ref, xs_ref, wg_ref, wu_ref, wd_ref, ys_ref, wg_bf, wu_bf, wd_bf):
    i = pl.program_id(0)
    active = i < nused_ref[0]

    @pl.when(active)
    def _():
        new_expert = jnp.logical_or(i == 0, te_ref[i] != te_ref[jnp.maximum(i - 1, 0)])

        @pl.when(new_expert)
        def _():
            wg_bf[...] = wg_ref[0].astype(BF16)
            wu_bf[...] = wu_ref[0].astype(BF16)
            wd_bf[...] = wd_ref[0].astype(BF16)

        x = xs_ref[...].astype(BF16)
        a = _dot(x, wg_bf[...])
        u = _dot(x, wu_bf[...])
        hm = (a * _sigmoid(a)) * u
        ys_ref[...] = _dot(hm.astype(BF16), wd_bf[...])

    @pl.when(jnp.logical_not(active))
    def _():
        ys_ref[...] = jnp.zeros_like(ys_ref)


def _moe(tile_expert, n_used, xs, wg, wu, wd):
    grid_spec = pltpu.PrefetchScalarGridSpec(
        num_scalar_prefetch=2,
        grid=(N_TILES,),
        in_specs=[
            pl.BlockSpec((TS, D_MODEL), lambda i, te, nu: (jnp.minimum(i, nu[0] - 1), 0)),
            pl.BlockSpec((1, D_MODEL, EXP_FF), lambda i, te, nu: (te[i], 0, 0)),
            pl.BlockSpec((1, D_MODEL, EXP_FF), lambda i, te, nu: (te[i], 0, 0)),
            pl.BlockSpec((1, EXP_FF, D_MODEL), lambda i, te, nu: (te[i], 0, 0)),
        ],
        out_specs=pl.BlockSpec((TS, D_MODEL), lambda i, te, nu: (i, 0)),
        scratch_shapes=[
            pltpu.VMEM((D_MODEL, EXP_FF), BF16),
            pltpu.VMEM((D_MODEL, EXP_FF), BF16),
            pltpu.VMEM((EXP_FF, D_MODEL), BF16),
        ],
    )
    return pl.pallas_call(
        _moe_kernel,
        grid_spec=grid_spec,
        out_shape=jax.ShapeDtypeStruct((N_SLOTS, D_MODEL), F32),
        compiler_params=_cparams(1, 48),
        name="moe",
    )(tile_expert, n_used, xs, wg, wu, wd)


def _final_kernel(slot_ref, h2_ref, route_ref, g_ref, ys_hbm, yp_ref, ysm_ref, ybuf, sem, *, n_prompt_blocks):
    i = pl.program_id(0)
    n = pl.num_programs(0)
    buf = i % 2

    def gather(blk, b):
        base = blk * (2 * TM_TOK)

        def body(r, carry):
            s = slot_ref[base + r]
            k = r & 1
            t = lax.shift_right_logical(r, 1)
            pltpu.make_async_copy(ys_hbm.at[pl.ds(s, 1)], ybuf.at[b, k, pl.ds(t, 1)], sem.at[b]).start()
            return carry
        lax.fori_loop(0, 2 * TM_TOK, body, 0, unroll=8)

    @pl.when(i == 0)
    def _():
        gather(0, 0)

    @pl.when(i + 1 < n)
    def _():
        gather(i + 1, 1 - buf)

    for k in range(2):
        pltpu.make_async_copy(ys_hbm.at[pl.ds(0, TM_TOK)], ybuf.at[buf, k], sem.at[buf]).wait()

    lane = lax.broadcasted_iota(I32, (TM_TOK, ROUTE_LANES), 1)
    route = route_ref[...]
    w0 = jnp.sum(jnp.where(lane == 2, route, 0.0), axis=-1, keepdims=True)
    w1 = jnp.sum(jnp.where(lane == 3, route, 0.0), axis=-1, keepdims=True)
    y = _rms(h2_ref[...] + (w0 * ybuf[buf, 0] + w1 * ybuf[buf, 1]), g_ref[...])

    @pl.when(i < n_prompt_blocks)
    def _():
        yp_ref[...] = y

    @pl.when(i >= n_prompt_blocks)
    def _():
        ysm_ref[...] = y


def _final(slot, h2, route, g, ys):
    npb = SEQ // TM_TOK
    grid_spec = pltpu.PrefetchScalarGridSpec(
        num_scalar_prefetch=1,
        grid=(N_TOK // TM_TOK,),
        in_specs=[pl.BlockSpec((TM_TOK, D_MODEL), lambda i, s: (i, 0)),
                  pl.BlockSpec((TM_TOK, ROUTE_LANES), lambda i, s: (i, 0)),
                  pl.BlockSpec((1, D_MODEL), lambda i, s: (0, 0)),
                  pl.BlockSpec(memory_space=pl.ANY)],
        out_specs=[pl.BlockSpec((TM_TOK, D_MODEL), lambda i, s: (jnp.minimum(i, npb - 1), 0)),
                   pl.BlockSpec((TM_TOK, D_MODEL), lambda i, s: (jnp.maximum(i - npb, 0), 0))],
        scratch_shapes=[pltpu.VMEM((2, 2, TM_TOK, D_MODEL), F32),
                        pltpu.SemaphoreType.DMA((2,))],
    )
    return pl.pallas_call(
        functools.partial(_final_kernel, n_prompt_blocks=npb),
        grid_spec=grid_spec,
        out_shape=[jax.ShapeDtypeStruct((SEQ, D_MODEL), F32),
                   jax.ShapeDtypeStruct((N_SAMPLE, D_MODEL), F32)],
        compiler_params=_cparams(1, 40),
        name="final",
    )(slot, h2, route, g, ys)


def _routing_tables(route):
    eid = route[:, 0:2].astype(I32).reshape(-1)
    onehot = (eid[:, None] == jnp.arange(N_EXPERTS, dtype=I32)[None, :]).astype(I32)
    csum = jnp.cumsum(onehot, axis=0)
    rank = jnp.sum(onehot * (csum - onehot), axis=1)
    counts = csum[-1]
    padded = (counts + TS - 1) // TS * TS
    seg_end = jnp.cumsum(padded)
    seg_start = seg_end - padded
    slot = jnp.sum(onehot * seg_start[None, :], axis=1) + rank
    n_used = (seg_end[-1] // TS).astype(I32)
    tile_start = jnp.arange(N_TILES, dtype=I32) * TS
    tile_expert = jnp.sum((seg_end[None, :] <= tile_start[:, None]).astype(I32), axis=1)
    used = jnp.arange(N_TILES) < n_used
    last_expert = jnp.max(jnp.where(used, tile_expert, 0))
    tile_expert = jnp.where(used, tile_expert, last_expert).astype(I32)
    return slot.astype(I32), counts, padded, seg_start, tile_expert, n_used.reshape(1)


def _rep_heads(x, n_heads):
    xpad = jnp.pad(x, ((0, 0), (0, SAMPLE_PAD - DEC_SEQ), (0, 0)))
    return jnp.tile(xpad, (1, n_heads, 1))


def kernel(x_prompt, x_sample, mem_prompt, cache_sb_k, cache_sb_v, state_hgrn, cache_mem_k, cache_mem_v,
           page_table, norm_mix, w_in, sb_logit_bias, lb_param, g_sb_out, g_hg_out, w_out, norm_xa, norm_mem,
           w_xq, w_xkv, w_xo, norm_ffn, w_route_group, w_route_expert, w_e_gate, w_e_up, w_e_down, norm_final):
    xp = x_prompt.reshape(SEQ, D_MODEL)
    xs = x_sample.reshape(N_SAMPLE, D_MODEL)

    lb = jnp.cumsum(jax.nn.softmax(lb_param.astype(F32), axis=0), axis=0)[0].reshape(1, WIDTH)
    w_in_bf = w_in[0].astype(BF16)
    w_out_bf = w_out[0].astype(BF16)
    w_xq_bf = w_xq[0].astype(BF16)
    w_xkv_bf = w_xkv[0].astype(BF16)
    w_xo_bf = w_xo[0].astype(BF16)
    w_r = jnp.concatenate([w_route_group[0],
                           jnp.transpose(w_route_expert[0], (1, 0, 2)).reshape(D_MODEL, N_EXPERTS)], axis=1)
    w_r = jnp.pad(w_r, ((0, 0), (0, ROUTE_LANES - w_r.shape[1])))
    w_r_hi = w_r.astype(BF16)
    w_r_lo = (w_r - w_r_hi.astype(F32)).astype(BF16)
    bias = sb_logit_bias[0].astype(F32)

    q_bf, k_bf, v_bf, kf_p, vf_p, kf_s, vf_s, hgin = _inproj(xp, xs, norm_mix[0].reshape(1, D_MODEL), w_in_bf)
    a_p = _sb_prompt(bias, q_bf, k_bf, v_bf, g_sb_out[0].reshape(1, HD))
    q_rows = jnp.transpose(q_bf[SEQ:].reshape(DEC_BATCH, DEC_SEQ, HEADS, HD), (0, 2, 1, 3))
    q_rows = q_rows.reshape(DEC_BATCH, SB_ROWS, HD)
    brow = jnp.broadcast_to(jnp.repeat(bias, DEC_SEQ)[:, None], (SB_ROWS, PAGE_COLS))
    a_s = _sb_sample(page_table.reshape(-1).astype(I32), q_rows, brow,
                     kf_s.reshape(DEC_BATCH, SB_ROWS, HD), vf_s.reshape(DEC_BATCH, SB_ROWS, HD),
                     g_sb_out[0].reshape(1, HD), cache_sb_k[0], cache_sb_v[0])
    a_s = jnp.transpose(a_s.reshape(DEC_BATCH, HEADS, DEC_SEQ, HD), (0, 2, 1, 3)).reshape(N_SAMPLE, WIDTH)
    g_hg = g_hg_out[0].reshape(1, HD)
    b_p, s_p = _hgrn_prompt(hgin, lb, g_hg)
    hgin_s = jnp.pad(hgin[SEQ:].reshape(DEC_BATCH, DEC_SEQ, 4 * WIDTH),
                     ((0, 0), (0, SAMPLE_PAD - DEC_SEQ), (0, 0))).reshape(DEC_BATCH * SAMPLE_PAD, 4 * WIDTH)
    b_s, s_s = _hgrn_sample(hgin_s, lb, g_hg, state_hgrn[0])
    a_all = jnp.concatenate([a_p, a_s.astype(BF16)], axis=0)
    b_all = jnp.concatenate([b_p, b_s.reshape(DEC_BATCH, SAMPLE_PAD, WIDTH)[:, :DEC_SEQ].reshape(N_SAMPLE, WIDTH)],
                            axis=0)
    h1, qx = _post_mixer(a_all, b_all, xp, xs, w_out_bf, norm_xa[0].reshape(1, D_MODEL), w_xq_bf)

    mk_p, mv_p = _mem_kv(mem_prompt[0], norm_mem[0].reshape(1, D_MODEL), w_xkv_bf)
    ox_p = _xattn_prompt(qx, mk_p, mv_p)
    qxrep = _rep_heads(qx[SEQ:].reshape(DEC_BATCH, DEC_SEQ, XA_WIDTH), XA_HEADS)
    ox_s = _xattn_sample(qxrep, cache_mem_k[0], cache_mem_v[0])
    ox = jnp.concatenate([ox_p, ox_s[:, :DEC_SEQ].reshape(N_SAMPLE, XA_WIDTH).astype(BF16)], axis=0)
    h2, xn2, route = _router(ox, h1, w_xo_bf, norm_ffn[0].reshape(1, D_MODEL), w_r_hi, w_r_lo)

    slot, counts, padded, seg_start, tile_expert, n_used = _routing_tables(route)
    xs_sorted = _dispatch(slot, counts, padded, seg_start, n_used, xn2)
    ys_sorted = _moe(tile_expert, n_used, xs_sorted, w_e_gate[0], w_e_up[0], w_e_down[0])
    y_p, y_s = _final(slot, h2, route, norm_final.reshape(1, D_MODEL), ys_sorted)

    sb_shape_p = (1, 1, SEQ, HEADS, HD)
    sb_shape_s = (1, DEC_BATCH, DEC_SEQ, HEADS, HD)
    return (y_p.reshape(1, SEQ, D_MODEL), y_s.reshape(DEC_BATCH, DEC_SEQ, D_MODEL),
            kf_p.reshape(sb_shape_p), vf_p.reshape(sb_shape_p), s_p.reshape(1, 1, HEADS, HD, HD),
            mk_p.reshape(1, 1, N_MEM, XA_HEADS, HD), mv_p.reshape(1, 1, N_MEM, XA_HEADS, HD),
            kf_s.reshape(sb_shape_s), vf_s.reshape(sb_shape_s), s_s.reshape(1, DEC_BATCH, HEADS, HD, HD))
```

```python
import functools
import math

import jax
import jax.numpy as jnp
from jax import lax
from jax.experimental import pallas as pl
from jax.experimental.pallas import tpu as pltpu

F32 = jnp.float32
BF16 = jnp.bfloat16
I32 = jnp.int32

D_MODEL = 2048
SEQ = 8192
DEC_BATCH = 128
DEC_SEQ = 4
N_SAMPLE = DEC_BATCH * DEC_SEQ
N_TOK = SEQ + N_SAMPLE
PAGE = 128
N_PAGES = 16
HEADS = 8
HD = 128
WIDTH = HEADS * HD
XA_HEADS = 4
XA_WIDTH = XA_HEADS * HD
N_MEM = 256
N_GROUPS = 4
EXP_PER_GROUP = 8
N_EXPERTS = 32
EXP_FF = 512
HG_CHUNK = 32
EPS = 1e-6
ATT_SCALE = 1.0 / math.sqrt(HD)

LANES = 128
SUBLANES = 8
MIB = 1024 * 1024

TM_IN = 512
TQ = 256
SB_HEADS_PER_STEP = 4
TM_TOK = 256
HG_ROWS = 128
TS = 256
SAMPLE_PAD = 8
PAGES_PER_STEP = 8
N_SLOTS = ((2 * N_TOK + N_EXPERTS * (TS - 1)) + TS - 1) // TS * TS
N_TILES = N_SLOTS // TS


def _cparams(n_axes, vmem_mib):
    return pltpu.CompilerParams(dimension_semantics=("arbitrary",) * n_axes,
                                vmem_limit_bytes=vmem_mib * MIB)


def _dot(a, b):
    return jnp.dot(a, b, preferred_element_type=F32)


def _dot_nt(a, b):
    return lax.dot_general(a, b, (((1,), (1,)), ((), ())), preferred_element_type=F32)


def _rms(x, g):
    ms = jnp.mean(x * x, axis=-1, keepdims=True)
    return x * lax.rsqrt(ms + EPS) * g


def _sigmoid(x):
    return 1.0 / (1.0 + jnp.exp(-x))


def _softplus(z):
    return jnp.maximum(z, 0.0) + jnp.log(1.0 + jnp.exp(-jnp.abs(z)))


def _suffix_matrix(n):
    r = lax.broadcasted_iota(I32, (n, n), 0)
    c = lax.broadcasted_iota(I32, (n, n), 1)
    return jnp.where(r >= c, 1.0, 0.0).astype(BF16)


def _split_bf16(x):
    hi = x.astype(BF16)
    lo = (x - hi.astype(F32)).astype(BF16)
    return hi, lo


def _inproj_kernel(xp_ref, xs_ref, g_ref, w_ref,
                   q_ref, kb_ref, vb_ref, kfp_ref, vfp_ref, kfs_ref, vfs_ref, hg_ref,
                   xn_ref, *, n_prompt_blocks):
    i = pl.program_id(0)
    j = pl.program_id(1)
    is_prompt = i < n_prompt_blocks

    @pl.when((j == 0) & is_prompt)
    def _():
        xn_ref[...] = _rms(xp_ref[...], g_ref[...]).astype(BF16)

    @pl.when((j == 0) & jnp.logical_not(is_prompt))
    def _():
        xn_ref[...] = _rms(xs_ref[...], g_ref[...]).astype(BF16)

    def proj():
        return _dot(xn_ref[...], w_ref[...])

    @pl.when(j == 0)
    def _():
        q_ref[...] = (proj() * ATT_SCALE).astype(BF16)

    for col, bf_ref, fp_ref, fs_ref in ((1, kb_ref, kfp_ref, kfs_ref), (2, vb_ref, vfp_ref, vfs_ref)):
        @pl.when((j == col) & is_prompt)
        def _():
            fp_ref[...] = proj()
            bf_ref[...] = fp_ref[...].astype(BF16)

        @pl.when((j == col) & jnp.logical_not(is_prompt))
        def _():
            fs_ref[...] = proj()
            bf_ref[...] = fs_ref[...].astype(BF16)

    @pl.when(j >= 3)
    def _():
        hg_ref[...] = proj()


def _inproj(xp, xs, g, w_bf):
    npb = SEQ // TM_IN
    nblk = N_TOK // TM_IN
    ncol = w_bf.shape[1] // WIDTH
    tok_spec = pl.BlockSpec((TM_IN, WIDTH), lambda i, j: (i, 0))
    return pl.pallas_call(
        functools.partial(_inproj_kernel, n_prompt_blocks=npb),
        grid=(nblk, ncol),
        in_specs=[
            pl.BlockSpec((TM_IN, D_MODEL), lambda i, j: (jnp.minimum(i, npb - 1), 0)),
            pl.BlockSpec((TM_IN, D_MODEL), lambda i, j: (jnp.maximum(i - npb, 0), 0)),
            pl.BlockSpec((1, D_MODEL), lambda i, j: (0, 0)),
            pl.BlockSpec((D_MODEL, WIDTH), lambda i, j: (0, j)),
        ],
        out_specs=[
            tok_spec, tok_spec, tok_spec,
            pl.BlockSpec((TM_IN, WIDTH), lambda i, j: (jnp.minimum(i, npb - 1), 0)),
            pl.BlockSpec((TM_IN, WIDTH), lambda i, j: (jnp.minimum(i, npb - 1), 0)),
            pl.BlockSpec((TM_IN, WIDTH), lambda i, j: (jnp.maximum(i - npb, 0), 0)),
            pl.BlockSpec((TM_IN, WIDTH), lambda i, j: (jnp.maximum(i - npb, 0), 0)),
            pl.BlockSpec((TM_IN, WIDTH), lambda i, j: (i, jnp.clip(j - 3, 0, 3))),
        ],
        out_shape=[
            jax.ShapeDtypeStruct((N_TOK, WIDTH), BF16),
            jax.ShapeDtypeStruct((N_TOK, WIDTH), BF16),
            jax.ShapeDtypeStruct((N_TOK, WIDTH), BF16),
            jax.ShapeDtypeStruct((SEQ, WIDTH), F32),
            jax.ShapeDtypeStruct((SEQ, WIDTH), F32),
            jax.ShapeDtypeStruct((N_SAMPLE, WIDTH), F32),
            jax.ShapeDtypeStruct((N_SAMPLE, WIDTH), F32),
            jax.ShapeDtypeStruct((N_TOK, 4 * WIDTH), F32),
        ],
        scratch_shapes=[pltpu.VMEM((TM_IN, D_MODEL), BF16)],
        compiler_params=_cparams(2, 48),
        name="inproj",
    )(xp, xs, g, w_bf)


def _sb_tiles(qs, ks, vs, biases, valids, u, r_refs, acc_refs):
    n = u.shape[0]
    zs = [_dot_nt(q, k) + b for q, k, b in zip(qs, ks, biases)]
    sps = []
    for z, valid in zip(zs, valids):
        sp = _softplus(z)
        sps.append(sp if valid is None else jnp.where(valid, sp, 0.0))
    pieces = [[_split_bf16(sp[:, c:c + n]) for c in range(0, sp.shape[1], n)] for sp in sps]
    sums = [[_dot(hi, u) + _dot(lo, u) for hi, lo in blocks] for blocks in pieces]

    carried = {}
    ws = []
    for z, valid, blocks, r_ref in zip(zs, valids, sums, r_refs):
        r = carried[id(r_ref)][1] if id(r_ref) in carried else r_ref[...]
        parts = [None] * len(blocks)
        for blk in range(len(blocks) - 1, -1, -1):
            parts[blk] = blocks[blk] + r
            r = r + blocks[blk][:, 0:1]
        carried[id(r_ref)] = (r_ref, r)
        s_all = parts[0] if len(parts) == 1 else jnp.concatenate(parts, axis=-1)
        w = jnp.exp(z - s_all)
        ws.append(w if valid is None else jnp.where(valid, w, 0.0))
    for r_ref, r in carried.values():
        r_ref[...] = r

    totals = {}
    for w, v, acc_ref in zip(ws, vs, acc_refs):
        d = _dot(w.astype(BF16), v)
        totals[id(acc_ref)] = (acc_ref, d + totals[id(acc_ref)][1] if id(acc_ref) in totals else d)
    for acc_ref, d in totals.values():
        acc_ref[...] += d


def _head_rms(o, g):
    return o * lax.rsqrt(jnp.mean(o * o, axis=-1, keepdims=True) + EPS) * g


def _sb_prompt_kernel(bias_ref, q_ref, k_ref, v_ref, g_ref, o_ref, acc_ref, r_ref):
    hg = pl.program_id(0)
    i = pl.program_id(1)
    u = _suffix_matrix(TQ)
    acc_ref[...] = jnp.zeros_like(acc_ref)
    r_ref[...] = jnp.zeros_like(r_ref)

    heads = range(SB_HEADS_PER_STEP)
    lanes = [slice(g * HD, (g + 1) * HD) for g in heads]
    biases = [bias_ref[hg * SB_HEADS_PER_STEP + g] for g in heads]
    r_refs = [r_ref.at[g] for g in heads]
    acc_refs = [acc_ref.at[g] for g in heads]

    def tiles(j, valid):
        start = pl.multiple_of(j * TQ, TQ)
        _sb_tiles([q_ref[:, sl] for sl in lanes],
                  [k_ref[pl.ds(start, TQ), sl] for sl in lanes],
                  [v_ref[pl.ds(start, TQ), sl] for sl in lanes],
                  biases, [valid] * SB_HEADS_PER_STEP, u, r_refs, acc_refs)

    row = lax.broadcasted_iota(I32, (TQ, TQ), 0)
    col = lax.broadcasted_iota(I32, (TQ, TQ), 1)
    tiles(i, col < row)

    def body(it, carry):
        tiles(i - 1 - it, None)
        return carry

    lax.fori_loop(0, i, body, 0)
    for g in range(SB_HEADS_PER_STEP):
        o_ref[:, g * HD:(g + 1) * HD] = _head_rms(acc_ref[g], g_ref[...]).astype(BF16)


def _sb_prompt(bias, q_bf, k_bf, v_bf, g):
    gw = SB_HEADS_PER_STEP * HD
    return pl.pallas_call(
        _sb_prompt_kernel,
        grid=(HEADS // SB_HEADS_PER_STEP, SEQ // TQ),
        in_specs=[
            pl.BlockSpec(memory_space=pltpu.SMEM),
            pl.BlockSpec((TQ, gw), lambda h, i: (i, h)),
            pl.BlockSpec((SEQ, gw), lambda h, i: (0, h)),
            pl.BlockSpec((SEQ, gw), lambda h, i: (0, h)),
            pl.BlockSpec((1, HD), lambda h, i: (0, 0)),
        ],
        out_specs=pl.BlockSpec((TQ, gw), lambda h, i: (i, h)),
        out_shape=jax.ShapeDtypeStruct((SEQ, WIDTH), BF16),
        scratch_shapes=[pltpu.VMEM((SB_HEADS_PER_STEP, TQ, HD), F32),
                        pltpu.VMEM((SB_HEADS_PER_STEP, TQ, 1), F32)],
        compiler_params=_cparams(2, 48),
        name="sb_prompt",
    )(bias, q_bf, k_bf, v_bf, g)


SB_ROWS = HEADS * DEC_SEQ
PAGE_COLS = PAGE * HEADS
NEW_COLS = LANES
SUFFIX_BLOCK = 256


def _sb_sample_kernel(pt_ref, q_ref, brow_ref, knew_ref, vnew_ref, g_ref, *rest):
    k_refs = rest[:PAGES_PER_STEP]
    v_refs = rest[PAGES_PER_STEP:2 * PAGES_PER_STEP]
    o_ref, acc_ref, r_ref, kn_ref, vn_ref = rest[2 * PAGES_PER_STEP:]
    c = pl.program_id(1)
    q = q_ref[0]
    row = lax.broadcasted_iota(I32, (SB_ROWS, PAGE_COLS), 0)
    col = lax.broadcasted_iota(I32, (SB_ROWS, PAGE_COLS), 1)
    own = (col % HEADS) == (row // DEC_SEQ)
    bias = brow_ref[...]

    @pl.when(c == 0)
    def _():
        acc_ref[...] = jnp.zeros_like(acc_ref)
        r_ref[...] = jnp.zeros_like(r_ref)
        kn_ref[...] = jnp.zeros_like(kn_ref)
        vn_ref[...] = jnp.zeros_like(vn_ref)
        kn_ref[0:SB_ROWS, :] = knew_ref[0]
        vn_ref[0:SB_ROWS, :] = vnew_ref[0]
        rn = lax.broadcasted_iota(I32, (SB_ROWS, NEW_COLS), 0)
        cn = lax.broadcasted_iota(I32, (SB_ROWS, NEW_COLS), 1)
        valid = ((cn % HEADS) == (rn // DEC_SEQ)) & ((cn // HEADS) < (rn % DEC_SEQ))
        _sb_tiles([q], [kn_ref[...].astype(BF16)], [vn_ref[...].astype(BF16)], [bias[:, 0:NEW_COLS]], [valid],
                  _suffix_matrix(NEW_COLS), [r_ref], [acc_ref])

    order = range(PAGES_PER_STEP - 1, -1, -1)
    _sb_tiles([q] * PAGES_PER_STEP,
              [k_refs[p][0].reshape(PAGE_COLS, HD).astype(BF16) for p in order],
              [v_refs[p][0].reshape(PAGE_COLS, HD).astype(BF16) for p in order],
              [bias] * PAGES_PER_STEP, [own] * PAGES_PER_STEP, _suffix_matrix(SUFFIX_BLOCK),
              [r_ref] * PAGES_PER_STEP, [acc_ref] * PAGES_PER_STEP)

    @pl.when(c == pl.num_programs(1) - 1)
    def _():
        o_ref[0] = _head_rms(acc_ref[...], g_ref[...]).astype(BF16).astype(F32)


def _sb_sample(page_table, q_rows, brow, knew, vnew, g, kcache, vcache):
    n_steps = N_PAGES // PAGES_PER_STEP

    def page_spec(p):
        def imap(b, c, pt):
            return (pt[b * N_PAGES + (n_steps - 1 - c) * PAGES_PER_STEP + p], 0, 0, 0)
        return pl.BlockSpec((1, PAGE, HEADS, HD), imap)

    grid_spec = pltpu.PrefetchScalarGridSpec(
        num_scalar_prefetch=1,
        grid=(DEC_BATCH, n_steps),
        in_specs=[
            pl.BlockSpec((1, SB_ROWS, HD), lambda b, c, pt: (b, 0, 0)),
            pl.BlockSpec((SB_ROWS, PAGE_COLS), lambda b, c, pt: (0, 0)),
            pl.BlockSpec((1, SB_ROWS, HD), lambda b, c, pt: (b, 0, 0)),
            pl.BlockSpec((1, SB_ROWS, HD), lambda b, c, pt: (b, 0, 0)),
            pl.BlockSpec((1, HD), lambda b, c, pt: (0, 0)),
        ] + [page_spec(p) for p in range(PAGES_PER_STEP)] * 2,
        out_specs=pl.BlockSpec((1, SB_ROWS, HD), lambda b, c, pt: (b, 0, 0)),
        scratch_shapes=[
            pltpu.VMEM((SB_ROWS, HD), F32),
            pltpu.VMEM((SB_ROWS, 1), F32),
            pltpu.VMEM((NEW_COLS, HD), F32),
            pltpu.VMEM((NEW_COLS, HD), F32),
        ],
    )
    return pl.pallas_call(
        _sb_sample_kernel,
        grid_spec=grid_spec,
        out_shape=jax.ShapeDtypeStruct((DEC_BATCH, SB_ROWS, HD), F32),
        compiler_params=_cparams(2, 40),
        name="sb_sample",
    )(page_table, q_rows, brow, knew, vnew, g,
      *([kcache] * PAGES_PER_STEP), *([vcache] * PAGES_PER_STEP))


def _hgrn_block(hq, hf, hi, hgate, lb, g_out, state_in, state_out, *, chunk, last, sequential):
    n_chunks = HG_ROWS // chunk
    one_m_lb = 1.0 - lb
    e = jnp.exp(-jnp.abs(hf))
    inv = 1.0 / (1.0 + e)
    sig_pos = jnp.where(hf >= 0, inv, e * inv)
    sig_neg = jnp.where(hf >= 0, e * inv, inv)
    f = lb + one_m_lb * sig_pos
    k = one_m_lb * sig_neg
    g = jnp.log(f)
    q = hq * _sigmoid(hq)

    rin = lax.broadcasted_iota(I32, (HG_ROWS, WIDTH), 0) % chunk
    b = g
    step = 1
    while step < chunk:
        b = b + jnp.where(rin >= step, pltpu.roll(b, step, axis=0), 0.0)
        step *= 2
    cid = lax.broadcasted_iota(I32, (HG_ROWS, WIDTH), 0) // chunk
    b_last_rows = jnp.zeros_like(b)
    b_last = []
    for c in range(n_chunks):
        bl = b[c * chunk + last:c * chunk + last + 1, :]
        b_last.append(bl)
        b_last_rows = jnp.where(cid == c, bl, b_last_rows)

    q_dec = q * jnp.exp(b)
    k_inv = k * jnp.exp(-b)
    k_dec = k * jnp.exp(b_last_rows - b)

    r2 = lax.broadcasted_iota(I32, (HG_ROWS, HG_ROWS), 0)
    c2 = lax.broadcasted_iota(I32, (HG_ROWS, HG_ROWS), 1)
    intra = ((r2 // chunk) == (c2 // chunk)) & (c2 <= r2)
    eye = r2 == c2
    outs = []
    for h in range(HEADS):
        sl = slice(h * HD, (h + 1) * HD)
        qd_h = q_dec[:, sl]
        v_h = hi[:, sl].astype(BF16)
        a = jnp.where(intra, _dot_nt(qd_h.astype(BF16), k_inv[:, sl].astype(BF16)), 0.0)
        o_h = _dot(a.astype(BF16), v_h)
        kdec_t = jnp.transpose(k_dec[:, sl])
        s_cur = state_in(0, h) if sequential else None
        for c in range(n_chunks):
            if not sequential:
                s_cur = state_in(c, h)
            in_rows = (r2 // chunk) == c
            in_cols = (c2 // chunk) == c
            o_h = o_h + _dot(jnp.where(in_rows, qd_h, 0.0).astype(BF16), s_cur.astype(BF16))
            dl = jnp.exp(b_last[c][:, sl])
            dcol = jnp.sum(jnp.where(eye, dl, 0.0), axis=1, keepdims=True)
            s_new = dcol * s_cur + _dot(jnp.where(in_cols, kdec_t, 0.0).astype(BF16), v_h)
            if sequential:
                s_cur = s_new
            else:
                state_out(c, h, s_new)
        if sequential:
            state_out(0, h, s_cur)
        gate = hgate[:, sl]
        outs.append((_head_rms(o_h, g_out) * (gate * _sigmoid(gate))).astype(BF16))
    return jnp.concatenate(outs, axis=-1)


def _hgrn_prompt_kernel(hq_ref, hf_ref, hi_ref, hg_ref, lb_ref, g_ref, o_ref, sfin_ref, s_ref):
    i = pl.program_id(0)

    @pl.when(i == 0)
    def _():
        s_ref[...] = jnp.zeros_like(s_ref)

    def state_in(c, h):
        return s_ref[h]

    def state_out(c, h, s):
        s_ref[h] = s

    o_ref[...] = _hgrn_block(hq_ref[...], hf_ref[...], hi_ref[...], hg_ref[...], lb_ref[...], g_ref[...],
                             state_in, state_out, chunk=HG_CHUNK, last=HG_CHUNK - 1, sequential=True)

    @pl.when(i == pl.num_programs(0) - 1)
    def _():
        sfin_ref[...] = s_ref[...]


def _hgrn_prompt(hgin, lb, g):
    def col(j):
        return pl.BlockSpec((HG_ROWS, WIDTH), lambda i: (i, j))
    return pl.pallas_call(
        _hgrn_prompt_kernel,
        grid=(SEQ // HG_ROWS,),
        in_specs=[col(0), col(1), col(2), col(3),
                  pl.BlockSpec((1, WIDTH), lambda i: (0, 0)),
                  pl.BlockSpec((1, HD), lambda i: (0, 0))],
        out_specs=[pl.BlockSpec((HG_ROWS, WIDTH), lambda i: (i, 0)),
                   pl.BlockSpec((HEADS, HD, HD), lambda i: (0, 0, 0))],
        out_shape=[jax.ShapeDtypeStruct((SEQ, WIDTH), BF16),
                   jax.ShapeDtypeStruct((HEADS, HD, HD), F32)],
        scratch_shapes=[pltpu.VMEM((HEADS, HD, HD), F32)],
        compiler_params=_cparams(1, 32),
        name="hgrn_prompt",
    )(hgin, hgin, hgin, hgin, lb, g)


def _hgrn_sample_kernel(hq_ref, hf_ref, hi_ref, hg_ref, lb_ref, g_ref, s0_ref, o_ref, sout_ref):
    def state_in(c, h):
        return s0_ref[c, h]

    def state_out(c, h, s):
        sout_ref[c, h] = s

    o_ref[...] = _hgrn_block(hq_ref[...], hf_ref[...], hi_ref[...], hg_ref[...], lb_ref[...], g_ref[...],
                             state_in, state_out, chunk=SAMPLE_PAD, last=DEC_SEQ - 1, sequential=False)


def _hgrn_sample(hgin_pad, lb, g, s0):
    bpb = HG_ROWS // SAMPLE_PAD

    def col(j):
        return pl.BlockSpec((HG_ROWS, WIDTH), lambda i: (i, j))
    return pl.pallas_call(
        _hgrn_sample_kernel,
        grid=(DEC_BATCH // bpb,),
        in_specs=[col(0), col(1), col(2), col(3),
                  pl.BlockSpec((1, WIDTH), lambda i: (0, 0)),
                  pl.BlockSpec((1, HD), lambda i: (0, 0)),
                  pl.BlockSpec((bpb, HEADS, HD, HD), lambda i: (i, 0, 0, 0))],
        out_specs=[pl.BlockSpec((HG_ROWS, WIDTH), lambda i: (i, 0)),
                   pl.BlockSpec((bpb, HEADS, HD, HD), lambda i: (i, 0, 0, 0))],
        out_shape=[jax.ShapeDtypeStruct((DEC_BATCH * SAMPLE_PAD, WIDTH), BF16),
                   jax.ShapeDtypeStruct((DEC_BATCH, HEADS, HD, HD), F32)],
        compiler_params=_cparams(1, 48),
        name="hgrn_sample",
    )(hgin_pad, hgin_pad, hgin_pad, hgin_pad, lb, g, s0)


def _post_mixer_kernel(a_ref, b_ref, xp_ref, xs_ref, wo_ref, gx_ref, wq_ref, h1_ref, qx_ref, *, n_prompt_blocks):
    i = pl.program_id(0)
    y = _dot(a_ref[...], wo_ref[0:WIDTH, :]) + _dot(b_ref[...], wo_ref[WIDTH:2 * WIDTH, :])

    def finish(x):
        h1 = x + y
        h1_ref[...] = h1
        qx_ref[...] = (_dot(_rms(h1, gx_ref[...]).astype(BF16), wq_ref[...]) * ATT_SCALE).astype(BF16)

    @pl.when(i < n_prompt_blocks)
    def _():
        finish(xp_ref[...])

    @pl.when(i >= n_prompt_blocks)
    def _():
        finish(xs_ref[...])


def _post_mixer(a, b, xp, xs, wo_bf, gx, wq_bf):
    npb = SEQ // TM_TOK
    return pl.pallas_call(
        functools.partial(_post_mixer_kernel, n_prompt_blocks=npb),
        grid=(N_TOK // TM_TOK,),
        in_specs=[
            pl.BlockSpec((TM_TOK, WIDTH), lambda i: (i, 0)),
            pl.BlockSpec((TM_TOK, WIDTH), lambda i: (i, 0)),
            pl.BlockSpec((TM_TOK, D_MODEL), lambda i: (jnp.minimum(i, npb - 1), 0)),
            pl.BlockSpec((TM_TOK, D_MODEL), lambda i: (jnp.maximum(i - npb, 0), 0)),
            pl.BlockSpec((2 * WIDTH, D_MODEL), lambda i: (0, 0)),
            pl.BlockSpec((1, D_MODEL), lambda i: (0, 0)),
            pl.BlockSpec((D_MODEL, XA_WIDTH), lambda i: (0, 0)),
        ],
        out_specs=[pl.BlockSpec((TM_TOK, D_MODEL), lambda i: (i, 0)),
                   pl.BlockSpec((TM_TOK, XA_WIDTH), lambda i: (i, 0))],
        out_shape=[jax.ShapeDtypeStruct((N_TOK, D_MODEL), F32),
                   jax.ShapeDtypeStruct((N_TOK, XA_WIDTH), BF16)],
        compiler_params=_cparams(1, 48),
        name="post_mixer",
    )(a, b, xp, xs, wo_bf, gx, wq_bf)


def _mem_kv_kernel(m_ref, g_ref, w_ref, mk_ref, mv_ref):
    kv = _dot(_rms(m_ref[...], g_ref[...]).astype(BF16), w_ref[...])
    mk_ref[...] = kv[:, 0:XA_WIDTH]
    mv_ref[...] = kv[:, XA_WIDTH:2 * XA_WIDTH]


def _mem_kv(mem, g, w_bf):
    return pl.pallas_call(
        _mem_kv_kernel,
        out_shape=[jax.ShapeDtypeStruct((N_MEM, XA_WIDTH), F32)] * 2,
        compiler_params=pltpu.CompilerParams(vmem_limit_bytes=32 * MIB),
        name="mem_kv",
    )(mem, g, w_bf)


def _softmax_rows(s):
    m = jnp.max(s, axis=-1, keepdims=True)
    e = jnp.exp(s - m)
    return e / jnp.sum(e, axis=-1, keepdims=True)


def _xattn_prompt_kernel(q_ref, mk_ref, mv_ref, o_ref):
    mk = mk_ref[...].astype(BF16)
    mv = mv_ref[...].astype(BF16)
    outs = []
    for h in range(XA_HEADS):
        sl = slice(h * HD, (h + 1) * HD)
        p = _softmax_rows(_dot_nt(q_ref[:, sl], mk[:, sl]))
        outs.append(_dot(p.astype(BF16), mv[:, sl]))
    o_ref[...] = jnp.concatenate(outs, axis=-1).astype(BF16)


def _xattn_prompt(qx, mk, mv):
    return pl.pallas_call(
        _xattn_prompt_kernel,
        grid=(SEQ // TM_TOK,),
        in_specs=[pl.BlockSpec((TM_TOK, XA_WIDTH), lambda i: (i, 0)),
                  pl.BlockSpec((N_MEM, XA_WIDTH), lambda i: (0, 0)),
                  pl.BlockSpec((N_MEM, XA_WIDTH), lambda i: (0, 0))],
        out_specs=pl.BlockSpec((TM_TOK, XA_WIDTH), lambda i: (i, 0)),
        out_shape=jax.ShapeDtypeStruct((SEQ, XA_WIDTH), BF16),
        compiler_params=_cparams(1, 32),
        name="xattn_prompt",
    )(qx, mk, mv)


XA_BATCHES_PER_STEP = 8


def _xattn_sample_kernel(qrep_ref, mk_ref, mv_ref, o_ref):
    rows = XA_HEADS * SAMPLE_PAD
    row = lax.broadcasted_iota(I32, (rows, XA_WIDTH), 0)
    col = lax.broadcasted_iota(I32, (rows, XA_WIDTH), 1)
    diag = (row // SAMPLE_PAD) == (col // HD)
    for bb in range(XA_BATCHES_PER_STEP):
        qbd = jnp.where(diag, qrep_ref[bb], jnp.zeros((), BF16))
        mk = jnp.concatenate([mk_ref[bb, :, h, :] for h in range(XA_HEADS)], axis=-1).astype(BF16)
        mv = jnp.concatenate([mv_ref[bb, :, h, :] for h in range(XA_HEADS)], axis=-1).astype(BF16)
        p = _softmax_rows(_dot_nt(qbd, mk))
        o = jnp.where(diag, _dot(p.astype(BF16), mv), 0.0)
        out = o[0:SAMPLE_PAD]
        for h in range(1, XA_HEADS):
            out = out + o[h * SAMPLE_PAD:(h + 1) * SAMPLE_PAD]
        o_ref[bb] = out.astype(BF16).astype(F32)


def _xattn_sample(qrep, mk, mv):
    rows = XA_HEADS * SAMPLE_PAD
    nb = XA_BATCHES_PER_STEP
    return pl.pallas_call(
        _xattn_sample_kernel,
        grid=(DEC_BATCH // nb,),
        in_specs=[pl.BlockSpec((nb, rows, XA_WIDTH), lambda i: (i, 0, 0)),
                  pl.BlockSpec((nb, N_MEM, XA_HEADS, HD), lambda i: (i, 0, 0, 0)),
                  pl.BlockSpec((nb, N_MEM, XA_HEADS, HD), lambda i: (i, 0, 0, 0))],
        out_specs=pl.BlockSpec((nb, SAMPLE_PAD, XA_WIDTH), lambda i: (i, 0, 0)),
        out_shape=jax.ShapeDtypeStruct((DEC_BATCH, SAMPLE_PAD, XA_WIDTH), F32),
        compiler_params=_cparams(1, 40),
        name="xattn_sample",
    )(qrep, mk, mv)


ROUTE_LANES = LANES


def _router_kernel(ox_ref, h1_ref, wxo_ref, gf_ref, wr_hi_ref, wr_lo_ref, h2_ref, xn_ref, route_ref):
    h2 = h1_ref[...] + _dot(ox_ref[...], wxo_ref[...])
    h2_ref[...] = h2
    xn = _rms(h2, gf_ref[...])
    xn_ref[...] = xn
    x_hi = xn.astype(BF16)
    x_lo = (xn - x_hi.astype(F32)).astype(BF16)
    logits = _dot(x_hi, wr_hi_ref[...]) + (_dot(x_lo, wr_hi_ref[...]) + _dot(x_hi, wr_lo_ref[...]))

    lane = lax.broadcasted_iota(I32, logits.shape, 1).astype(F32)
    neg = jnp.float32(-jnp.inf)
    big = jnp.float32(ROUTE_LANES)
    lg = jnp.where(lane < N_GROUPS, logits, neg)
    gmax = jnp.max(lg, axis=-1, keepdims=True)
    g_idx = jnp.min(jnp.where(lg == gmax, lane, big), axis=-1, keepdims=True)
    g_w = 1.0 / jnp.sum(jnp.exp(lg - gmax), axis=-1, keepdims=True)
    lo_lane = N_GROUPS + g_idx * EXP_PER_GROUP
    le = jnp.where((lane >= lo_lane) & (lane < lo_lane + EXP_PER_GROUP), logits, neg)
    v1 = jnp.max(le, axis=-1, keepdims=True)
    i1 = jnp.min(jnp.where(le == v1, lane, big), axis=-1, keepdims=True)
    le2 = jnp.where(lane == i1, neg, le)
    v2 = jnp.max(le2, axis=-1, keepdims=True)
    i2 = jnp.min(jnp.where(le2 == v2, lane, big), axis=-1, keepdims=True)
    e2 = jnp.exp(v2 - v1)
    w1 = g_w / (1.0 + e2)
    w2 = g_w * e2 / (1.0 + e2)
    route = jnp.where(lane == 0, (i1 - N_GROUPS).astype(F32),
                      jnp.where(lane == 1, (i2 - N_GROUPS).astype(F32),
                                jnp.where(lane == 2, w1, jnp.where(lane == 3, w2, 0.0))))
    route_ref[...] = route


def _router(ox, h1, wxo_bf, gf, wr_hi, wr_lo):
    return pl.pallas_call(
        _router_kernel,
        grid=(N_TOK // TM_TOK,),
        in_specs=[pl.BlockSpec((TM_TOK, XA_WIDTH), lambda i: (i, 0)),
                  pl.BlockSpec((TM_TOK, D_MODEL), lambda i: (i, 0)),
                  pl.BlockSpec((XA_WIDTH, D_MODEL), lambda i: (0, 0)),
                  pl.BlockSpec((1, D_MODEL), lambda i: (0, 0)),
                  pl.BlockSpec((D_MODEL, ROUTE_LANES), lambda i: (0, 0)),
                  pl.BlockSpec((D_MODEL, ROUTE_LANES), lambda i: (0, 0))],
        out_specs=[pl.BlockSpec((TM_TOK, D_MODEL), lambda i: (i, 0)),
                   pl.BlockSpec((TM_TOK, D_MODEL), lambda i: (i, 0)),
                   pl.BlockSpec((TM_TOK, ROUTE_LANES), lambda i: (i, 0))],
        out_shape=[jax.ShapeDtypeStruct((N_TOK, D_MODEL), F32),
                   jax.ShapeDtypeStruct((N_TOK, D_MODEL), F32),
                   jax.ShapeDtypeStruct((N_TOK, ROUTE_LANES), F32)],
        compiler_params=_cparams(1, 40),
        name="router",
    )(ox, h1, wxo_bf, gf, wr_hi, wr_lo)


def _dispatch_kernel(slot_ref, cnt_ref, pad_ref, start_ref, nused_ref,
                     x_ref, xs_hbm, zrow, sems):
    i = pl.program_id(0)
    sem = sems.at[0]
    zsem = sems.at[1]

    @pl.when(i == 0)
    def _():
        zrow[...] = jnp.zeros_like(zrow)

        def pad_rows(fn):
            def per_expert(e, carry):
                base = start_ref[e]

                def body(r, c2):
                    fn(base + r)
                    return c2
                lax.fori_loop(cnt_ref[e], pad_ref[e], body, 0)
                return carry
            lax.fori_loop(0, N_EXPERTS, per_expert, 0)

        pad_rows(lambda row: pltpu.make_async_copy(zrow.at[pl.ds(0, 1)], xs_hbm.at[pl.ds(row, 1)], zsem).start())
        pad_rows(lambda row: pltpu.make_async_copy(zrow.at[pl.ds(0, 1)], xs_hbm.at[pl.ds(0, 1)], zsem).wait())

        def unused_tiles(fn):
            def body(t, carry):
                fn(pl.multiple_of(t * TS, TS))
                return carry
            lax.fori_loop(nused_ref[0], N_TILES, body, 0)

        unused_tiles(lambda r0: pltpu.make_async_copy(zrow, xs_hbm.at[pl.ds(r0, TS)], zsem).start())
        unused_tiles(lambda r0: pltpu.make_async_copy(zrow, xs_hbm.at[pl.ds(0, TS)], zsem).wait())

    base = i * (2 * TM_TOK)

    def body(r, carry):
        t = lax.shift_right_logical(r, 1)
        pltpu.make_async_copy(x_ref.at[pl.ds(t, 1)], xs_hbm.at[pl.ds(slot_ref[base + r], 1)], sem).start()
        return carry
    lax.fori_loop(0, 2 * TM_TOK, body, 0, unroll=8)

    for _ in range(2):
        pltpu.make_async_copy(x_ref, xs_hbm.at[pl.ds(0, TM_TOK)], sem).wait()


def _dispatch(slot, counts, padded, seg_start, n_used, xn):
    grid_spec = pltpu.PrefetchScalarGridSpec(
        num_scalar_prefetch=5,
        grid=(N_TOK // TM_TOK,),
        in_specs=[pl.BlockSpec((TM_TOK, D_MODEL), lambda i, *_: (i, 0))],
        out_specs=pl.BlockSpec(memory_space=pl.ANY),
        scratch_shapes=[pltpu.VMEM((TS, D_MODEL), F32),
                        pltpu.SemaphoreType.DMA((2,))],
    )
    return pl.pallas_call(
        _dispatch_kernel,
        grid_spec=grid_spec,
        out_shape=jax.ShapeDtypeStruct((N_SLOTS, D_MODEL), F32),
        compiler_params=_cparams(1, 16),
        name="dispatch",
    )(slot, counts, padded, seg_start, n_used, xn)


def _moe_kernel(te_ref, nused_ref, xs_ref, wg_ref, wu_ref, wd_ref, ys_ref, wg_bf, wu_bf, wd_bf):
    i = pl.program_id(0)
    active = i < nused_ref[0]

    @pl.when(active)
    def _():
        new_expert = jnp.logical_or(i == 0, te_ref[i] != te_ref[jnp.maximum(i - 1, 0)])

        @pl.when(new_expert)
        def _():
            wg_bf[...] = wg_ref[0].astype(BF16)
            wu_bf[...] = wu_ref[0].astype(BF16)
            wd_bf[...] = wd_ref[0].astype(BF16)

        x = xs_ref[...].astype(BF16)
        a = _dot(x, wg_bf[...])
        u = _dot(x, wu_bf[...])
        hm = (a * _sigmoid(a)) * u
        ys_ref[...] = _dot(hm.astype(BF16), wd_bf[...])

    @pl.when(jnp.logical_not(active))
    def _():
        ys_ref[...] = jnp.zeros_like(ys_ref)


def _moe(tile_expert, n_used, xs, wg, wu, wd):
    grid_spec = pltpu.PrefetchScalarGridSpec(
        num_scalar_prefetch=2,
        grid=(N_TILES,),
        in_specs=[
            pl.BlockSpec((TS, D_MODEL), lambda i, te, nu: (jnp.minimum(i, nu[0] - 1), 0)),
            pl.BlockSpec((1, D_MODEL, EXP_FF), lambda i, te, nu: (te[i], 0, 0)),
            pl.BlockSpec((1, D_MODEL, EXP_FF), lambda i, te, nu: (te[i], 0, 0)),
            pl.BlockSpec((1, EXP_FF, D_MODEL), lambda i, te, nu: (te[i], 0, 0)),
        ],
        out_specs=pl.BlockSpec((TS, D_MODEL), lambda i, te, nu: (i, 0)),
        scratch_shapes=[
            pltpu.VMEM((D_MODEL, EXP_FF), BF16),
            pltpu.VMEM((D_MODEL, EXP_FF), BF16),
            pltpu.VMEM((EXP_FF, D_MODEL), BF16),
        ],
    )
    return pl.pallas_call(
        _moe_kernel,
        grid_spec=grid_spec,
        out_shape=jax.ShapeDtypeStruct((N_SLOTS, D_MODEL), F32),
        compiler_params=_cparams(1, 48),
        name="moe",
    )(tile_expert, n_used, xs, wg, wu, wd)


def _final_kernel(slot_ref, h2_ref, route_ref, g_ref, ys_hbm, yp_ref, ysm_ref, ybuf, sem, *, n_prompt_blocks):
    i = pl.program_id(0)
    n = pl.num_programs(0)
    buf = i % 2

    def gather(blk, b):
        base = blk * (2 * TM_TOK)

        def body(r, carry):
            s = slot_ref[base + r]
            k = r & 1
            t = lax.shift_right_logical(r, 1)
            pltpu.make_async_copy(ys_hbm.at[pl.ds(s, 1)], ybuf.at[b, k, pl.ds(t, 1)], sem.at[b]).start()
            return carry
        lax.fori_loop(0, 2 * TM_TOK, body, 0, unroll=8)

    @pl.when(i == 0)
    def _():
        gather(0, 0)

    @pl.when(i + 1 < n)
    def _():
        gather(i + 1, 1 - buf)

    for k in range(2):
        pltpu.make_async_copy(ys_hbm.at[pl.ds(0, TM_TOK)], ybuf.at[buf, k], sem.at[buf]).wait()

    lane = lax.broadcasted_iota(I32, (TM_TOK, ROUTE_LANES), 1)
    route = route_ref[...]
    w0 = jnp.sum(jnp.where(lane == 2, route, 0.0), axis=-1, keepdims=True)
    w1 = jnp.sum(jnp.where(lane == 3, route, 0.0), axis=-1, keepdims=True)
    y = _rms(h2_ref[...] + (w0 * ybuf[buf, 0] + w1 * ybuf[buf, 1]), g_ref[...])

    @pl.when(i < n_prompt_blocks)
    def _():
        yp_ref[...] = y

    @pl.when(i >= n_prompt_blocks)
    def _():
        ysm_ref[...] = y


def _final(slot, h2, route, g, ys):
    npb = SEQ // TM_TOK
    grid_spec = pltpu.PrefetchScalarGridSpec(
        num_scalar_prefetch=1,
        grid=(N_TOK // TM_TOK,),
        in_specs=[pl.BlockSpec((TM_TOK, D_MODEL), lambda i, s: (i, 0)),
                  pl.BlockSpec((TM_TOK, ROUTE_LANES), lambda i, s: (i, 0)),
                  pl.BlockSpec((1, D_MODEL), lambda i, s: (0, 0)),
                  pl.BlockSpec(memory_space=pl.ANY)],
        out_specs=[pl.BlockSpec((TM_TOK, D_MODEL), lambda i, s: (jnp.minimum(i, npb - 1), 0)),
                   pl.BlockSpec((TM_TOK, D_MODEL), lambda i, s: (jnp.maximum(i - npb, 0), 0))],
        scratch_shapes=[pltpu.VMEM((2, 2, TM_TOK, D_MODEL), F32),
                        pltpu.SemaphoreType.DMA((2,))],
    )
    return pl.pallas_call(
        functools.partial(_final_kernel, n_prompt_blocks=npb),
        grid_spec=grid_spec,
        out_shape=[jax.ShapeDtypeStruct((SEQ, D_MODEL), F32),
                   jax.ShapeDtypeStruct((N_SAMPLE, D_MODEL), F32)],
        compiler_params=_cparams(1, 40),
        name="final",
    )(slot, h2, route, g, ys)


def _routing_tables(route):
    eid = route[:, 0:2].astype(I32).reshape(-1)
    onehot = (eid[:, None] == jnp.arange(N_EXPERTS, dtype=I32)[None, :]).astype(I32)
    csum = jnp.cumsum(onehot, axis=0)
    rank = jnp.sum(onehot * (csum - onehot), axis=1)
    counts = csum[-1]
    padded = (counts + TS - 1) // TS * TS
    seg_end = jnp.cumsum(padded)
    seg_start = seg_end - padded
    slot = jnp.sum(onehot * seg_start[None, :], axis=1) + rank
    n_used = (seg_end[-1] // TS).astype(I32)
    tile_start = jnp.arange(N_TILES, dtype=I32) * TS
    tile_expert = jnp.sum((seg_end[None, :] <= tile_start[:, None]).astype(I32), axis=1)
    used = jnp.arange(N_TILES) < n_used
    last_expert = jnp.max(jnp.where(used, tile_expert, 0))
    tile_expert = jnp.where(used, tile_expert, last_expert).astype(I32)
    return slot.astype(I32), counts, padded, seg_start, tile_expert, n_used.reshape(1)


def _rep_heads(x, n_heads):
    xpad = jnp.pad(x, ((0, 0), (0, SAMPLE_PAD - DEC_SEQ), (0, 0)))
    return jnp.tile(xpad, (1, n_heads, 1))


def kernel(x_prompt, x_sample, mem_prompt, cache_sb_k, cache_sb_v, state_hgrn, cache_mem_k, cache_mem_v,
           page_table, norm_mix, w_in, sb_logit_bias, lb_param, g_sb_out, g_hg_out, w_out, norm_xa, norm_mem,
           w_xq, w_xkv, w_xo, norm_ffn, w_route_group, w_route_expert, w_e_gate, w_e_up, w_e_down, norm_final):
    xp = x_prompt.reshape(SEQ, D_MODEL)
    xs = x_sample.reshape(N_SAMPLE, D_MODEL)

    lb = jnp.cumsum(jax.nn.softmax(lb_param.astype(F32), axis=0), axis=0)[0].reshape(1, WIDTH)
    w_in_bf = w_in[0].astype(BF16)
    w_out_bf = w_out[0].astype(BF16)
    w_xq_bf = w_xq[0].astype(BF16)
    w_xkv_bf = w_xkv[0].astype(BF16)
    w_xo_bf = w_xo[0].astype(BF16)
    w_r = jnp.concatenate([w_route_group[0],
                           jnp.transpose(w_route_expert[0], (1, 0, 2)).reshape(D_MODEL, N_EXPERTS)], axis=1)
    w_r = jnp.pad(w_r, ((0, 0), (0, ROUTE_LANES - w_r.shape[1])))
    w_r_hi = w_r.astype(BF16)
    w_r_lo = (w_r - w_r_hi.astype(F32)).astype(BF16)
    bias = sb_logit_bias[0].astype(F32)

    q_bf, k_bf, v_bf, kf_p, vf_p, kf_s, vf_s, hgin = _inproj(xp, xs, norm_mix[0].reshape(1, D_MODEL), w_in_bf)
    a_p = _sb_prompt(bias, q_bf, k_bf, v_bf, g_sb_out[0].reshape(1, HD))
    q_rows = jnp.transpose(q_bf[SEQ:].reshape(DEC_BATCH, DEC_SEQ, HEADS, HD), (0, 2, 1, 3))
    q_rows = q_rows.reshape(DEC_BATCH, SB_ROWS, HD)
    brow = jnp.broadcast_to(jnp.repeat(bias, DEC_SEQ)[:, None], (SB_ROWS, PAGE_COLS))
    a_s = _sb_sample(page_table.reshape(-1).astype(I32), q_rows, brow,
                     kf_s.reshape(DEC_BATCH, SB_ROWS, HD), vf_s.reshape(DEC_BATCH, SB_ROWS, HD),
                     g_sb_out[0].reshape(1, HD), cache_sb_k[0], cache_sb_v[0])
    a_s = jnp.transpose(a_s.reshape(DEC_BATCH, HEADS, DEC_SEQ, HD), (0, 2, 1, 3)).reshape(N_SAMPLE, WIDTH)
    g_hg = g_hg_out[0].reshape(1, HD)
    b_p, s_p = _hgrn_prompt(hgin, lb, g_hg)
    hgin_s = jnp.pad(hgin[SEQ:].reshape(DEC_BATCH, DEC_SEQ, 4 * WIDTH),
                     ((0, 0), (0, SAMPLE_PAD - DEC_SEQ), (0, 0))).reshape(DEC_BATCH * SAMPLE_PAD, 4 * WIDTH)
    b_s, s_s = _hgrn_sample(hgin_s, lb, g_hg, state_hgrn[0])
    a_all = jnp.concatenate([a_p, a_s.astype(BF16)], axis=0)
    b_all = jnp.concatenate([b_p, b_s.reshape(DEC_BATCH, SAMPLE_PAD, WIDTH)[:, :DEC_SEQ].reshape(N_SAMPLE, WIDTH)],
                            axis=0)
    h1, qx = _post_mixer(a_all, b_all, xp, xs, w_out_bf, norm_xa[0].reshape(1, D_MODEL), w_xq_bf)

    mk_p, mv_p = _mem_kv(mem_prompt[0], norm_mem[0].reshape(1, D_MODEL), w_xkv_bf)
    ox_p = _xattn_prompt(qx, mk_p, mv_p)
    qxrep = _rep_heads(qx[SEQ:].reshape(DEC_BATCH, DEC_SEQ, XA_WIDTH), XA_HEADS)
    ox_s = _xattn_sample(qxrep, cache_mem_k[0], cache_mem_v[0])
    ox = jnp.concatenate([ox_p, ox_s[:, :DEC_SEQ].reshape(N_SAMPLE, XA_WIDTH).astype(BF16)], axis=0)
    h2, xn2, route = _router(ox, h1, w_xo_bf, norm_ffn[0].reshape(1, D_MODEL), w_r_hi, w_r_lo)

    slot, counts, padded, seg_start, tile_expert, n_used = _routing_tables(route)
    xs_sorted = _dispatch(slot, counts, padded, seg_start, n_used, xn2)
    ys_sorted = _moe(tile_expert, n_used, xs_sorted, w_e_gate[0], w_e_up[0], w_e_down[0])
    y_p, y_s = _final(slot, h2, route, norm_final.reshape(1, D_MODEL), ys_sorted)

    sb_shape_p = (1, 1, SEQ, HEADS, HD)
    sb_shape_s = (1, DEC_BATCH, DEC_SEQ, HEADS, HD)
    return (y_p.reshape(1, SEQ, D_MODEL), y_s.reshape(DEC_BATCH, DEC_SEQ, D_MODEL),
            kf_p.reshape(sb_shape_p), vf_p.reshape(sb_shape_p), s_p.reshape(1, 1, HEADS, HD, HD),
            mk_p.reshape(1, 1, N_MEM, XA_HEADS, HD), mv_p.reshape(1, 1, N_MEM, XA_HEADS, HD),
            kf_s.reshape(sb_shape_s), vf_s.reshape(sb_shape_s), s_s.reshape(1, DEC_BATCH, HEADS, HD, HD))
```

```python
import functools
import math

import jax
import jax.numpy as jnp
from jax import lax
from jax.experimental import pallas as pl
from jax.experimental.pallas import tpu as pltpu

F32 = jnp.float32
BF16 = jnp.bfloat16
I32 = jnp.int32

D_MODEL = 2048
SEQ = 8192
DEC_BATCH = 128
DEC_SEQ = 4
N_SAMPLE = DEC_BATCH * DEC_SEQ
N_TOK = SEQ + N_SAMPLE
PAGE = 128
N_PAGES = 16
HEADS = 8
HD = 128
WIDTH = HEADS * HD
XA_HEADS = 4
XA_WIDTH = XA_HEADS * HD
N_MEM = 256
N_GROUPS = 4
EXP_PER_GROUP = 8
N_EXPERTS = 32
EXP_FF = 512
HG_CHUNK = 32
EPS = 1e-6
ATT_SCALE = 1.0 / math.sqrt(HD)

LANES = 128
SUBLANES = 8
MIB = 1024 * 1024

TM_IN = 512
TQ = 256
SB_HEADS_PER_STEP = 8
TM_TOK = 256
HG_ROWS = 128
TS = 256
SAMPLE_PAD = 8
PAGES_PER_STEP = 8
N_SLOTS = ((2 * N_TOK + N_EXPERTS * (TS - 1)) + TS - 1) // TS * TS
N_TILES = N_SLOTS // TS


def _cparams(n_axes, vmem_mib):
    return pltpu.CompilerParams(dimension_semantics=("arbitrary",) * n_axes,
                                vmem_limit_bytes=vmem_mib * MIB)


def _dot(a, b):
    return jnp.dot(a, b, preferred_element_type=F32)


def _dot_nt(a, b):
    return lax.dot_general(a, b, (((1,), (1,)), ((), ())), preferred_element_type=F32)


def _rms(x, g):
    ms = jnp.mean(x * x, axis=-1, keepdims=True)
    return x * lax.rsqrt(ms + EPS) * g


def _sigmoid(x):
    return 1.0 / (1.0 + jnp.exp(-x))


def _softplus(z):
    return jnp.maximum(z, 0.0) + jnp.log(1.0 + jnp.exp(-jnp.abs(z)))


def _suffix_matrix(n):
    r = lax.broadcasted_iota(I32, (n, n), 0)
    c = lax.broadcasted_iota(I32, (n, n), 1)
    return jnp.where(r >= c, 1.0, 0.0).astype(BF16)


def _split_bf16(x):
    hi = x.astype(BF16)
    lo = (x - hi.astype(F32)).astype(BF16)
    return hi, lo


def _inproj_kernel(xp_ref, xs_ref, g_ref, w_ref,
                   q_ref, kb_ref, vb_ref, kfp_ref, vfp_ref, kfs_ref, vfs_ref, hg_ref,
                   xn_ref, *, n_prompt_blocks):
    i = pl.program_id(0)
    j = pl.program_id(1)
    is_prompt = i < n_prompt_blocks

    @pl.when((j == 0) & is_prompt)
    def _():
        xn_ref[...] = _rms(xp_ref[...], g_ref[...]).astype(BF16)

    @pl.when((j == 0) & jnp.logical_not(is_prompt))
    def _():
        xn_ref[...] = _rms(xs_ref[...], g_ref[...]).astype(BF16)

    def proj():
        return _dot(xn_ref[...], w_ref[...])

    @pl.when(j == 0)
    def _():
        q_ref[...] = (proj() * ATT_SCALE).astype(BF16)

    for col, bf_ref, fp_ref, fs_ref in ((1, kb_ref, kfp_ref, kfs_ref), (2, vb_ref, vfp_ref, vfs_ref)):
        @pl.when((j == col) & is_prompt)
        def _():
            fp_ref[...] = proj()
            bf_ref[...] = fp_ref[...].astype(BF16)

        @pl.when((j == col) & jnp.logical_not(is_prompt))
        def _():
            fs_ref[...] = proj()
            bf_ref[...] = fs_ref[...].astype(BF16)

    @pl.when(j >= 3)
    def _():
        hg_ref[...] = proj()


def _inproj(xp, xs, g, w_bf):
    npb = SEQ // TM_IN
    nblk = N_TOK // TM_IN
    ncol = w_bf.shape[1] // WIDTH
    tok_spec = pl.BlockSpec((TM_IN, WIDTH), lambda i, j: (i, 0))
    return pl.pallas_call(
        functools.partial(_inproj_kernel, n_prompt_blocks=npb),
        grid=(nblk, ncol),
        in_specs=[
            pl.BlockSpec((TM_IN, D_MODEL), lambda i, j: (jnp.minimum(i, npb - 1), 0)),
            pl.BlockSpec((TM_IN, D_MODEL), lambda i, j: (jnp.maximum(i - npb, 0), 0)),
            pl.BlockSpec((1, D_MODEL), lambda i, j: (0, 0)),
            pl.BlockSpec((D_MODEL, WIDTH), lambda i, j: (0, j)),
        ],
        out_specs=[
            tok_spec, tok_spec, tok_spec,
            pl.BlockSpec((TM_IN, WIDTH), lambda i, j: (jnp.minimum(i, npb - 1), 0)),
            pl.BlockSpec((TM_IN, WIDTH), lambda i, j: (jnp.minimum(i, npb - 1), 0)),
            pl.BlockSpec((TM_IN, WIDTH), lambda i, j: (jnp.maximum(i - npb, 0), 0)),
            pl.BlockSpec((TM_IN, WIDTH), lambda i, j: (jnp.maximum(i - npb, 0), 0)),
            pl.BlockSpec((TM_IN, WIDTH), lambda i, j: (i, jnp.clip(j - 3, 0, 3))),
        ],
        out_shape=[
            jax.ShapeDtypeStruct((N_TOK, WIDTH), BF16),
            jax.ShapeDtypeStruct((N_TOK, WIDTH), BF16),
            jax.ShapeDtypeStruct((N_TOK, WIDTH), BF16),
            jax.ShapeDtypeStruct((SEQ, WIDTH), F32),
            jax.ShapeDtypeStruct((SEQ, WIDTH), F32),
            jax.ShapeDtypeStruct((N_SAMPLE, WIDTH), F32),
            jax.ShapeDtypeStruct((N_SAMPLE, WIDTH), F32),
            jax.ShapeDtypeStruct((N_TOK, 4 * WIDTH), F32),
        ],
        scratch_shapes=[pltpu.VMEM((TM_IN, D_MODEL), BF16)],
        compiler_params=_cparams(2, 48),
        name="inproj",
    )(xp, xs, g, w_bf)


def _sb_tiles(qs, ks, vs, biases, valids, u, r_refs, acc_refs):
    n = u.shape[0]
    zs = [_dot_nt(q, k) + b for q, k, b in zip(qs, ks, biases)]
    sps = []
    for z, valid in zip(zs, valids):
        sp = _softplus(z)
        sps.append(sp if valid is None else jnp.where(valid, sp, 0.0))
    pieces = [[jnp.concatenate(_split_bf16(sp[:, c:c + n]), axis=1) for c in range(0, sp.shape[1], n)]
              for sp in sps]
    u2 = jnp.concatenate([u, u], axis=0)
    sums = [[_dot(p, u2) for p in blocks] for blocks in pieces]

    carried = {}
    ws = []
    for z, valid, blocks, r_ref in zip(zs, valids, sums, r_refs):
        r = carried[id(r_ref)][1] if id(r_ref) in carried else r_ref[...]
        parts = [None] * len(blocks)
        for blk in range(len(blocks) - 1, -1, -1):
            parts[blk] = blocks[blk] + r
            r = r + blocks[blk][:, 0:1]
        carried[id(r_ref)] = (r_ref, r)
        s_all = parts[0] if len(parts) == 1 else jnp.concatenate(parts, axis=-1)
        w = jnp.exp(z - s_all)
        ws.append(w if valid is None else jnp.where(valid, w, 0.0))
    for r_ref, r in carried.values():
        r_ref[...] = r

    totals = {}
    for w, v, acc_ref in zip(ws, vs, acc_refs):
        d = _dot(w.astype(BF16), v)
        totals[id(acc_ref)] = (acc_ref, d + totals[id(acc_ref)][1] if id(acc_ref) in totals else d)
    for acc_ref, d in totals.values():
        acc_ref[...] += d


def _head_rms(o, g):
    return o * lax.rsqrt(jnp.mean(o * o, axis=-1, keepdims=True) + EPS) * g


def _sb_prompt_kernel(bias_ref, q_ref, k_ref, v_ref, g_ref, o_ref, acc_ref, r_ref):
    hg = pl.program_id(0)
    i = pl.program_id(1)
    u = _suffix_matrix(TQ)
    acc_ref[...] = jnp.zeros_like(acc_ref)
    r_ref[...] = jnp.zeros_like(r_ref)

    heads = range(SB_HEADS_PER_STEP)
    lanes = [slice(g * HD, (g + 1) * HD) for g in heads]
    biases = [bias_ref[hg * SB_HEADS_PER_STEP + g] for g in heads]
    r_refs = [r_ref.at[g] for g in heads]
    acc_refs = [acc_ref.at[g] for g in heads]

    def tiles(j, valid):
        start = pl.multiple_of(j * TQ, TQ)
        _sb_tiles([q_ref[:, sl] for sl in lanes],
                  [k_ref[pl.ds(start, TQ), sl] for sl in lanes],
                  [v_ref[pl.ds(start, TQ), sl] for sl in lanes],
                  biases, [valid] * SB_HEADS_PER_STEP, u, r_refs, acc_refs)

    row = lax.broadcasted_iota(I32, (TQ, TQ), 0)
    col = lax.broadcasted_iota(I32, (TQ, TQ), 1)
    tiles(i, col < row)

    def body(it, carry):
        tiles(i - 1 - it, None)
        return carry

    lax.fori_loop(0, i, body, 0)
    for g in range(SB_HEADS_PER_STEP):
        o_ref[:, g * HD:(g + 1) * HD] = _head_rms(acc_ref[g], g_ref[...]).astype(BF16)


def _sb_prompt(bias, q_bf, k_bf, v_bf, g):
    gw = SB_HEADS_PER_STEP * HD
    return pl.pallas_call(
        _sb_prompt_kernel,
        grid=(HEADS // SB_HEADS_PER_STEP, SEQ // TQ),
        in_specs=[
            pl.BlockSpec(memory_space=pltpu.SMEM),
            pl.BlockSpec((TQ, gw), lambda h, i: (i, h)),
            pl.BlockSpec((SEQ, gw), lambda h, i: (0, h), pipeline_mode=pl.Buffered(1)),
            pl.BlockSpec((SEQ, gw), lambda h, i: (0, h), pipeline_mode=pl.Buffered(1)),
            pl.BlockSpec((1, HD), lambda h, i: (0, 0)),
        ],
        out_specs=pl.BlockSpec((TQ, gw), lambda h, i: (i, h)),
        out_shape=jax.ShapeDtypeStruct((SEQ, WIDTH), BF16),
        scratch_shapes=[pltpu.VMEM((SB_HEADS_PER_STEP, TQ, HD), F32),
                        pltpu.VMEM((SB_HEADS_PER_STEP, TQ, 1), F32)],
        compiler_params=_cparams(2, 48),
        name="sb_prompt",
    )(bias, q_bf, k_bf, v_bf, g)


SB_ROWS = HEADS * DEC_SEQ
PAGE_COLS = PAGE * HEADS
NEW_COLS = LANES
SUFFIX_BLOCK = 256


def _sb_sample_kernel(pt_ref, q_ref, brow_ref, knew_ref, vnew_ref, g_ref, *rest):
    k_refs = rest[:PAGES_PER_STEP]
    v_refs = rest[PAGES_PER_STEP:2 * PAGES_PER_STEP]
    o_ref, acc_ref, r_ref, kn_ref, vn_ref = rest[2 * PAGES_PER_STEP:]
    c = pl.program_id(1)
    q = q_ref[0]
    row = lax.broadcasted_iota(I32, (SB_ROWS, PAGE_COLS), 0)
    col = lax.broadcasted_iota(I32, (SB_ROWS, PAGE_COLS), 1)
    own = (col % HEADS) == (row // DEC_SEQ)
    bias = brow_ref[...]

    @pl.when(c == 0)
    def _():
        acc_ref[...] = jnp.zeros_like(acc_ref)
        r_ref[...] = jnp.zeros_like(r_ref)
        kn_ref[...] = jnp.zeros_like(kn_ref)
        vn_ref[...] = jnp.zeros_like(vn_ref)
        kn_ref[0:SB_ROWS, :] = knew_ref[0]
        vn_ref[0:SB_ROWS, :] = vnew_ref[0]
        rn = lax.broadcasted_iota(I32, (SB_ROWS, NEW_COLS), 0)
        cn = lax.broadcasted_iota(I32, (SB_ROWS, NEW_COLS), 1)
        valid = ((cn % HEADS) == (rn // DEC_SEQ)) & ((cn // HEADS) < (rn % DEC_SEQ))
        _sb_tiles([q], [kn_ref[...].astype(BF16)], [vn_ref[...].astype(BF16)], [bias[:, 0:NEW_COLS]], [valid],
                  _suffix_matrix(NEW_COLS), [r_ref], [acc_ref])

    order = range(PAGES_PER_STEP - 1, -1, -1)
    _sb_tiles([q] * PAGES_PER_STEP,
              [k_refs[p][0].reshape(PAGE_COLS, HD).astype(BF16) for p in order],
              [v_refs[p][0].reshape(PAGE_COLS, HD).astype(BF16) for p in order],
              [bias] * PAGES_PER_STEP, [own] * PAGES_PER_STEP, _suffix_matrix(SUFFIX_BLOCK),
              [r_ref] * PAGES_PER_STEP, [acc_ref] * PAGES_PER_STEP)

    @pl.when(c == pl.num_programs(1) - 1)
    def _():
        o_ref[0] = _head_rms(acc_ref[...], g_ref[...]).astype(BF16).astype(F32)


def _sb_sample(page_table, q_rows, brow, knew, vnew, g, kcache, vcache):
    n_steps = N_PAGES // PAGES_PER_STEP

    def page_spec(p):
        def imap(b, c, pt):
            return (pt[b * N_PAGES + (n_steps - 1 - c) * PAGES_PER_STEP + p], 0, 0, 0)
        return pl.BlockSpec((1, PAGE, HEADS, HD), imap)

    grid_spec = pltpu.PrefetchScalarGridSpec(
        num_scalar_prefetch=1,
        grid=(DEC_BATCH, n_steps),
        in_specs=[
            pl.BlockSpec((1, SB_ROWS, HD), lambda b, c, pt: (b, 0, 0)),
            pl.BlockSpec((SB_ROWS, PAGE_COLS), lambda b, c, pt: (0, 0)),
            pl.BlockSpec((1, SB_ROWS, HD), lambda b, c, pt: (b, 0, 0)),
            pl.BlockSpec((1, SB_ROWS, HD), lambda b, c, pt: (b, 0, 0)),
            pl.BlockSpec((1, HD), lambda b, c, pt: (0, 0)),
        ] + [page_spec(p) for p in range(PAGES_PER_STEP)] * 2,
        out_specs=pl.BlockSpec((1, SB_ROWS, HD), lambda b, c, pt: (b, 0, 0)),
        scratch_shapes=[
            pltpu.VMEM((SB_ROWS, HD), F32),
            pltpu.VMEM((SB_ROWS, 1), F32),
            pltpu.VMEM((NEW_COLS, HD), F32),
            pltpu.VMEM((NEW_COLS, HD), F32),
        ],
    )
    return pl.pallas_call(
        _sb_sample_kernel,
        grid_spec=grid_spec,
        out_shape=jax.ShapeDtypeStruct((DEC_BATCH, SB_ROWS, HD), F32),
        compiler_params=_cparams(2, 40),
        name="sb_sample",
    )(page_table, q_rows, brow, knew, vnew, g,
      *([kcache] * PAGES_PER_STEP), *([vcache] * PAGES_PER_STEP))


def _hgrn_block(hq, hf, hi, hgate, lb, g_out, state_in, state_out, *, chunk, last, sequential):
    n_chunks = HG_ROWS // chunk
    one_m_lb = 1.0 - lb
    e = jnp.exp(-jnp.abs(hf))
    inv = 1.0 / (1.0 + e)
    sig_pos = jnp.where(hf >= 0, inv, e * inv)
    sig_neg = jnp.where(hf >= 0, e * inv, inv)
    f = lb + one_m_lb * sig_pos
    k = one_m_lb * sig_neg
    g = jnp.log(f)
    q = hq * _sigmoid(hq)

    rin = lax.broadcasted_iota(I32, (HG_ROWS, WIDTH), 0) % chunk
    b = g
    step = 1
    while step < chunk:
        b = b + jnp.where(rin >= step, pltpu.roll(b, step, axis=0), 0.0)
        step *= 2
    cid = lax.broadcasted_iota(I32, (HG_ROWS, WIDTH), 0) // chunk
    b_last_rows = jnp.zeros_like(b)
    b_last = []
    for c in range(n_chunks):
        bl = b[c * chunk + last:c * chunk + last + 1, :]
        b_last.append(bl)
        b_last_rows = jnp.where(cid == c, bl, b_last_rows)

    q_dec = q * jnp.exp(b)
    k_inv = k * jnp.exp(-b)
    k_dec = k * jnp.exp(b_last_rows - b)

    r2 = lax.broadcasted_iota(I32, (HG_ROWS, HG_ROWS), 0)
    c2 = lax.broadcasted_iota(I32, (HG_ROWS, HG_ROWS), 1)
    intra = ((r2 // chunk) == (c2 // chunk)) & (c2 <= r2)
    eye = r2 == c2
    outs = []
    for h in range(HEADS):
        sl = slice(h * HD, (h + 1) * HD)
        qd_h = q_dec[:, sl]
        v_h = hi[:, sl].astype(BF16)
        a = jnp.where(intra, _dot_nt(qd_h.astype(BF16), k_inv[:, sl].astype(BF16)), 0.0)
        o_h = _dot(a.astype(BF16), v_h)
        kdec_t = jnp.transpose(k_dec[:, sl])
        s_cur = state_in(0, h) if sequential else None
        for c in range(n_chunks):
            if not sequential:
                s_cur = state_in(c, h)
            in_rows = (r2 // chunk) == c
            in_cols = (c2 // chunk) == c
            o_h = o_h + _dot(jnp.where(in_rows, qd_h, 0.0).astype(BF16), s_cur.astype(BF16))
            dl = jnp.exp(b_last[c][:, sl])
            dcol = jnp.sum(jnp.where(eye, dl, 0.0), axis=1, keepdims=True)
            s_new = dcol * s_cur + _dot(jnp.where(in_cols, kdec_t, 0.0).astype(BF16), v_h)
            if sequential:
                s_cur = s_new
            else:
                state_out(c, h, s_new)
        if sequential:
            state_out(0, h, s_cur)
        gate = hgate[:, sl]
        outs.append((_head_rms(o_h, g_out) * (gate * _sigmoid(gate))).astype(BF16))
    return jnp.concatenate(outs, axis=-1)


def _hgrn_prompt_kernel(hq_ref, hf_ref, hi_ref, hg_ref, lb_ref, g_ref, o_ref, sfin_ref, s_ref):
    i = pl.program_id(0)

    @pl.when(i == 0)
    def _():
        s_ref[...] = jnp.zeros_like(s_ref)

    def state_in(c, h):
        return s_ref[h]

    def state_out(c, h, s):
        s_ref[h] = s

    o_ref[...] = _hgrn_block(hq_ref[...], hf_ref[...], hi_ref[...], hg_ref[...], lb_ref[...], g_ref[...],
                             state_in, state_out, chunk=HG_CHUNK, last=HG_CHUNK - 1, sequential=True)

    @pl.when(i == pl.num_programs(0) - 1)
    def _():
        sfin_ref[...] = s_ref[...]


def _hgrn_prompt(hgin, lb, g):
    def col(j):
        return pl.BlockSpec((HG_ROWS, WIDTH), lambda i: (i, j))
    return pl.pallas_call(
        _hgrn_prompt_kernel,
        grid=(SEQ // HG_ROWS,),
        in_specs=[col(0), col(1), col(2), col(3),
                  pl.BlockSpec((1, WIDTH), lambda i: (0, 0)),
                  pl.BlockSpec((1, HD), lambda i: (0, 0))],
        out_specs=[pl.BlockSpec((HG_ROWS, WIDTH), lambda i: (i, 0)),
                   pl.BlockSpec((HEADS, HD, HD), lambda i: (0, 0, 0))],
        out_shape=[jax.ShapeDtypeStruct((SEQ, WIDTH), BF16),
                   jax.ShapeDtypeStruct((HEADS, HD, HD), F32)],
        scratch_shapes=[pltpu.VMEM((HEADS, HD, HD), F32)],
        compiler_params=_cparams(1, 32),
        name="hgrn_prompt",
    )(hgin, hgin, hgin, hgin, lb, g)


def _hgrn_sample_kernel(hq_ref, hf_ref, hi_ref, hg_ref, lb_ref, g_ref, s0_ref, o_ref, sout_ref):
    def state_in(c, h):
        return s0_ref[c, h]

    def state_out(c, h, s):
        sout_ref[c, h] = s

    o_ref[...] = _hgrn_block(hq_ref[...], hf_ref[...], hi_ref[...], hg_ref[...], lb_ref[...], g_ref[...],
                             state_in, state_out, chunk=SAMPLE_PAD, last=DEC_SEQ - 1, sequential=False)


def _hgrn_sample(hgin_pad, lb, g, s0):
    bpb = HG_ROWS // SAMPLE_PAD

    def col(j):
        return pl.BlockSpec((HG_ROWS, WIDTH), lambda i: (i, j))
    return pl.pallas_call(
        _hgrn_sample_kernel,
        grid=(DEC_BATCH // bpb,),
        in_specs=[col(0), col(1), col(2), col(3),
                  pl.BlockSpec((1, WIDTH), lambda i: (0, 0)),
                  pl.BlockSpec((1, HD), lambda i: (0, 0)),
                  pl.BlockSpec((bpb, HEADS, HD, HD), lambda i: (i, 0, 0, 0))],
        out_specs=[pl.BlockSpec((HG_ROWS, WIDTH), lambda i: (i, 0)),
                   pl.BlockSpec((bpb, HEADS, HD, HD), lambda i: (i, 0, 0, 0))],
        out_shape=[jax.ShapeDtypeStruct((DEC_BATCH * SAMPLE_PAD, WIDTH), BF16),
                   jax.ShapeDtypeStruct((DEC_BATCH, HEADS, HD, HD), F32)],
        compiler_params=_cparams(1, 48),
        name="hgrn_sample",
    )(hgin_pad, hgin_pad, hgin_pad, hgin_pad, lb, g, s0)


def _post_mixer_kernel(ap_ref, as_ref, bp_ref, bs_ref, xp_ref, xs_ref, wo_ref, gx_ref, wq_ref, h1_ref, qx_ref, *,
                       n_prompt_blocks):
    i = pl.program_id(0)

    def finish(a_ref, b_ref, x_ref):
        y = _dot(a_ref[...], wo_ref[0:WIDTH, :]) + _dot(b_ref[...], wo_ref[WIDTH:2 * WIDTH, :])
        h1 = x_ref[...] + y
        h1_ref[...] = h1
        qx_ref[...] = (_dot(_rms(h1, gx_ref[...]).astype(BF16), wq_ref[...]) * ATT_SCALE).astype(BF16)

    @pl.when(i < n_prompt_blocks)
    def _():
        finish(ap_ref, bp_ref, xp_ref)

    @pl.when(i >= n_prompt_blocks)
    def _():
        finish(as_ref, bs_ref, xs_ref)


def _post_mixer(a_p, a_s, b_p, b_s, xp, xs, wo_bf, gx, wq_bf):
    npb = SEQ // TM_TOK

    def prompt_rows(w):
        return pl.BlockSpec((TM_TOK, w), lambda i: (jnp.minimum(i, npb - 1), 0))

    def sample_rows(w):
        return pl.BlockSpec((TM_TOK, w), lambda i: (jnp.maximum(i - npb, 0), 0))

    return pl.pallas_call(
        functools.partial(_post_mixer_kernel, n_prompt_blocks=npb),
        grid=(N_TOK // TM_TOK,),
        in_specs=[
            prompt_rows(WIDTH), sample_rows(WIDTH), prompt_rows(WIDTH), sample_rows(WIDTH),
            prompt_rows(D_MODEL), sample_rows(D_MODEL),
            pl.BlockSpec((2 * WIDTH, D_MODEL), lambda i: (0, 0)),
            pl.BlockSpec((1, D_MODEL), lambda i: (0, 0)),
            pl.BlockSpec((D_MODEL, XA_WIDTH), lambda i: (0, 0)),
        ],
        out_specs=[pl.BlockSpec((TM_TOK, D_MODEL), lambda i: (i, 0)),
                   pl.BlockSpec((TM_TOK, XA_WIDTH), lambda i: (i, 0))],
        out_shape=[jax.ShapeDtypeStruct((N_TOK, D_MODEL), F32),
                   jax.ShapeDtypeStruct((N_TOK, XA_WIDTH), BF16)],
        compiler_params=_cparams(1, 48),
        name="post_mixer",
    )(a_p, a_s, b_p, b_s, xp, xs, wo_bf, gx, wq_bf)


def _mem_kv_kernel(m_ref, g_ref, w_ref, mk_ref, mv_ref):
    kv = _dot(_rms(m_ref[...], g_ref[...]).astype(BF16), w_ref[...])
    mk_ref[...] = kv[:, 0:XA_WIDTH]
    mv_ref[...] = kv[:, XA_WIDTH:2 * XA_WIDTH]


def _mem_kv(mem, g, w_bf):
    return pl.pallas_call(
        _mem_kv_kernel,
        out_shape=[jax.ShapeDtypeStruct((N_MEM, XA_WIDTH), F32)] * 2,
        compiler_params=pltpu.CompilerParams(vmem_limit_bytes=32 * MIB),
        name="mem_kv",
    )(mem, g, w_bf)


def _softmax_rows(s):
    m = jnp.max(s, axis=-1, keepdims=True)
    e = jnp.exp(s - m)
    return e / jnp.sum(e, axis=-1, keepdims=True)


def _xattn_prompt_kernel(q_ref, mk_ref, mv_ref, o_ref):
    mk = mk_ref[...].astype(BF16)
    mv = mv_ref[...].astype(BF16)
    outs = []
    for h in range(XA_HEADS):
        sl = slice(h * HD, (h + 1) * HD)
        p = _softmax_rows(_dot_nt(q_ref[:, sl], mk[:, sl]))
        outs.append(_dot(p.astype(BF16), mv[:, sl]))
    o_ref[...] = jnp.concatenate(outs, axis=-1).astype(BF16)


def _xattn_prompt(qx, mk, mv):
    return pl.pallas_call(
        _xattn_prompt_kernel,
        grid=(SEQ // TM_TOK,),
        in_specs=[pl.BlockSpec((TM_TOK, XA_WIDTH), lambda i: (i, 0)),
                  pl.BlockSpec((N_MEM, XA_WIDTH), lambda i: (0, 0)),
                  pl.BlockSpec((N_MEM, XA_WIDTH), lambda i: (0, 0))],
        out_specs=pl.BlockSpec((TM_TOK, XA_WIDTH), lambda i: (i, 0)),
        out_shape=jax.ShapeDtypeStruct((SEQ, XA_WIDTH), BF16),
        compiler_params=_cparams(1, 32),
        name="xattn_prompt",
    )(qx, mk, mv)


XA_BATCHES_PER_STEP = 8


def _xattn_sample_kernel(qrep_ref, mk_ref, mv_ref, o_ref):
    rows = XA_HEADS * SAMPLE_PAD
    row = lax.broadcasted_iota(I32, (rows, XA_WIDTH), 0)
    col = lax.broadcasted_iota(I32, (rows, XA_WIDTH), 1)
    diag = (row // SAMPLE_PAD) == (col // HD)
    for bb in range(XA_BATCHES_PER_STEP):
        qbd = jnp.where(diag, qrep_ref[bb], jnp.zeros((), BF16))
        mk = jnp.concatenate([mk_ref[bb, :, h, :] for h in range(XA_HEADS)], axis=-1).astype(BF16)
        mv = jnp.concatenate([mv_ref[bb, :, h, :] for h in range(XA_HEADS)], axis=-1).astype(BF16)
        p = _softmax_rows(_dot_nt(qbd, mk))
        o = jnp.where(diag, _dot(p.astype(BF16), mv), 0.0)
        out = o[0:SAMPLE_PAD]
        for h in range(1, XA_HEADS):
            out = out + o[h * SAMPLE_PAD:(h + 1) * SAMPLE_PAD]
        o_ref[bb] = out.astype(BF16).astype(F32)


def _xattn_sample(qrep, mk, mv):
    rows = XA_HEADS * SAMPLE_PAD
    nb = XA_BATCHES_PER_STEP
    return pl.pallas_call(
        _xattn_sample_kernel,
        grid=(DEC_BATCH // nb,),
        in_specs=[pl.BlockSpec((nb, rows, XA_WIDTH), lambda i: (i, 0, 0)),
                  pl.BlockSpec((nb, N_MEM, XA_HEADS, HD), lambda i: (i, 0, 0, 0)),
                  pl.BlockSpec((nb, N_MEM, XA_HEADS, HD), lambda i: (i, 0, 0, 0))],
        out_specs=pl.BlockSpec((nb, SAMPLE_PAD, XA_WIDTH), lambda i: (i, 0, 0)),
        out_shape=jax.ShapeDtypeStruct((DEC_BATCH, SAMPLE_PAD, XA_WIDTH), F32),
        compiler_params=_cparams(1, 40),
        name="xattn_sample",
    )(qrep, mk, mv)


ROUTE_LANES = LANES


def _router_kernel(ox_ref, h1_ref, wxo_ref, gf_ref, wr_hi_ref, wr_lo_ref, h2_ref, xn_ref, route_ref):
    h2 = h1_ref[...] + _dot(ox_ref[...], wxo_ref[...])
    h2_ref[...] = h2
    xn = _rms(h2, gf_ref[...])
    xn_ref[...] = xn
    x_hi = xn.astype(BF16)
    x_lo = (xn - x_hi.astype(F32)).astype(BF16)
    logits = _dot(x_hi, wr_hi_ref[...]) + (_dot(x_lo, wr_hi_ref[...]) + _dot(x_hi, wr_lo_ref[...]))

    lane = lax.broadcasted_iota(I32, logits.shape, 1).astype(F32)
    neg = jnp.float32(-jnp.inf)
    big = jnp.float32(ROUTE_LANES)
    lg = jnp.where(lane < N_GROUPS, logits, neg)
    gmax = jnp.max(lg, axis=-1, keepdims=True)
    g_idx = jnp.min(jnp.where(lg == gmax, lane, big), axis=-1, keepdims=True)
    g_w = 1.0 / jnp.sum(jnp.exp(lg - gmax), axis=-1, keepdims=True)
    lo_lane = N_GROUPS + g_idx * EXP_PER_GROUP
    le = jnp.where((lane >= lo_lane) & (lane < lo_lane + EXP_PER_GROUP), logits, neg)
    v1 = jnp.max(le, axis=-1, keepdims=True)
    i1 = jnp.min(jnp.where(le == v1, lane, big), axis=-1, keepdims=True)
    le2 = jnp.where(lane == i1, neg, le)
    v2 = jnp.max(le2, axis=-1, keepdims=True)
    i2 = jnp.min(jnp.where(le2 == v2, lane, big), axis=-1, keepdims=True)
    e2 = jnp.exp(v2 - v1)
    w1 = g_w / (1.0 + e2)
    w2 = g_w * e2 / (1.0 + e2)
    route = jnp.where(lane == 0, (i1 - N_GROUPS).astype(F32),
                      jnp.where(lane == 1, (i2 - N_GROUPS).astype(F32),
                                jnp.where(lane == 2, w1, jnp.where(lane == 3, w2, 0.0))))
    route_ref[...] = route


def _router(ox, h1, wxo_bf, gf, wr_hi, wr_lo):
    return pl.pallas_call(
        _router_kernel,
        grid=(N_TOK // TM_TOK,),
        in_specs=[pl.BlockSpec((TM_TOK, XA_WIDTH), lambda i: (i, 0)),
                  pl.BlockSpec((TM_TOK, D_MODEL), lambda i: (i, 0)),
                  pl.BlockSpec((XA_WIDTH, D_MODEL), lambda i: (0, 0)),
                  pl.BlockSpec((1, D_MODEL), lambda i: (0, 0)),
                  pl.BlockSpec((D_MODEL, ROUTE_LANES), lambda i: (0, 0)),
                  pl.BlockSpec((D_MODEL, ROUTE_LANES), lambda i: (0, 0))],
        out_specs=[pl.BlockSpec((TM_TOK, D_MODEL), lambda i: (i, 0)),
                   pl.BlockSpec((TM_TOK, D_MODEL), lambda i: (i, 0)),
                   pl.BlockSpec((TM_TOK, ROUTE_LANES), lambda i: (i, 0))],
        out_shape=[jax.ShapeDtypeStruct((N_TOK, D_MODEL), F32),
                   jax.ShapeDtypeStruct((N_TOK, D_MODEL), F32),
                   jax.ShapeDtypeStruct((N_TOK, ROUTE_LANES), F32)],
        compiler_params=_cparams(1, 40),
        name="router",
    )(ox, h1, wxo_bf, gf, wr_hi, wr_lo)


def _dispatch_kernel(slot_ref, cnt_ref, pad_ref, start_ref, nused_ref,
                     x_ref, xs_hbm, zrow, sems):
    i = pl.program_id(0)
    sem = sems.at[0]
    zsem = sems.at[1]

    @pl.when(i == 0)
    def _():
        zrow[...] = jnp.zeros_like(zrow)

        def pad_rows(fn):
            def per_expert(e, carry):
                base = start_ref[e]

                def body(r, c2):
                    fn(base + r)
                    return c2
                lax.fori_loop(cnt_ref[e], pad_ref[e], body, 0)
                return carry
            lax.fori_loop(0, N_EXPERTS, per_expert, 0)

        pad_rows(lambda row: pltpu.make_async_copy(zrow.at[pl.ds(0, 1)], xs_hbm.at[pl.ds(row, 1)], zsem).start())
        pad_rows(lambda row: pltpu.make_async_copy(zrow.at[pl.ds(0, 1)], xs_hbm.at[pl.ds(0, 1)], zsem).wait())

        def unused_tiles(fn):
            def body(t, carry):
                fn(pl.multiple_of(t * TS, TS))
                return carry
            lax.fori_loop(nused_ref[0], N_TILES, body, 0)

        unused_tiles(lambda r0: pltpu.make_async_copy(zrow, xs_hbm.at[pl.ds(r0, TS)], zsem).start())
        unused_tiles(lambda r0: pltpu.make_async_copy(zrow, xs_hbm.at[pl.ds(0, TS)], zsem).wait())

    base = i * (2 * TM_TOK)

    def body(r, carry):
        t = lax.shift_right_logical(r, 1)
        pltpu.make_async_copy(x_ref.at[pl.ds(t, 1)], xs_hbm.at[pl.ds(slot_ref[base + r], 1)], sem).start()
        return carry
    lax.fori_loop(0, 2 * TM_TOK, body, 0, unroll=8)

    for _ in range(2):
        pltpu.make_async_copy(x_ref, xs_hbm.at[pl.ds(0, TM_TOK)], sem).wait()


def _dispatch(slot, counts, padded, seg_start, n_used, xn):
    grid_spec = pltpu.PrefetchScalarGridSpec(
        num_scalar_prefetch=5,
        grid=(N_TOK // TM_TOK,),
        in_specs=[pl.BlockSpec((TM_TOK, D_MODEL), lambda i, *_: (i, 0))],
        out_specs=pl.BlockSpec(memory_space=pl.ANY),
        scratch_shapes=[pltpu.VMEM((TS, D_MODEL), F32),
                        pltpu.SemaphoreType.DMA((2,))],
    )
    return pl.pallas_call(
        _dispatch_kernel,
        grid_spec=grid_spec,
        out_shape=jax.ShapeDtypeStruct((N_SLOTS, D_MODEL), F32),
        compiler_params=_cparams(1, 16),
        name="dispatch",
    )(slot, counts, padded, seg_start, n_used, xn)


def _moe_kernel(te_ref, nused_ref, first_ref, nxt_ref, par_ref,
                xs_ref, wg_hbm, wu_hbm, wd_hbm, ys_ref,
                wg_f, wu_f, wd_f, wg_bf, wu_bf, wd_bf, wsem):
    i = pl.program_id(0)
    active = i < nused_ref[0]

    def weight_copies(e, b):
        return (pltpu.make_async_copy(wg_hbm.at[e], wg_f.at[b], wsem.at[b]),
                pltpu.make_async_copy(wu_hbm.at[e], wu_f.at[b], wsem.at[b]),
                pltpu.make_async_copy(wd_hbm.at[e], wd_f.at[b], wsem.at[b]))

    @pl.when(active)
    def _():
        @pl.when(first_ref[i] == 1)
        def _():
            b = par_ref[i]

            @pl.when(i == 0)
            def _():
                for c in weight_copies(te_ref[0], b):
                    c.start()

            for c in weight_copies(te_ref[i], b):
                c.wait()

            @pl.when(nxt_ref[i] >= 0)
            def _():
                for c in weight_copies(nxt_ref[i], 1 - b):
                    c.start()

            wg_bf[...] = wg_f[b].astype(BF16)
            wu_bf[...] = wu_f[b].astype(BF16)
            wd_bf[...] = wd_f[b].astype(BF16)

        x = xs_ref[...].astype(BF16)
        a = _dot(x, wg_bf[...])
        u = _dot(x, wu_bf[...])
        hm = (a * _sigmoid(a)) * u
        ys_ref[...] = _dot(hm.astype(BF16), wd_bf[...])

    @pl.when(jnp.logical_not(active))
    def _():
        ys_ref[...] = jnp.zeros_like(ys_ref)


def _moe(tile_expert, n_used, first, nxt, parity, xs, wg, wu, wd):
    grid_spec = pltpu.PrefetchScalarGridSpec(
        num_scalar_prefetch=5,
        grid=(N_TILES,),
        in_specs=[
            pl.BlockSpec((TS, D_MODEL), lambda i, te, nu, *_: (jnp.minimum(i, nu[0] - 1), 0)),
            pl.BlockSpec(memory_space=pl.ANY),
            pl.BlockSpec(memory_space=pl.ANY),
            pl.BlockSpec(memory_space=pl.ANY),
        ],
        out_specs=pl.BlockSpec((TS, D_MODEL), lambda i, *_: (i, 0)),
        scratch_shapes=[
            pltpu.VMEM((2, D_MODEL, EXP_FF), F32),
            pltpu.VMEM((2, D_MODEL, EXP_FF), F32),
            pltpu.VMEM((2, EXP_FF, D_MODEL), F32),
            pltpu.VMEM((D_MODEL, EXP_FF), BF16),
            pltpu.VMEM((D_MODEL, EXP_FF), BF16),
            pltpu.VMEM((EXP_FF, D_MODEL), BF16),
            pltpu.SemaphoreType.DMA((2,)),
        ],
    )
    return pl.pallas_call(
        _moe_kernel,
        grid_spec=grid_spec,
        out_shape=jax.ShapeDtypeStruct((N_SLOTS, D_MODEL), F32),
        compiler_params=_cparams(1, 48),
        name="moe",
    )(tile_expert, n_used, first, nxt, parity, xs, wg, wu, wd)


def _final_kernel(slot_ref, h2_ref, route_ref, g_ref, ys_hbm, yp_ref, ysm_ref, ybuf, sem, *, n_prompt_blocks):
    i = pl.program_id(0)
    n = pl.num_programs(0)
    buf = i % 2

    def gather(blk, b):
        base = blk * (2 * TM_TOK)

        def body(r, carry):
            s = slot_ref[base + r]
            k = r & 1
            t = lax.shift_right_logical(r, 1)
            pltpu.make_async_copy(ys_hbm.at[pl.ds(s, 1)], ybuf.at[b, k, pl.ds(t, 1)], sem.at[b]).start()
            return carry
        lax.fori_loop(0, 2 * TM_TOK, body, 0, unroll=8)

    @pl.when(i == 0)
    def _():
        gather(0, 0)

    @pl.when(i + 1 < n)
    def _():
        gather(i + 1, 1 - buf)

    for k in range(2):
        pltpu.make_async_copy(ys_hbm.at[pl.ds(0, TM_TOK)], ybuf.at[buf, k], sem.at[buf]).wait()

    lane = lax.broadcasted_iota(I32, (TM_TOK, ROUTE_LANES), 1)
    route = route_ref[...]
    w0 = jnp.sum(jnp.where(lane == 2, route, 0.0), axis=-1, keepdims=True)
    w1 = jnp.sum(jnp.where(lane == 3, route, 0.0), axis=-1, keepdims=True)
    y = _rms(h2_ref[...] + (w0 * ybuf[buf, 0] + w1 * ybuf[buf, 1]), g_ref[...])

    @pl.when(i < n_prompt_blocks)
    def _():
        yp_ref[...] = y

    @pl.when(i >= n_prompt_blocks)
    def _():
        ysm_ref[...] = y


def _final(slot, h2, route, g, ys):
    npb = SEQ // TM_TOK
    grid_spec = pltpu.PrefetchScalarGridSpec(
        num_scalar_prefetch=1,
        grid=(N_TOK // TM_TOK,),
        in_specs=[pl.BlockSpec((TM_TOK, D_MODEL), lambda i, s: (i, 0)),
                  pl.BlockSpec((TM_TOK, ROUTE_LANES), lambda i, s: (i, 0)),
                  pl.BlockSpec((1, D_MODEL), lambda i, s: (0, 0)),
                  pl.BlockSpec(memory_space=pl.ANY)],
        out_specs=[pl.BlockSpec((TM_TOK, D_MODEL), lambda i, s: (jnp.minimum(i, npb - 1), 0)),
                   pl.BlockSpec((TM_TOK, D_MODEL), lambda i, s: (jnp.maximum(i - npb, 0), 0))],
        scratch_shapes=[pltpu.VMEM((2, 2, TM_TOK, D_MODEL), F32),
                        pltpu.SemaphoreType.DMA((2,))],
    )
    return pl.pallas_call(
        functools.partial(_final_kernel, n_prompt_blocks=npb),
        grid_spec=grid_spec,
        out_shape=[jax.ShapeDtypeStruct((SEQ, D_MODEL), F32),
                   jax.ShapeDtypeStruct((N_SAMPLE, D_MODEL), F32)],
        compiler_params=_cparams(1, 40),
        name="final",
    )(slot, h2, route, g, ys)


def _routing_tables(route):
    eid = route[:, 0:2].astype(I32).reshape(-1)
    onehot = (eid[:, None] == jnp.arange(N_EXPERTS, dtype=I32)[None, :]).astype(I32)
    csum = jnp.cumsum(onehot, axis=0)
    rank = jnp.sum(onehot * (csum - onehot), axis=1)
    counts = csum[-1]
    padded = (counts + TS - 1) // TS * TS
    seg_end = jnp.cumsum(padded)
    seg_start = seg_end - padded
    slot = jnp.sum(onehot * seg_start[None, :], axis=1) + rank
    n_used = (seg_end[-1] // TS).astype(I32)
    tile_start = jnp.arange(N_TILES, dtype=I32) * TS
    tile_expert = jnp.sum((seg_end[None, :] <= tile_start[:, None]).astype(I32), axis=1)
    used = jnp.arange(N_TILES) < n_used
    last_expert = jnp.max(jnp.where(used, tile_expert, 0))
    tile_expert = jnp.where(used, tile_expert, last_expert).astype(I32)
    prev = jnp.concatenate([jnp.full((1,), -1, I32), tile_expert[:-1]])
    first = (used & (tile_expert != prev)).astype(I32)
    parity = ((jnp.cumsum(first) - 1) % 2).astype(I32)
    e_ids = jnp.arange(N_EXPERTS, dtype=I32)
    later = (e_ids[None, :] > e_ids[:, None]) & (counts[None, :] > 0)
    next_nonempty = jnp.min(jnp.where(later, e_ids[None, :], N_EXPERTS), axis=1)
    next_nonempty = jnp.where(next_nonempty == N_EXPERTS, -1, next_nonempty)
    nxt = jnp.sum((tile_expert[:, None] == e_ids[None, :]).astype(I32) * next_nonempty[None, :], axis=1)
    return (slot.astype(I32), counts, padded, seg_start, tile_expert, n_used.reshape(1),
            first, nxt.astype(I32), parity)


def _rep_heads(x, n_heads):
    xpad = jnp.pad(x, ((0, 0), (0, SAMPLE_PAD - DEC_SEQ), (0, 0)))
    return jnp.tile(xpad, (1, n_heads, 1))


def kernel(x_prompt, x_sample, mem_prompt, cache_sb_k, cache_sb_v, state_hgrn, cache_mem_k, cache_mem_v,
           page_table, norm_mix, w_in, sb_logit_bias, lb_param, g_sb_out, g_hg_out, w_out, norm_xa, norm_mem,
           w_xq, w_xkv, w_xo, norm_ffn, w_route_group, w_route_expert, w_e_gate, w_e_up, w_e_down, norm_final):
    xp = x_prompt.reshape(SEQ, D_MODEL)
    xs = x_sample.reshape(N_SAMPLE, D_MODEL)

    lb = jnp.cumsum(jax.nn.softmax(lb_param.astype(F32), axis=0), axis=0)[0].reshape(1, WIDTH)
    w_in_bf = w_in[0].astype(BF16)
    w_out_bf = w_out[0].astype(BF16)
    w_xq_bf = w_xq[0].astype(BF16)
    w_xkv_bf = w_xkv[0].astype(BF16)
    w_xo_bf = w_xo[0].astype(BF16)
    w_r = jnp.concatenate([w_route_group[0],
                           jnp.transpose(w_route_expert[0], (1, 0, 2)).reshape(D_MODEL, N_EXPERTS)], axis=1)
    w_r = jnp.pad(w_r, ((0, 0), (0, ROUTE_LANES - w_r.shape[1])))
    w_r_hi = w_r.astype(BF16)
    w_r_lo = (w_r - w_r_hi.astype(F32)).astype(BF16)
    bias = sb_logit_bias[0].astype(F32)

    q_bf, k_bf, v_bf, kf_p, vf_p, kf_s, vf_s, hgin = _inproj(xp, xs, norm_mix[0].reshape(1, D_MODEL), w_in_bf)
    a_p = _sb_prompt(bias, q_bf, k_bf, v_bf, g_sb_out[0].reshape(1, HD))
    q_rows = jnp.transpose(q_bf[SEQ:].reshape(DEC_BATCH, DEC_SEQ, HEADS, HD), (0, 2, 1, 3))
    q_rows = q_rows.reshape(DEC_BATCH, SB_ROWS, HD)
    brow = jnp.broadcast_to(jnp.repeat(bias, DEC_SEQ)[:, None], (SB_ROWS, PAGE_COLS))
    a_s = _sb_sample(page_table.reshape(-1).astype(I32), q_rows, brow,
                     kf_s.reshape(DEC_BATCH, SB_ROWS, HD), vf_s.reshape(DEC_BATCH, SB_ROWS, HD),
                     g_sb_out[0].reshape(1, HD), cache_sb_k[0], cache_sb_v[0])
    a_s = jnp.transpose(a_s.reshape(DEC_BATCH, HEADS, DEC_SEQ, HD), (0, 2, 1, 3)).reshape(N_SAMPLE, WIDTH)
    g_hg = g_hg_out[0].reshape(1, HD)
    b_p, s_p = _hgrn_prompt(hgin, lb, g_hg)
    hgin_s = jnp.pad(hgin[SEQ:].reshape(DEC_BATCH, DEC_SEQ, 4 * WIDTH),
                     ((0, 0), (0, SAMPLE_PAD - DEC_SEQ), (0, 0))).reshape(DEC_BATCH * SAMPLE_PAD, 4 * WIDTH)
    b_s, s_s = _hgrn_sample(hgin_s, lb, g_hg, state_hgrn[0])
    b_s = b_s.reshape(DEC_BATCH, SAMPLE_PAD, WIDTH)[:, :DEC_SEQ].reshape(N_SAMPLE, WIDTH)
    h1, qx = _post_mixer(a_p, a_s.astype(BF16), b_p, b_s, xp, xs, w_out_bf,
                         norm_xa[0].reshape(1, D_MODEL), w_xq_bf)

    mk_p, mv_p = _mem_kv(mem_prompt[0], norm_mem[0].reshape(1, D_MODEL), w_xkv_bf)
    ox_p = _xattn_prompt(qx, mk_p, mv_p)
    qxrep = _rep_heads(qx[SEQ:].reshape(DEC_BATCH, DEC_SEQ, XA_WIDTH), XA_HEADS)
    ox_s = _xattn_sample(qxrep, cache_mem_k[0], cache_mem_v[0])
    ox = jnp.concatenate([ox_p, ox_s[:, :DEC_SEQ].reshape(N_SAMPLE, XA_WIDTH).astype(BF16)], axis=0)
    h2, xn2, route = _router(ox, h1, w_xo_bf, norm_ffn[0].reshape(1, D_MODEL), w_r_hi, w_r_lo)

    slot, counts, padded, seg_start, tile_expert, n_used, first, nxt, parity = _routing_tables(route)
    xs_sorted = _dispatch(slot, counts, padded, seg_start, n_used, xn2)
    ys_sorted = _moe(tile_expert, n_used, first, nxt, parity, xs_sorted, w_e_gate[0], w_e_up[0], w_e_down[0])
    y_p, y_s = _final(slot, h2, route, norm_final.reshape(1, D_MODEL), ys_sorted)

    sb_shape_p = (1, 1, SEQ, HEADS, HD)
    sb_shape_s = (1, DEC_BATCH, DEC_SEQ, HEADS, HD)
    return (y_p.reshape(1, SEQ, D_MODEL), y_s.reshape(DEC_BATCH, DEC_SEQ, D_MODEL),
            kf_p.reshape(sb_shape_p), vf_p.reshape(sb_shape_p), s_p.reshape(1, 1, HEADS, HD, HD),
            mk_p.reshape(1, 1, N_MEM, XA_HEADS, HD), mv_p.reshape(1, 1, N_MEM, XA_HEADS, HD),
            kf_s.reshape(sb_shape_s), vf_s.reshape(sb_shape_s), s_s.reshape(1, DEC_BATCH, HEADS, HD, HD))
```

```python
import functools
import math

import jax
import jax.numpy as jnp
from jax import lax
from jax.experimental import pallas as pl
from jax.experimental.pallas import tpu as pltpu

F32 = jnp.float32
BF16 = jnp.bfloat16
I32 = jnp.int32

D_MODEL = 2048
SEQ = 8192
DEC_BATCH = 128
DEC_SEQ = 4
N_SAMPLE = DEC_BATCH * DEC_SEQ
N_TOK = SEQ + N_SAMPLE
PAGE = 128
N_PAGES = 16
HEADS = 8
HD = 128
WIDTH = HEADS * HD
XA_HEADS = 4
XA_WIDTH = XA_HEADS * HD
N_MEM = 256
N_GROUPS = 4
EXP_PER_GROUP = 8
N_EXPERTS = 32
EXP_FF = 512
HG_CHUNK = 32
EPS = 1e-6
ATT_SCALE = 1.0 / math.sqrt(HD)

LANES = 128
SUBLANES = 8
MIB = 1024 * 1024

TM_IN = 512
TQ = 256
SB_HEADS_PER_STEP = 8
TM_TOK = 256
HG_ROWS = 128
TS = 256
SAMPLE_PAD = 8
PAGES_PER_STEP = 8
N_SLOTS = ((2 * N_TOK + N_EXPERTS * (TS - 1)) + TS - 1) // TS * TS
N_TILES = N_SLOTS // TS


def _cparams(n_axes, vmem_mib):
    return pltpu.CompilerParams(dimension_semantics=("arbitrary",) * n_axes,
                                vmem_limit_bytes=vmem_mib * MIB)


def _dot(a, b):
    return jnp.dot(a, b, preferred_element_type=F32)


def _dot_nt(a, b):
    return lax.dot_general(a, b, (((1,), (1,)), ((), ())), preferred_element_type=F32)


def _rms(x, g):
    ms = jnp.mean(x * x, axis=-1, keepdims=True)
    return x * lax.rsqrt(ms + EPS) * g


def _sigmoid(x):
    return 1.0 / (1.0 + jnp.exp(-x))


def _softplus(z):
    return jnp.maximum(z, 0.0) + jnp.log(1.0 + jnp.exp(-jnp.abs(z)))


def _suffix_matrix(n):
    r = lax.broadcasted_iota(I32, (n, n), 0)
    c = lax.broadcasted_iota(I32, (n, n), 1)
    return jnp.where(r >= c, 1.0, 0.0).astype(BF16)


def _split_bf16(x):
    hi = x.astype(BF16)
    lo = (x - hi.astype(F32)).astype(BF16)
    return hi, lo


def _inproj_kernel(xp_ref, xs_ref, g_ref, w_ref,
                   q_ref, kb_ref, vb_ref, kfp_ref, vfp_ref, kfs_ref, vfs_ref, hg_ref,
                   xn_ref, *, n_prompt_blocks):
    i = pl.program_id(0)
    j = pl.program_id(1)
    is_prompt = i < n_prompt_blocks

    @pl.when((j == 0) & is_prompt)
    def _():
        xn_ref[...] = _rms(xp_ref[...], g_ref[...]).astype(BF16)

    @pl.when((j == 0) & jnp.logical_not(is_prompt))
    def _():
        xn_ref[...] = _rms(xs_ref[...], g_ref[...]).astype(BF16)

    def proj():
        return _dot(xn_ref[...], w_ref[...])

    @pl.when(j == 0)
    def _():
        q_ref[...] = (proj() * ATT_SCALE).astype(BF16)

    for col, bf_ref, fp_ref, fs_ref in ((1, kb_ref, kfp_ref, kfs_ref), (2, vb_ref, vfp_ref, vfs_ref)):
        @pl.when((j == col) & is_prompt)
        def _():
            fp_ref[...] = proj()
            bf_ref[...] = fp_ref[...].astype(BF16)

        @pl.when((j == col) & jnp.logical_not(is_prompt))
        def _():
            fs_ref[...] = proj()
            bf_ref[...] = fs_ref[...].astype(BF16)

    @pl.when(j >= 3)
    def _():
        hg_ref[...] = proj()


def _inproj(xp, xs, g, w_bf):
    npb = SEQ // TM_IN
    nblk = N_TOK // TM_IN
    ncol = w_bf.shape[1] // WIDTH
    tok_spec = pl.BlockSpec((TM_IN, WIDTH), lambda i, j: (i, 0))
    return pl.pallas_call(
        functools.partial(_inproj_kernel, n_prompt_blocks=npb),
        grid=(nblk, ncol),
        in_specs=[
            pl.BlockSpec((TM_IN, D_MODEL), lambda i, j: (jnp.minimum(i, npb - 1), 0)),
            pl.BlockSpec((TM_IN, D_MODEL), lambda i, j: (jnp.maximum(i - npb, 0), 0)),
            pl.BlockSpec((1, D_MODEL), lambda i, j: (0, 0)),
            pl.BlockSpec((D_MODEL, WIDTH), lambda i, j: (0, j)),
        ],
        out_specs=[
            tok_spec, tok_spec, tok_spec,
            pl.BlockSpec((TM_IN, WIDTH), lambda i, j: (jnp.minimum(i, npb - 1), 0)),
            pl.BlockSpec((TM_IN, WIDTH), lambda i, j: (jnp.minimum(i, npb - 1), 0)),
            pl.BlockSpec((TM_IN, WIDTH), lambda i, j: (jnp.maximum(i - npb, 0), 0)),
            pl.BlockSpec((TM_IN, WIDTH), lambda i, j: (jnp.maximum(i - npb, 0), 0)),
            pl.BlockSpec((TM_IN, WIDTH), lambda i, j: (i, jnp.clip(j - 3, 0, 3))),
        ],
        out_shape=[
            jax.ShapeDtypeStruct((N_TOK, WIDTH), BF16),
            jax.ShapeDtypeStruct((N_TOK, WIDTH), BF16),
            jax.ShapeDtypeStruct((N_TOK, WIDTH), BF16),
            jax.ShapeDtypeStruct((SEQ, WIDTH), F32),
            jax.ShapeDtypeStruct((SEQ, WIDTH), F32),
            jax.ShapeDtypeStruct((N_SAMPLE, WIDTH), F32),
            jax.ShapeDtypeStruct((N_SAMPLE, WIDTH), F32),
            jax.ShapeDtypeStruct((N_TOK, 4 * WIDTH), F32),
        ],
        scratch_shapes=[pltpu.VMEM((TM_IN, D_MODEL), BF16)],
        compiler_params=_cparams(2, 48),
        name="inproj",
    )(xp, xs, g, w_bf)


def _sb_tiles(qs, ks, vs, biases, valids, u, r_refs, acc_refs):
    n = u.shape[0]
    zs = [_dot_nt(q, k) + b for q, k, b in zip(qs, ks, biases)]
    sps = []
    for z, valid in zip(zs, valids):
        sp = _softplus(z)
        sps.append(sp if valid is None else jnp.where(valid, sp, 0.0))
    sums = [[_dot(sp[:, c:c + n].astype(BF16), u) for c in range(0, sp.shape[1], n)] for sp in sps]

    carried = {}
    ws = []
    for z, valid, blocks, r_ref in zip(zs, valids, sums, r_refs):
        r = carried[id(r_ref)][1] if id(r_ref) in carried else r_ref[...]
        parts = [None] * len(blocks)
        for blk in range(len(blocks) - 1, -1, -1):
            parts[blk] = blocks[blk] + r
            r = r + blocks[blk][:, 0:1]
        carried[id(r_ref)] = (r_ref, r)
        s_all = parts[0] if len(parts) == 1 else jnp.concatenate(parts, axis=-1)
        w = jnp.exp(z - s_all)
        ws.append(w if valid is None else jnp.where(valid, w, 0.0))
    for r_ref, r in carried.values():
        r_ref[...] = r

    totals = {}
    for w, v, acc_ref in zip(ws, vs, acc_refs):
        d = _dot(w.astype(BF16), v)
        totals[id(acc_ref)] = (acc_ref, d + totals[id(acc_ref)][1] if id(acc_ref) in totals else d)
    for acc_ref, d in totals.values():
        acc_ref[...] += d


def _head_rms(o, g):
    return o * lax.rsqrt(jnp.mean(o * o, axis=-1, keepdims=True) + EPS) * g


def _sb_prompt_kernel(bias_ref, q_ref, k_ref, v_ref, g_ref, o_ref, acc_ref, r_ref):
    hg = pl.program_id(0)
    i = pl.program_id(1)
    u = _suffix_matrix(TQ)
    acc_ref[...] = jnp.zeros_like(acc_ref)
    r_ref[...] = jnp.zeros_like(r_ref)

    heads = range(SB_HEADS_PER_STEP)
    lanes = [slice(g * HD, (g + 1) * HD) for g in heads]
    biases = [bias_ref[hg * SB_HEADS_PER_STEP + g] for g in heads]
    r_refs = [r_ref.at[g] for g in heads]
    acc_refs = [acc_ref.at[g] for g in heads]

    def tiles(j, valid):
        start = pl.multiple_of(j * TQ, TQ)
        _sb_tiles([q_ref[:, sl] for sl in lanes],
                  [k_ref[pl.ds(start, TQ), sl] for sl in lanes],
                  [v_ref[pl.ds(start, TQ), sl] for sl in lanes],
                  biases, [valid] * SB_HEADS_PER_STEP, u, r_refs, acc_refs)

    row = lax.broadcasted_iota(I32, (TQ, TQ), 0)
    col = lax.broadcasted_iota(I32, (TQ, TQ), 1)
    tiles(i, col < row)

    def body(it, carry):
        tiles(i - 1 - it, None)
        return carry

    lax.fori_loop(0, i, body, 0)
    for g in range(SB_HEADS_PER_STEP):
        o_ref[:, g * HD:(g + 1) * HD] = _head_rms(acc_ref[g], g_ref[...]).astype(BF16)


def _sb_prompt(bias, q_bf, k_bf, v_bf, g):
    gw = SB_HEADS_PER_STEP * HD
    return pl.pallas_call(
        _sb_prompt_kernel,
        grid=(HEADS // SB_HEADS_PER_STEP, SEQ // TQ),
        in_specs=[
            pl.BlockSpec(memory_space=pltpu.SMEM),
            pl.BlockSpec((TQ, gw), lambda h, i: (i, h)),
            pl.BlockSpec((SEQ, gw), lambda h, i: (0, h), pipeline_mode=pl.Buffered(1)),
            pl.BlockSpec((SEQ, gw), lambda h, i: (0, h), pipeline_mode=pl.Buffered(1)),
            pl.BlockSpec((1, HD), lambda h, i: (0, 0)),
        ],
        out_specs=pl.BlockSpec((TQ, gw), lambda h, i: (i, h)),
        out_shape=jax.ShapeDtypeStruct((SEQ, WIDTH), BF16),
        scratch_shapes=[pltpu.VMEM((SB_HEADS_PER_STEP, TQ, HD), F32),
                        pltpu.VMEM((SB_HEADS_PER_STEP, TQ, 1), F32)],
        compiler_params=_cparams(2, 48),
        name="sb_prompt",
    )(bias, q_bf, k_bf, v_bf, g)


SB_ROWS = HEADS * DEC_SEQ
PAGE_COLS = PAGE * HEADS
NEW_COLS = LANES
SUFFIX_BLOCK = 256


def _sb_sample_kernel(pt_ref, q_ref, brow_ref, knew_ref, vnew_ref, g_ref, *rest):
    k_refs = rest[:PAGES_PER_STEP]
    v_refs = rest[PAGES_PER_STEP:2 * PAGES_PER_STEP]
    o_ref, acc_ref, r_ref, kn_ref, vn_ref = rest[2 * PAGES_PER_STEP:]
    c = pl.program_id(1)
    q = q_ref[0]
    row = lax.broadcasted_iota(I32, (SB_ROWS, PAGE_COLS), 0)
    col = lax.broadcasted_iota(I32, (SB_ROWS, PAGE_COLS), 1)
    own = (col % HEADS) == (row // DEC_SEQ)
    bias = brow_ref[...]

    @pl.when(c == 0)
    def _():
        acc_ref[...] = jnp.zeros_like(acc_ref)
        r_ref[...] = jnp.zeros_like(r_ref)
        kn_ref[...] = jnp.zeros_like(kn_ref)
        vn_ref[...] = jnp.zeros_like(vn_ref)
        kn_ref[0:SB_ROWS, :] = knew_ref[0]
        vn_ref[0:SB_ROWS, :] = vnew_ref[0]
        rn = lax.broadcasted_iota(I32, (SB_ROWS, NEW_COLS), 0)
        cn = lax.broadcasted_iota(I32, (SB_ROWS, NEW_COLS), 1)
        valid = ((cn % HEADS) == (rn // DEC_SEQ)) & ((cn // HEADS) < (rn % DEC_SEQ))
        _sb_tiles([q], [kn_ref[...].astype(BF16)], [vn_ref[...].astype(BF16)], [bias[:, 0:NEW_COLS]], [valid],
                  _suffix_matrix(NEW_COLS), [r_ref], [acc_ref])

    order = range(PAGES_PER_STEP - 1, -1, -1)
    _sb_tiles([q] * PAGES_PER_STEP,
              [k_refs[p][0].reshape(PAGE_COLS, HD).astype(BF16) for p in order],
              [v_refs[p][0].reshape(PAGE_COLS, HD).astype(BF16) for p in order],
              [bias] * PAGES_PER_STEP, [own] * PAGES_PER_STEP, _suffix_matrix(SUFFIX_BLOCK),
              [r_ref] * PAGES_PER_STEP, [acc_ref] * PAGES_PER_STEP)

    @pl.when(c == pl.num_programs(1) - 1)
    def _():
        o_ref[0] = _head_rms(acc_ref[...], g_ref[...]).astype(BF16).astype(F32)


def _sb_sample(page_table, q_rows, brow, knew, vnew, g, kcache, vcache):
    n_steps = N_PAGES // PAGES_PER_STEP

    def page_spec(p):
        def imap(b, c, pt):
            return (pt[b * N_PAGES + (n_steps - 1 - c) * PAGES_PER_STEP + p], 0, 0, 0)
        return pl.BlockSpec((1, PAGE, HEADS, HD), imap)

    grid_spec = pltpu.PrefetchScalarGridSpec(
        num_scalar_prefetch=1,
        grid=(DEC_BATCH, n_steps),
        in_specs=[
            pl.BlockSpec((1, SB_ROWS, HD), lambda b, c, pt: (b, 0, 0)),
            pl.BlockSpec((SB_ROWS, PAGE_COLS), lambda b, c, pt: (0, 0)),
            pl.BlockSpec((1, SB_ROWS, HD), lambda b, c, pt: (b, 0, 0)),
            pl.BlockSpec((1, SB_ROWS, HD), lambda b, c, pt: (b, 0, 0)),
            pl.BlockSpec((1, HD), lambda b, c, pt: (0, 0)),
        ] + [page_spec(p) for p in range(PAGES_PER_STEP)] * 2,
        out_specs=pl.BlockSpec((1, SB_ROWS, HD), lambda b, c, pt: (b, 0, 0)),
        scratch_shapes=[
            pltpu.VMEM((SB_ROWS, HD), F32),
            pltpu.VMEM((SB_ROWS, 1), F32),
            pltpu.VMEM((NEW_COLS, HD), F32),
            pltpu.VMEM((NEW_COLS, HD), F32),
        ],
    )
    return pl.pallas_call(
        _sb_sample_kernel,
        grid_spec=grid_spec,
        out_shape=jax.ShapeDtypeStruct((DEC_BATCH, SB_ROWS, HD), F32),
        compiler_params=_cparams(2, 40),
        name="sb_sample",
    )(page_table, q_rows, brow, knew, vnew, g,
      *([kcache] * PAGES_PER_STEP), *([vcache] * PAGES_PER_STEP))


def _hgrn_block(hq, hf, hi, hgate, lb, g_out, state_in, state_out, *, chunk, last, sequential):
    n_chunks = HG_ROWS // chunk
    one_m_lb = 1.0 - lb
    e = jnp.exp(-jnp.abs(hf))
    inv = 1.0 / (1.0 + e)
    sig_pos = jnp.where(hf >= 0, inv, e * inv)
    sig_neg = jnp.where(hf >= 0, e * inv, inv)
    f = lb + one_m_lb * sig_pos
    k = one_m_lb * sig_neg
    g = jnp.log(f)
    q = hq * _sigmoid(hq)

    rin = lax.broadcasted_iota(I32, (HG_ROWS, WIDTH), 0) % chunk
    b = g
    step = 1
    while step < chunk:
        b = b + jnp.where(rin >= step, pltpu.roll(b, step, axis=0), 0.0)
        step *= 2
    cid = lax.broadcasted_iota(I32, (HG_ROWS, WIDTH), 0) // chunk
    b_last_rows = jnp.zeros_like(b)
    b_last = []
    for c in range(n_chunks):
        bl = b[c * chunk + last:c * chunk + last + 1, :]
        b_last.append(bl)
        b_last_rows = jnp.where(cid == c, bl, b_last_rows)

    q_dec = q * jnp.exp(b)
    k_inv = k * jnp.exp(-b)
    k_dec = k * jnp.exp(b_last_rows - b)

    r2 = lax.broadcasted_iota(I32, (HG_ROWS, HG_ROWS), 0)
    c2 = lax.broadcasted_iota(I32, (HG_ROWS, HG_ROWS), 1)
    intra = ((r2 // chunk) == (c2 // chunk)) & (c2 <= r2)
    eye = r2 == c2
    outs = []
    for h in range(HEADS):
        sl = slice(h * HD, (h + 1) * HD)
        qd_h = q_dec[:, sl]
        v_h = hi[:, sl].astype(BF16)
        a = jnp.where(intra, _dot_nt(qd_h.astype(BF16), k_inv[:, sl].astype(BF16)), 0.0)
        o_h = _dot(a.astype(BF16), v_h)
        kdec_t = jnp.transpose(k_dec[:, sl])
        s_cur = state_in(0, h) if sequential else None
        for c in range(n_chunks):
            if not sequential:
                s_cur = state_in(c, h)
            in_rows = (r2 // chunk) == c
            in_cols = (c2 // chunk) == c
            o_h = o_h + _dot(jnp.where(in_rows, qd_h, 0.0).astype(BF16), s_cur.astype(BF16))
            dl = jnp.exp(b_last[c][:, sl])
            dcol = jnp.sum(jnp.where(eye, dl, 0.0), axis=1, keepdims=True)
            s_new = dcol * s_cur + _dot(jnp.where(in_cols, kdec_t, 0.0).astype(BF16), v_h)
            if sequential:
                s_cur = s_new
            else:
                state_out(c, h, s_new)
        if sequential:
            state_out(0, h, s_cur)
        gate = hgate[:, sl]
        outs.append((_head_rms(o_h, g_out) * (gate * _sigmoid(gate))).astype(BF16))
    return jnp.concatenate(outs, axis=-1)


def _hgrn_prompt_kernel(hq_ref, hf_ref, hi_ref, hg_ref, lb_ref, g_ref, o_ref, sfin_ref, s_ref):
    i = pl.program_id(0)

    @pl.when(i == 0)
    def _():
        s_ref[...] = jnp.zeros_like(s_ref)

    def state_in(c, h):
        return s_ref[h]

    def state_out(c, h, s):
        s_ref[h] = s

    o_ref[...] = _hgrn_block(hq_ref[...], hf_ref[...], hi_ref[...], hg_ref[...], lb_ref[...], g_ref[...],
                             state_in, state_out, chunk=HG_CHUNK, last=HG_CHUNK - 1, sequential=True)

    @pl.when(i == pl.num_programs(0) - 1)
    def _():
        sfin_ref[...] = s_ref[...]


def _hgrn_prompt(hgin, lb, g):
    def col(j):
        return pl.BlockSpec((HG_ROWS, WIDTH), lambda i: (i, j))
    return pl.pallas_call(
        _hgrn_prompt_kernel,
        grid=(SEQ // HG_ROWS,),
        in_specs=[col(0), col(1), col(2), col(3),
                  pl.BlockSpec((1, WIDTH), lambda i: (0, 0)),
                  pl.BlockSpec((1, HD), lambda i: (0, 0))],
        out_specs=[pl.BlockSpec((HG_ROWS, WIDTH), lambda i: (i, 0)),
                   pl.BlockSpec((HEADS, HD, HD), lambda i: (0, 0, 0))],
        out_shape=[jax.ShapeDtypeStruct((SEQ, WIDTH), BF16),
                   jax.ShapeDtypeStruct((HEADS, HD, HD), F32)],
        scratch_shapes=[pltpu.VMEM((HEADS, HD, HD), F32)],
        compiler_params=_cparams(1, 32),
        name="hgrn_prompt",
    )(hgin, hgin, hgin, hgin, lb, g)


def _hgrn_sample_kernel(hq_ref, hf_ref, hi_ref, hg_ref, lb_ref, g_ref, s0_ref, o_ref, sout_ref):
    def state_in(c, h):
        return s0_ref[c, h]

    def state_out(c, h, s):
        sout_ref[c, h] = s

    o_ref[...] = _hgrn_block(hq_ref[...], hf_ref[...], hi_ref[...], hg_ref[...], lb_ref[...], g_ref[...],
                             state_in, state_out, chunk=SAMPLE_PAD, last=DEC_SEQ - 1, sequential=False)


def _hgrn_sample(hgin_pad, lb, g, s0):
    bpb = HG_ROWS // SAMPLE_PAD

    def col(j):
        return pl.BlockSpec((HG_ROWS, WIDTH), lambda i: (i, j))
    return pl.pallas_call(
        _hgrn_sample_kernel,
        grid=(DEC_BATCH // bpb,),
        in_specs=[col(0), col(1), col(2), col(3),
                  pl.BlockSpec((1, WIDTH), lambda i: (0, 0)),
                  pl.BlockSpec((1, HD), lambda i: (0, 0)),
                  pl.BlockSpec((bpb, HEADS, HD, HD), lambda i: (i, 0, 0, 0))],
        out_specs=[pl.BlockSpec((HG_ROWS, WIDTH), lambda i: (i, 0)),
                   pl.BlockSpec((bpb, HEADS, HD, HD), lambda i: (i, 0, 0, 0))],
        out_shape=[jax.ShapeDtypeStruct((DEC_BATCH * SAMPLE_PAD, WIDTH), BF16),
                   jax.ShapeDtypeStruct((DEC_BATCH, HEADS, HD, HD), F32)],
        compiler_params=_cparams(1, 48),
        name="hgrn_sample",
    )(hgin_pad, hgin_pad, hgin_pad, hgin_pad, lb, g, s0)


def _post_mixer_kernel(ap_ref, as_ref, bp_ref, bs_ref, xp_ref, xs_ref, wo_ref, gx_ref, wq_ref, h1_ref, qx_ref, *,
                       n_prompt_blocks):
    i = pl.program_id(0)

    def finish(a_ref, b_ref, x_ref):
        y = _dot(a_ref[...], wo_ref[0:WIDTH, :]) + _dot(b_ref[...], wo_ref[WIDTH:2 * WIDTH, :])
        h1 = x_ref[...] + y
        h1_ref[...] = h1
        qx_ref[...] = (_dot(_rms(h1, gx_ref[...]).astype(BF16), wq_ref[...]) * ATT_SCALE).astype(BF16)

    @pl.when(i < n_prompt_blocks)
    def _():
        finish(ap_ref, bp_ref, xp_ref)

    @pl.when(i >= n_prompt_blocks)
    def _():
        finish(as_ref, bs_ref, xs_ref)


def _post_mixer(a_p, a_s, b_p, b_s, xp, xs, wo_bf, gx, wq_bf):
    npb = SEQ // TM_TOK

    def prompt_rows(w):
        return pl.BlockSpec((TM_TOK, w), lambda i: (jnp.minimum(i, npb - 1), 0))

    def sample_rows(w):
        return pl.BlockSpec((TM_TOK, w), lambda i: (jnp.maximum(i - npb, 0), 0))

    return pl.pallas_call(
        functools.partial(_post_mixer_kernel, n_prompt_blocks=npb),
        grid=(N_TOK // TM_TOK,),
        in_specs=[
            prompt_rows(WIDTH), sample_rows(WIDTH), prompt_rows(WIDTH), sample_rows(WIDTH),
            prompt_rows(D_MODEL), sample_rows(D_MODEL),
            pl.BlockSpec((2 * WIDTH, D_MODEL), lambda i: (0, 0)),
            pl.BlockSpec((1, D_MODEL), lambda i: (0, 0)),
            pl.BlockSpec((D_MODEL, XA_WIDTH), lambda i: (0, 0)),
        ],
        out_specs=[pl.BlockSpec((TM_TOK, D_MODEL), lambda i: (i, 0)),
                   pl.BlockSpec((TM_TOK, XA_WIDTH), lambda i: (i, 0))],
        out_shape=[jax.ShapeDtypeStruct((N_TOK, D_MODEL), F32),
                   jax.ShapeDtypeStruct((N_TOK, XA_WIDTH), BF16)],
        compiler_params=_cparams(1, 48),
        name="post_mixer",
    )(a_p, a_s, b_p, b_s, xp, xs, wo_bf, gx, wq_bf)


def _mem_kv_kernel(m_ref, g_ref, w_ref, mk_ref, mv_ref):
    kv = _dot(_rms(m_ref[...], g_ref[...]).astype(BF16), w_ref[...])
    mk_ref[...] = kv[:, 0:XA_WIDTH]
    mv_ref[...] = kv[:, XA_WIDTH:2 * XA_WIDTH]


def _mem_kv(mem, g, w_bf):
    return pl.pallas_call(
        _mem_kv_kernel,
        out_shape=[jax.ShapeDtypeStruct((N_MEM, XA_WIDTH), F32)] * 2,
        compiler_params=pltpu.CompilerParams(vmem_limit_bytes=32 * MIB),
        name="mem_kv",
    )(mem, g, w_bf)


def _softmax_rows(s):
    m = jnp.max(s, axis=-1, keepdims=True)
    e = jnp.exp(s - m)
    return e / jnp.sum(e, axis=-1, keepdims=True)


def _xattn_prompt_kernel(q_ref, mk_ref, mv_ref, o_ref):
    mk = mk_ref[...].astype(BF16)
    mv = mv_ref[...].astype(BF16)
    outs = []
    for h in range(XA_HEADS):
        sl = slice(h * HD, (h + 1) * HD)
        p = _softmax_rows(_dot_nt(q_ref[:, sl], mk[:, sl]))
        outs.append(_dot(p.astype(BF16), mv[:, sl]))
    o_ref[...] = jnp.concatenate(outs, axis=-1).astype(BF16)


def _xattn_prompt(qx, mk, mv):
    return pl.pallas_call(
        _xattn_prompt_kernel,
        grid=(SEQ // TM_TOK,),
        in_specs=[pl.BlockSpec((TM_TOK, XA_WIDTH), lambda i: (i, 0)),
                  pl.BlockSpec((N_MEM, XA_WIDTH), lambda i: (0, 0)),
                  pl.BlockSpec((N_MEM, XA_WIDTH), lambda i: (0, 0))],
        out_specs=pl.BlockSpec((TM_TOK, XA_WIDTH), lambda i: (i, 0)),
        out_shape=jax.ShapeDtypeStruct((SEQ, XA_WIDTH), BF16),
        compiler_params=_cparams(1, 32),
        name="xattn_prompt",
    )(qx, mk, mv)


XA_BATCHES_PER_STEP = 8


def _xattn_sample_kernel(qrep_ref, mk_ref, mv_ref, o_ref):
    rows = XA_HEADS * SAMPLE_PAD
    row = lax.broadcasted_iota(I32, (rows, XA_WIDTH), 0)
    col = lax.broadcasted_iota(I32, (rows, XA_WIDTH), 1)
    diag = (row // SAMPLE_PAD) == (col // HD)
    for bb in range(XA_BATCHES_PER_STEP):
        qbd = jnp.where(diag, qrep_ref[bb], jnp.zeros((), BF16))
        mk = jnp.concatenate([mk_ref[bb, :, h, :] for h in range(XA_HEADS)], axis=-1).astype(BF16)
        mv = jnp.concatenate([mv_ref[bb, :, h, :] for h in range(XA_HEADS)], axis=-1).astype(BF16)
        p = _softmax_rows(_dot_nt(qbd, mk))
        o = jnp.where(diag, _dot(p.astype(BF16), mv), 0.0)
        out = o[0:SAMPLE_PAD]
        for h in range(1, XA_HEADS):
            out = out + o[h * SAMPLE_PAD:(h + 1) * SAMPLE_PAD]
        o_ref[bb] = out.astype(BF16).astype(F32)


def _xattn_sample(qrep, mk, mv):
    rows = XA_HEADS * SAMPLE_PAD
    nb = XA_BATCHES_PER_STEP
    return pl.pallas_call(
        _xattn_sample_kernel,
        grid=(DEC_BATCH // nb,),
        in_specs=[pl.BlockSpec((nb, rows, XA_WIDTH), lambda i: (i, 0, 0)),
                  pl.BlockSpec((nb, N_MEM, XA_HEADS, HD), lambda i: (i, 0, 0, 0)),
                  pl.BlockSpec((nb, N_MEM, XA_HEADS, HD), lambda i: (i, 0, 0, 0))],
        out_specs=pl.BlockSpec((nb, SAMPLE_PAD, XA_WIDTH), lambda i: (i, 0, 0)),
        out_shape=jax.ShapeDtypeStruct((DEC_BATCH, SAMPLE_PAD, XA_WIDTH), F32),
        compiler_params=_cparams(1, 40),
        name="xattn_sample",
    )(qrep, mk, mv)


ROUTE_LANES = LANES


def _router_kernel(ox_ref, h1_ref, wxo_ref, gf_ref, wr_hi_ref, wr_lo_ref, h2_ref, xn_ref, route_ref):
    h2 = h1_ref[...] + _dot(ox_ref[...], wxo_ref[...])
    h2_ref[...] = h2
    xn = _rms(h2, gf_ref[...])
    xn_ref[...] = xn
    x_hi = xn.astype(BF16)
    x_lo = (xn - x_hi.astype(F32)).astype(BF16)
    logits = _dot(x_hi, wr_hi_ref[...]) + (_dot(x_lo, wr_hi_ref[...]) + _dot(x_hi, wr_lo_ref[...]))

    lane = lax.broadcasted_iota(I32, logits.shape, 1).astype(F32)
    neg = jnp.float32(-jnp.inf)
    big = jnp.float32(ROUTE_LANES)
    lg = jnp.where(lane < N_GROUPS, logits, neg)
    gmax = jnp.max(lg, axis=-1, keepdims=True)
    g_idx = jnp.min(jnp.where(lg == gmax, lane, big), axis=-1, keepdims=True)
    g_w = 1.0 / jnp.sum(jnp.exp(lg - gmax), axis=-1, keepdims=True)
    lo_lane = N_GROUPS + g_idx * EXP_PER_GROUP
    le = jnp.where((lane >= lo_lane) & (lane < lo_lane + EXP_PER_GROUP), logits, neg)
    v1 = jnp.max(le, axis=-1, keepdims=True)
    i1 = jnp.min(jnp.where(le == v1, lane, big), axis=-1, keepdims=True)
    le2 = jnp.where(lane == i1, neg, le)
    v2 = jnp.max(le2, axis=-1, keepdims=True)
    i2 = jnp.min(jnp.where(le2 == v2, lane, big), axis=-1, keepdims=True)
    e2 = jnp.exp(v2 - v1)
    w1 = g_w / (1.0 + e2)
    w2 = g_w * e2 / (1.0 + e2)
    route = jnp.where(lane == 0, (i1 - N_GROUPS).astype(F32),
                      jnp.where(lane == 1, (i2 - N_GROUPS).astype(F32),
                                jnp.where(lane == 2, w1, jnp.where(lane == 3, w2, 0.0))))
    route_ref[...] = route


def _router(ox, h1, wxo_bf, gf, wr_hi, wr_lo):
    return pl.pallas_call(
        _router_kernel,
        grid=(N_TOK // TM_TOK,),
        in_specs=[pl.BlockSpec((TM_TOK, XA_WIDTH), lambda i: (i, 0)),
                  pl.BlockSpec((TM_TOK, D_MODEL), lambda i: (i, 0)),
                  pl.BlockSpec((XA_WIDTH, D_MODEL), lambda i: (0, 0)),
                  pl.BlockSpec((1, D_MODEL), lambda i: (0, 0)),
                  pl.BlockSpec((D_MODEL, ROUTE_LANES), lambda i: (0, 0)),
                  pl.BlockSpec((D_MODEL, ROUTE_LANES), lambda i: (0, 0))],
        out_specs=[pl.BlockSpec((TM_TOK, D_MODEL), lambda i: (i, 0)),
                   pl.BlockSpec((TM_TOK, D_MODEL), lambda i: (i, 0)),
                   pl.BlockSpec((TM_TOK, ROUTE_LANES), lambda i: (i, 0))],
        out_shape=[jax.ShapeDtypeStruct((N_TOK, D_MODEL), F32),
                   jax.ShapeDtypeStruct((N_TOK, D_MODEL), F32),
                   jax.ShapeDtypeStruct((N_TOK, ROUTE_LANES), F32)],
        compiler_params=_cparams(1, 40),
        name="router",
    )(ox, h1, wxo_bf, gf, wr_hi, wr_lo)


def _dispatch_kernel(slot_ref, cnt_ref, pad_ref, start_ref, nused_ref,
                     x_ref, xs_hbm, zrow, sems):
    i = pl.program_id(0)
    sem = sems.at[0]
    zsem = sems.at[1]

    @pl.when(i == 0)
    def _():
        zrow[...] = jnp.zeros_like(zrow)

        def pad_rows(fn):
            def per_expert(e, carry):
                base = start_ref[e]

                def body(r, c2):
                    fn(base + r)
                    return c2
                lax.fori_loop(cnt_ref[e], pad_ref[e], body, 0)
                return carry
            lax.fori_loop(0, N_EXPERTS, per_expert, 0)

        pad_rows(lambda row: pltpu.make_async_copy(zrow.at[pl.ds(0, 1)], xs_hbm.at[pl.ds(row, 1)], zsem).start())
        pad_rows(lambda row: pltpu.make_async_copy(zrow.at[pl.ds(0, 1)], xs_hbm.at[pl.ds(0, 1)], zsem).wait())

        def unused_tiles(fn):
            def body(t, carry):
                fn(pl.multiple_of(t * TS, TS))
                return carry
            lax.fori_loop(nused_ref[0], N_TILES, body, 0)

        unused_tiles(lambda r0: pltpu.make_async_copy(zrow, xs_hbm.at[pl.ds(r0, TS)], zsem).start())
        unused_tiles(lambda r0: pltpu.make_async_copy(zrow, xs_hbm.at[pl.ds(0, TS)], zsem).wait())

    base = i * (2 * TM_TOK)

    for r in range(2 * TM_TOK):
        pltpu.make_async_copy(x_ref.at[pl.ds(r // 2, 1)], xs_hbm.at[pl.ds(slot_ref[base + r], 1)], sem).start()

    for _ in range(2):
        pltpu.make_async_copy(x_ref, xs_hbm.at[pl.ds(0, TM_TOK)], sem).wait()


def _dispatch(slot, counts, padded, seg_start, n_used, xn):
    grid_spec = pltpu.PrefetchScalarGridSpec(
        num_scalar_prefetch=5,
        grid=(N_TOK // TM_TOK,),
        in_specs=[pl.BlockSpec((TM_TOK, D_MODEL), lambda i, *_: (i, 0))],
        out_specs=pl.BlockSpec(memory_space=pl.ANY),
        scratch_shapes=[pltpu.VMEM((TS, D_MODEL), F32),
                        pltpu.SemaphoreType.DMA((2,))],
    )
    return pl.pallas_call(
        _dispatch_kernel,
        grid_spec=grid_spec,
        out_shape=jax.ShapeDtypeStruct((N_SLOTS, D_MODEL), F32),
        compiler_params=_cparams(1, 16),
        name="dispatch",
    )(slot, counts, padded, seg_start, n_used, xn)


def _moe_kernel(te_ref, nused_ref, first_ref, nxt_ref, par_ref,
                xs_ref, wg_hbm, wu_hbm, wd_hbm, ys_ref,
                wg_f, wu_f, wd_f, wg_bf, wu_bf, wd_bf, wsem):
    i = pl.program_id(0)
    active = i < nused_ref[0]

    def weight_copies(e, b):
        return (pltpu.make_async_copy(wg_hbm.at[e], wg_f.at[b], wsem.at[b]),
                pltpu.make_async_copy(wu_hbm.at[e], wu_f.at[b], wsem.at[b]),
                pltpu.make_async_copy(wd_hbm.at[e], wd_f.at[b], wsem.at[b]))

    @pl.when(active)
    def _():
        @pl.when(first_ref[i] == 1)
        def _():
            b = par_ref[i]

            @pl.when(i == 0)
            def _():
                for c in weight_copies(te_ref[0], b):
                    c.start()

            for c in weight_copies(te_ref[i], b):
                c.wait()

            @pl.when(nxt_ref[i] >= 0)
            def _():
                for c in weight_copies(nxt_ref[i], 1 - b):
                    c.start()

            wg_bf[...] = wg_f[b].astype(BF16)
            wu_bf[...] = wu_f[b].astype(BF16)
            wd_bf[...] = wd_f[b].astype(BF16)

        x = xs_ref[...].astype(BF16)
        a = _dot(x, wg_bf[...])
        u = _dot(x, wu_bf[...])
        hm = (a * _sigmoid(a)) * u
        ys_ref[...] = _dot(hm.astype(BF16), wd_bf[...])

    @pl.when(jnp.logical_not(active))
    def _():
        ys_ref[...] = jnp.zeros_like(ys_ref)


def _moe(tile_expert, n_used, first, nxt, parity, xs, wg, wu, wd):
    grid_spec = pltpu.PrefetchScalarGridSpec(
        num_scalar_prefetch=5,
        grid=(N_TILES,),
        in_specs=[
            pl.BlockSpec((TS, D_MODEL), lambda i, te, nu, *_: (jnp.minimum(i, nu[0] - 1), 0)),
            pl.BlockSpec(memory_space=pl.ANY),
            pl.BlockSpec(memory_space=pl.ANY),
            pl.BlockSpec(memory_space=pl.ANY),
        ],
        out_specs=pl.BlockSpec((TS, D_MODEL), lambda i, *_: (i, 0)),
        scratch_shapes=[
            pltpu.VMEM((2, D_MODEL, EXP_FF), F32),
            pltpu.VMEM((2, D_MODEL, EXP_FF), F32),
            pltpu.VMEM((2, EXP_FF, D_MODEL), F32),
            pltpu.VMEM((D_MODEL, EXP_FF), BF16),
            pltpu.VMEM((D_MODEL, EXP_FF), BF16),
            pltpu.VMEM((EXP_FF, D_MODEL), BF16),
            pltpu.SemaphoreType.DMA((2,)),
        ],
    )
    return pl.pallas_call(
        _moe_kernel,
        grid_spec=grid_spec,
        out_shape=jax.ShapeDtypeStruct((N_SLOTS, D_MODEL), F32),
        compiler_params=_cparams(1, 48),
        name="moe",
    )(tile_expert, n_used, first, nxt, parity, xs, wg, wu, wd)


def _final_kernel(slot_ref, h2_ref, route_ref, g_ref, ys_hbm, yp_ref, ysm_ref, ybuf, sem, *, n_prompt_blocks):
    i = pl.program_id(0)
    n = pl.num_programs(0)
    buf = i % 2

    def gather(blk, b):
        base = blk * (2 * TM_TOK)
        for r in range(2 * TM_TOK):
            pltpu.make_async_copy(ys_hbm.at[pl.ds(slot_ref[base + r], 1)], ybuf.at[b, r % 2, pl.ds(r // 2, 1)],
                                  sem.at[b]).start()

    @pl.when(i == 0)
    def _():
        gather(0, 0)

    for b in range(2):
        @pl.when((i + 1 < n) & ((i + 1) % 2 == b))
        def _():
            gather(i + 1, b)

    for k in range(2):
        pltpu.make_async_copy(ys_hbm.at[pl.ds(0, TM_TOK)], ybuf.at[buf, k], sem.at[buf]).wait()

    lane = lax.broadcasted_iota(I32, (TM_TOK, ROUTE_LANES), 1)
    route = route_ref[...]
    w0 = jnp.sum(jnp.where(lane == 2, route, 0.0), axis=-1, keepdims=True)
    w1 = jnp.sum(jnp.where(lane == 3, route, 0.0), axis=-1, keepdims=True)
    y = _rms(h2_ref[...] + (w0 * ybuf[buf, 0] + w1 * ybuf[buf, 1]), g_ref[...])

    @pl.when(i < n_prompt_blocks)
    def _():
        yp_ref[...] = y

    @pl.when(i >= n_prompt_blocks)
    def _():
        ysm_ref[...] = y


def _final(slot, h2, route, g, ys):
    npb = SEQ // TM_TOK
    grid_spec = pltpu.PrefetchScalarGridSpec(
        num_scalar_prefetch=1,
        grid=(N_TOK // TM_TOK,),
        in_specs=[pl.BlockSpec((TM_TOK, D_MODEL), lambda i, s: (i, 0)),
                  pl.BlockSpec((TM_TOK, ROUTE_LANES), lambda i, s: (i, 0)),
                  pl.BlockSpec((1, D_MODEL), lambda i, s: (0, 0)),
                  pl.BlockSpec(memory_space=pl.ANY)],
        out_specs=[pl.BlockSpec((TM_TOK, D_MODEL), lambda i, s: (jnp.minimum(i, npb - 1), 0)),
                   pl.BlockSpec((TM_TOK, D_MODEL), lambda i, s: (jnp.maximum(i - npb, 0), 0))],
        scratch_shapes=[pltpu.VMEM((2, 2, TM_TOK, D_MODEL), F32),
                        pltpu.SemaphoreType.DMA((2,))],
    )
    return pl.pallas_call(
        functools.partial(_final_kernel, n_prompt_blocks=npb),
        grid_spec=grid_spec,
        out_shape=[jax.ShapeDtypeStruct((SEQ, D_MODEL), F32),
                   jax.ShapeDtypeStruct((N_SAMPLE, D_MODEL), F32)],
        compiler_params=_cparams(1, 40),
        name="final",
    )(slot, h2, route, g, ys)


def _routing_tables(route):
    eid = route[:, 0:2].astype(I32).reshape(-1)
    onehot = (eid[:, None] == jnp.arange(N_EXPERTS, dtype=I32)[None, :]).astype(I32)
    csum = jnp.cumsum(onehot, axis=0)
    rank = jnp.sum(onehot * (csum - onehot), axis=1)
    counts = csum[-1]
    padded = (counts + TS - 1) // TS * TS
    seg_end = jnp.cumsum(padded)
    seg_start = seg_end - padded
    slot = jnp.sum(onehot * seg_start[None, :], axis=1) + rank
    n_used = (seg_end[-1] // TS).astype(I32)
    tile_start = jnp.arange(N_TILES, dtype=I32) * TS
    tile_expert = jnp.sum((seg_end[None, :] <= tile_start[:, None]).astype(I32), axis=1)
    used = jnp.arange(N_TILES) < n_used
    last_expert = jnp.max(jnp.where(used, tile_expert, 0))
    tile_expert = jnp.where(used, tile_expert, last_expert).astype(I32)
    prev = jnp.concatenate([jnp.full((1,), -1, I32), tile_expert[:-1]])
    first = (used & (tile_expert != prev)).astype(I32)
    parity = ((jnp.cumsum(first) - 1) % 2).astype(I32)
    e_ids = jnp.arange(N_EXPERTS, dtype=I32)
    later = (e_ids[None, :] > e_ids[:, None]) & (counts[None, :] > 0)
    next_nonempty = jnp.min(jnp.where(later, e_ids[None, :], N_EXPERTS), axis=1)
    next_nonempty = jnp.where(next_nonempty == N_EXPERTS, -1, next_nonempty)
    nxt = jnp.sum((tile_expert[:, None] == e_ids[None, :]).astype(I32) * next_nonempty[None, :], axis=1)
    return (slot.astype(I32), counts, padded, seg_start, tile_expert, n_used.reshape(1),
            first, nxt.astype(I32), parity)


def _rep_heads(x, n_heads):
    xpad = jnp.pad(x, ((0, 0), (0, SAMPLE_PAD - DEC_SEQ), (0, 0)))
    return jnp.tile(xpad, (1, n_heads, 1))


def kernel(x_prompt, x_sample, mem_prompt, cache_sb_k, cache_sb_v, state_hgrn, cache_mem_k, cache_mem_v,
           page_table, norm_mix, w_in, sb_logit_bias, lb_param, g_sb_out, g_hg_out, w_out, norm_xa, norm_mem,
           w_xq, w_xkv, w_xo, norm_ffn, w_route_group, w_route_expert, w_e_gate, w_e_up, w_e_down, norm_final):
    xp = x_prompt.reshape(SEQ, D_MODEL)
    xs = x_sample.reshape(N_SAMPLE, D_MODEL)

    lb = jnp.cumsum(jax.nn.softmax(lb_param.astype(F32), axis=0), axis=0)[0].reshape(1, WIDTH)
    w_in_bf = w_in[0].astype(BF16)
    w_out_bf = w_out[0].astype(BF16)
    w_xq_bf = w_xq[0].astype(BF16)
    w_xkv_bf = w_xkv[0].astype(BF16)
    w_xo_bf = w_xo[0].astype(BF16)
    w_r = jnp.concatenate([w_route_group[0],
                           jnp.transpose(w_route_expert[0], (1, 0, 2)).reshape(D_MODEL, N_EXPERTS)], axis=1)
    w_r = jnp.pad(w_r, ((0, 0), (0, ROUTE_LANES - w_r.shape[1])))
    w_r_hi = w_r.astype(BF16)
    w_r_lo = (w_r - w_r_hi.astype(F32)).astype(BF16)
    bias = sb_logit_bias[0].astype(F32)

    q_bf, k_bf, v_bf, kf_p, vf_p, kf_s, vf_s, hgin = _inproj(xp, xs, norm_mix[0].reshape(1, D_MODEL), w_in_bf)
    a_p = _sb_prompt(bias, q_bf, k_bf, v_bf, g_sb_out[0].reshape(1, HD))
    q_rows = jnp.transpose(q_bf[SEQ:].reshape(DEC_BATCH, DEC_SEQ, HEADS, HD), (0, 2, 1, 3))
    q_rows = q_rows.reshape(DEC_BATCH, SB_ROWS, HD)
    brow = jnp.broadcast_to(jnp.repeat(bias, DEC_SEQ)[:, None], (SB_ROWS, PAGE_COLS))
    a_s = _sb_sample(page_table.reshape(-1).astype(I32), q_rows, brow,
                     kf_s.reshape(DEC_BATCH, SB_ROWS, HD), vf_s.reshape(DEC_BATCH, SB_ROWS, HD),
                     g_sb_out[0].reshape(1, HD), cache_sb_k[0], cache_sb_v[0])
    a_s = jnp.transpose(a_s.reshape(DEC_BATCH, HEADS, DEC_SEQ, HD), (0, 2, 1, 3)).reshape(N_SAMPLE, WIDTH)
    g_hg = g_hg_out[0].reshape(1, HD)
    b_p, s_p = _hgrn_prompt(hgin, lb, g_hg)
    hgin_s = jnp.pad(hgin[SEQ:].reshape(DEC_BATCH, DEC_SEQ, 4 * WIDTH),
                     ((0, 0), (0, SAMPLE_PAD - DEC_SEQ), (0, 0))).reshape(DEC_BATCH * SAMPLE_PAD, 4 * WIDTH)
    b_s, s_s = _hgrn_sample(hgin_s, lb, g_hg, state_hgrn[0])
    b_s = b_s.reshape(DEC_BATCH, SAMPLE_PAD, WIDTH)[:, :DEC_SEQ].reshape(N_SAMPLE, WIDTH)
    h1, qx = _post_mixer(a_p, a_s.astype(BF16), b_p, b_s, xp, xs, w_out_bf,
                         norm_xa[0].reshape(1, D_MODEL), w_xq_bf)

    mk_p, mv_p = _mem_kv(mem_prompt[0], norm_mem[0].reshape(1, D_MODEL), w_xkv_bf)
    ox_p = _xattn_prompt(qx, mk_p, mv_p)
    qxrep = _rep_heads(qx[SEQ:].reshape(DEC_BATCH, DEC_SEQ, XA_WIDTH), XA_HEADS)
    ox_s = _xattn_sample(qxrep, cache_mem_k[0], cache_mem_v[0])
    ox = jnp.concatenate([ox_p, ox_s[:, :DEC_SEQ].reshape(N_SAMPLE, XA_WIDTH).astype(BF16)], axis=0)
    h2, xn2, route = _router(ox, h1, w_xo_bf, norm_ffn[0].reshape(1, D_MODEL), w_r_hi, w_r_lo)

    slot, counts, padded, seg_start, tile_expert, n_used, first, nxt, parity = _routing_tables(route)
    xs_sorted = _dispatch(slot, counts, padded, seg_start, n_used, xn2)
    ys_sorted = _moe(tile_expert, n_used, first, nxt, parity, xs_sorted, w_e_gate[0], w_e_up[0], w_e_down[0])
    y_p, y_s = _final(slot, h2, route, norm_final.reshape(1, D_MODEL), ys_sorted)

    sb_shape_p = (1, 1, SEQ, HEADS, HD)
    sb_shape_s = (1, DEC_BATCH, DEC_SEQ, HEADS, HD)
    return (y_p.reshape(1, SEQ, D_MODEL), y_s.reshape(DEC_BATCH, DEC_SEQ, D_MODEL),
            kf_p.reshape(sb_shape_p), vf_p.reshape(sb_shape_p), s_p.reshape(1, 1, HEADS, HD, HD),
            mk_p.reshape(1, 1, N_MEM, XA_HEADS, HD), mv_p.reshape(1, 1, N_MEM, XA_HEADS, HD),
            kf_s.reshape(sb_shape_s), vf_s.reshape(sb_shape_s), s_s.reshape(1, DEC_BATCH, HEADS, HD, HD))
```

```python
import functools
import math

import jax
import jax.numpy as jnp
from jax import lax
from jax.experimental import pallas as pl
from jax.experimental.pallas import tpu as pltpu

F32 = jnp.float32
BF16 = jnp.bfloat16
I32 = jnp.int32

D_MODEL = 2048
SEQ = 8192
DEC_BATCH = 128
DEC_SEQ = 4
N_SAMPLE = DEC_BATCH * DEC_SEQ
N_TOK = SEQ + N_SAMPLE
PAGE = 128
N_PAGES = 16
HEADS = 8
HD = 128
WIDTH = HEADS * HD
XA_HEADS = 4
XA_WIDTH = XA_HEADS * HD
N_MEM = 256
N_GROUPS = 4
EXP_PER_GROUP = 8
N_EXPERTS = 32
EXP_FF = 512
HG_CHUNK = 32
EPS = 1e-6
ATT_SCALE = 1.0 / math.sqrt(HD)

LANES = 128
SUBLANES = 8
MIB = 1024 * 1024

TM_IN = 512
TQ = 256
SB_HEADS_PER_STEP = 8
TM_TOK = 256
HG_ROWS = 128
TS = 256
SAMPLE_PAD = 8
PAGES_PER_STEP = 16
N_SLOTS = ((2 * N_TOK + N_EXPERTS * (TS - 1)) + TS - 1) // TS * TS
N_TILES = N_SLOTS // TS


def _cparams(n_axes, vmem_mib):
    return pltpu.CompilerParams(dimension_semantics=("arbitrary",) * n_axes,
                                vmem_limit_bytes=vmem_mib * MIB)


def _dot(a, b):
    return jnp.dot(a, b, preferred_element_type=F32)


def _dot_nt(a, b):
    return lax.dot_general(a, b, (((1,), (1,)), ((), ())), preferred_element_type=F32)


def _rms(x, g):
    ms = jnp.mean(x * x, axis=-1, keepdims=True)
    return x * lax.rsqrt(ms + EPS) * g


def _sigmoid(x):
    return 1.0 / (1.0 + jnp.exp(-x))


def _softplus(z):
    return jnp.maximum(z, 0.0) + jnp.log(1.0 + jnp.exp(-jnp.abs(z)))


def _suffix_matrix(n):
    r = lax.broadcasted_iota(I32, (n, n), 0)
    c = lax.broadcasted_iota(I32, (n, n), 1)
    return jnp.where(r >= c, 1.0, 0.0).astype(BF16)


def _split_bf16(x):
    hi = x.astype(BF16)
    lo = (x - hi.astype(F32)).astype(BF16)
    return hi, lo


def _inproj_kernel(xp_ref, xs_ref, g_ref, w_ref,
                   q_ref, kb_ref, vb_ref, kfp_ref, vfp_ref, kfs_ref, vfs_ref, hg_ref,
                   xn_ref, *, n_prompt_blocks):
    i = pl.program_id(0)
    j = pl.program_id(1)
    is_prompt = i < n_prompt_blocks

    @pl.when((j == 0) & is_prompt)
    def _():
        xn_ref[...] = _rms(xp_ref[...], g_ref[...]).astype(BF16)

    @pl.when((j == 0) & jnp.logical_not(is_prompt))
    def _():
        xn_ref[...] = _rms(xs_ref[...], g_ref[...]).astype(BF16)

    def proj():
        return _dot(xn_ref[...], w_ref[...])

    @pl.when(j == 0)
    def _():
        q_ref[...] = (proj() * ATT_SCALE).astype(BF16)

    for col, bf_ref, fp_ref, fs_ref in ((1, kb_ref, kfp_ref, kfs_ref), (2, vb_ref, vfp_ref, vfs_ref)):
        @pl.when((j == col) & is_prompt)
        def _():
            fp_ref[...] = proj()
            bf_ref[...] = fp_ref[...].astype(BF16)

        @pl.when((j == col) & jnp.logical_not(is_prompt))
        def _():
            fs_ref[...] = proj()
            bf_ref[...] = fs_ref[...].astype(BF16)

    @pl.when(j >= 3)
    def _():
        hg_ref[...] = proj()


def _inproj(xp, xs, g, w_bf):
    npb = SEQ // TM_IN
    nblk = N_TOK // TM_IN
    ncol = w_bf.shape[1] // WIDTH
    tok_spec = pl.BlockSpec((TM_IN, WIDTH), lambda i, j: (i, 0))
    return pl.pallas_call(
        functools.partial(_inproj_kernel, n_prompt_blocks=npb),
        grid=(nblk, ncol),
        in_specs=[
            pl.BlockSpec((TM_IN, D_MODEL), lambda i, j: (jnp.minimum(i, npb - 1), 0)),
            pl.BlockSpec((TM_IN, D_MODEL), lambda i, j: (jnp.maximum(i - npb, 0), 0)),
            pl.BlockSpec((1, D_MODEL), lambda i, j: (0, 0)),
            pl.BlockSpec((D_MODEL, WIDTH), lambda i, j: (0, j)),
        ],
        out_specs=[
            tok_spec, tok_spec, tok_spec,
            pl.BlockSpec((TM_IN, WIDTH), lambda i, j: (jnp.minimum(i, npb - 1), 0)),
            pl.BlockSpec((TM_IN, WIDTH), lambda i, j: (jnp.minimum(i, npb - 1), 0)),
            pl.BlockSpec((TM_IN, WIDTH), lambda i, j: (jnp.maximum(i - npb, 0), 0)),
            pl.BlockSpec((TM_IN, WIDTH), lambda i, j: (jnp.maximum(i - npb, 0), 0)),
            pl.BlockSpec((TM_IN, WIDTH), lambda i, j: (i, jnp.clip(j - 3, 0, 3))),
        ],
        out_shape=[
            jax.ShapeDtypeStruct((N_TOK, WIDTH), BF16),
            jax.ShapeDtypeStruct((N_TOK, WIDTH), BF16),
            jax.ShapeDtypeStruct((N_TOK, WIDTH), BF16),
            jax.ShapeDtypeStruct((SEQ, WIDTH), F32),
            jax.ShapeDtypeStruct((SEQ, WIDTH), F32),
            jax.ShapeDtypeStruct((N_SAMPLE, WIDTH), F32),
            jax.ShapeDtypeStruct((N_SAMPLE, WIDTH), F32),
            jax.ShapeDtypeStruct((N_TOK, 4 * WIDTH), F32),
        ],
        scratch_shapes=[pltpu.VMEM((TM_IN, D_MODEL), BF16)],
        compiler_params=_cparams(2, 48),
        name="inproj",
    )(xp, xs, g, w_bf)


def _sb_tiles(qs, ks, vs, biases, valids, u, r_refs, acc_refs):
    n = u.shape[0]
    zs = [_dot_nt(q, k) + b for q, k, b in zip(qs, ks, biases)]
    sps = []
    for z, valid in zip(zs, valids):
        sp = _softplus(z)
        sps.append(sp if valid is None else jnp.where(valid, sp, 0.0))
    sums = [[_dot(sp[:, c:c + n].astype(BF16), u) for c in range(0, sp.shape[1], n)] for sp in sps]

    carried = {}
    ws = []
    for z, valid, blocks, r_ref in zip(zs, valids, sums, r_refs):
        r = carried[id(r_ref)][1] if id(r_ref) in carried else r_ref[...]
        parts = [None] * len(blocks)
        for blk in range(len(blocks) - 1, -1, -1):
            parts[blk] = blocks[blk] + r
            r = r + blocks[blk][:, 0:1]
        carried[id(r_ref)] = (r_ref, r)
        s_all = parts[0] if len(parts) == 1 else jnp.concatenate(parts, axis=-1)
        w = jnp.exp(z - s_all)
        ws.append(w if valid is None else jnp.where(valid, w, 0.0))
    for r_ref, r in carried.values():
        r_ref[...] = r

    totals = {}
    for w, v, acc_ref in zip(ws, vs, acc_refs):
        d = _dot(w.astype(BF16), v)
        totals[id(acc_ref)] = (acc_ref, d + totals[id(acc_ref)][1] if id(acc_ref) in totals else d)
    for acc_ref, d in totals.values():
        acc_ref[...] += d


def _head_rms(o, g):
    return o * lax.rsqrt(jnp.mean(o * o, axis=-1, keepdims=True) + EPS) * g


def _sb_prompt_kernel(bias_ref, q_ref, k_ref, v_ref, g_ref, o_ref, acc_ref, r_ref):
    hg = pl.program_id(0)
    i = pl.program_id(1)
    u = _suffix_matrix(TQ)
    acc_ref[...] = jnp.zeros_like(acc_ref)
    r_ref[...] = jnp.zeros_like(r_ref)

    heads = range(SB_HEADS_PER_STEP)
    lanes = [slice(g * HD, (g + 1) * HD) for g in heads]
    biases = [bias_ref[hg * SB_HEADS_PER_STEP + g] for g in heads]
    r_refs = [r_ref.at[g] for g in heads]
    acc_refs = [acc_ref.at[g] for g in heads]

    def tiles(j, valid):
        start = pl.multiple_of(j * TQ, TQ)
        _sb_tiles([q_ref[:, sl] for sl in lanes],
                  [k_ref[pl.ds(start, TQ), sl] for sl in lanes],
                  [v_ref[pl.ds(start, TQ), sl] for sl in lanes],
                  biases, [valid] * SB_HEADS_PER_STEP, u, r_refs, acc_refs)

    row = lax.broadcasted_iota(I32, (TQ, TQ), 0)
    col = lax.broadcasted_iota(I32, (TQ, TQ), 1)
    tiles(i, col < row)

    def body(it, carry):
        tiles(i - 1 - it, None)
        return carry

    lax.fori_loop(0, i, body, 0)
    for g in range(SB_HEADS_PER_STEP):
        o_ref[:, g * HD:(g + 1) * HD] = _head_rms(acc_ref[g], g_ref[...]).astype(BF16)


def _sb_prompt(bias, q_bf, k_bf, v_bf, g):
    gw = SB_HEADS_PER_STEP * HD
    return pl.pallas_call(
        _sb_prompt_kernel,
        grid=(HEADS // SB_HEADS_PER_STEP, SEQ // TQ),
        in_specs=[
            pl.BlockSpec(memory_space=pltpu.SMEM),
            pl.BlockSpec((TQ, gw), lambda h, i: (i, h)),
            pl.BlockSpec((SEQ, gw), lambda h, i: (0, h), pipeline_mode=pl.Buffered(1)),
            pl.BlockSpec((SEQ, gw), lambda h, i: (0, h), pipeline_mode=pl.Buffered(1)),
            pl.BlockSpec((1, HD), lambda h, i: (0, 0)),
        ],
        out_specs=pl.BlockSpec((TQ, gw), lambda h, i: (i, h)),
        out_shape=jax.ShapeDtypeStruct((SEQ, WIDTH), BF16),
        scratch_shapes=[pltpu.VMEM((SB_HEADS_PER_STEP, TQ, HD), F32),
                        pltpu.VMEM((SB_HEADS_PER_STEP, TQ, 1), F32)],
        compiler_params=_cparams(2, 48),
        name="sb_prompt",
    )(bias, q_bf, k_bf, v_bf, g)


SB_ROWS = HEADS * DEC_SEQ
PAGE_COLS = PAGE * HEADS
NEW_COLS = LANES
SUFFIX_BLOCK = 256


def _sb_sample_kernel(pt_ref, q_ref, brow_ref, knew_ref, vnew_ref, g_ref, *rest):
    k_refs = rest[:PAGES_PER_STEP]
    v_refs = rest[PAGES_PER_STEP:2 * PAGES_PER_STEP]
    o_ref, acc_ref, r_ref, kn_ref, vn_ref = rest[2 * PAGES_PER_STEP:]
    c = pl.program_id(1)
    q = q_ref[0]
    row = lax.broadcasted_iota(I32, (SB_ROWS, PAGE_COLS), 0)
    col = lax.broadcasted_iota(I32, (SB_ROWS, PAGE_COLS), 1)
    own = (col % HEADS) == (row // DEC_SEQ)
    bias = brow_ref[...]

    @pl.when(c == 0)
    def _():
        acc_ref[...] = jnp.zeros_like(acc_ref)
        r_ref[...] = jnp.zeros_like(r_ref)
        kn_ref[...] = jnp.zeros_like(kn_ref)
        vn_ref[...] = jnp.zeros_like(vn_ref)
        kn_ref[0:SB_ROWS, :] = knew_ref[0]
        vn_ref[0:SB_ROWS, :] = vnew_ref[0]
        rn = lax.broadcasted_iota(I32, (SB_ROWS, NEW_COLS), 0)
        cn = lax.broadcasted_iota(I32, (SB_ROWS, NEW_COLS), 1)
        valid = ((cn % HEADS) == (rn // DEC_SEQ)) & ((cn // HEADS) < (rn % DEC_SEQ))
        _sb_tiles([q], [kn_ref[...].astype(BF16)], [vn_ref[...].astype(BF16)], [bias[:, 0:NEW_COLS]], [valid],
                  _suffix_matrix(NEW_COLS), [r_ref], [acc_ref])

    order = range(PAGES_PER_STEP - 1, -1, -1)
    _sb_tiles([q] * PAGES_PER_STEP,
              [k_refs[p][0].reshape(PAGE_COLS, HD).astype(BF16) for p in order],
              [v_refs[p][0].reshape(PAGE_COLS, HD).astype(BF16) for p in order],
              [bias] * PAGES_PER_STEP, [own] * PAGES_PER_STEP, _suffix_matrix(SUFFIX_BLOCK),
              [r_ref] * PAGES_PER_STEP, [acc_ref] * PAGES_PER_STEP)

    @pl.when(c == pl.num_programs(1) - 1)
    def _():
        o_ref[0] = _head_rms(acc_ref[...], g_ref[...]).astype(BF16).astype(F32)


def _sb_sample(page_table, q_rows, brow, knew, vnew, g, kcache, vcache):
    n_steps = N_PAGES // PAGES_PER_STEP

    def page_spec(p):
        def imap(b, c, pt):
            return (pt[b * N_PAGES + (n_steps - 1 - c) * PAGES_PER_STEP + p], 0, 0, 0)
        return pl.BlockSpec((1, PAGE, HEADS, HD), imap)

    grid_spec = pltpu.PrefetchScalarGridSpec(
        num_scalar_prefetch=1,
        grid=(DEC_BATCH, n_steps),
        in_specs=[
            pl.BlockSpec((1, SB_ROWS, HD), lambda b, c, pt: (b, 0, 0)),
            pl.BlockSpec((SB_ROWS, PAGE_COLS), lambda b, c, pt: (0, 0)),
            pl.BlockSpec((1, SB_ROWS, HD), lambda b, c, pt: (b, 0, 0)),
            pl.BlockSpec((1, SB_ROWS, HD), lambda b, c, pt: (b, 0, 0)),
            pl.BlockSpec((1, HD), lambda b, c, pt: (0, 0)),
        ] + [page_spec(p) for p in range(PAGES_PER_STEP)] * 2,
        out_specs=pl.BlockSpec((1, SB_ROWS, HD), lambda b, c, pt: (b, 0, 0)),
        scratch_shapes=[
            pltpu.VMEM((SB_ROWS, HD), F32),
            pltpu.VMEM((SB_ROWS, 1), F32),
            pltpu.VMEM((NEW_COLS, HD), F32),
            pltpu.VMEM((NEW_COLS, HD), F32),
        ],
    )
    return pl.pallas_call(
        _sb_sample_kernel,
        grid_spec=grid_spec,
        out_shape=jax.ShapeDtypeStruct((DEC_BATCH, SB_ROWS, HD), F32),
        compiler_params=_cparams(2, 48),
        name="sb_sample",
    )(page_table, q_rows, brow, knew, vnew, g,
      *([kcache] * PAGES_PER_STEP), *([vcache] * PAGES_PER_STEP))


def _hgrn_block(hq, hf, hi, hgate, lb, g_out, state_in, state_out, *, chunk, last, sequential):
    n_chunks = HG_ROWS // chunk
    one_m_lb = 1.0 - lb
    e = jnp.exp(-jnp.abs(hf))
    inv = 1.0 / (1.0 + e)
    sig_pos = jnp.where(hf >= 0, inv, e * inv)
    sig_neg = jnp.where(hf >= 0, e * inv, inv)
    f = lb + one_m_lb * sig_pos
    k = one_m_lb * sig_neg
    g = jnp.log(f)
    q = hq * _sigmoid(hq)

    rin = lax.broadcasted_iota(I32, (HG_ROWS, WIDTH), 0) % chunk
    b = g
    step = 1
    while step < chunk:
        b = b + jnp.where(rin >= step, pltpu.roll(b, step, axis=0), 0.0)
        step *= 2
    cid = lax.broadcasted_iota(I32, (HG_ROWS, WIDTH), 0) // chunk
    b_last_rows = jnp.zeros_like(b)
    b_last = []
    for c in range(n_chunks):
        bl = b[c * chunk + last:c * chunk + last + 1, :]
        b_last.append(bl)
        b_last_rows = jnp.where(cid == c, bl, b_last_rows)

    q_dec = q * jnp.exp(b)
    k_inv = k * jnp.exp(-b)
    k_dec = k * jnp.exp(b_last_rows - b)

    r2 = lax.broadcasted_iota(I32, (HG_ROWS, HG_ROWS), 0)
    c2 = lax.broadcasted_iota(I32, (HG_ROWS, HG_ROWS), 1)
    intra = ((r2 // chunk) == (c2 // chunk)) & (c2 <= r2)
    eye = r2 == c2
    outs = []
    for h in range(HEADS):
        sl = slice(h * HD, (h + 1) * HD)
        qd_h = q_dec[:, sl]
        v_h = hi[:, sl].astype(BF16)
        a = jnp.where(intra, _dot_nt(qd_h.astype(BF16), k_inv[:, sl].astype(BF16)), 0.0)
        o_h = _dot(a.astype(BF16), v_h)
        kdec_t = jnp.transpose(k_dec[:, sl])
        s_cur = state_in(0, h) if sequential else None
        for c in range(n_chunks):
            if not sequential:
                s_cur = state_in(c, h)
            in_rows = (r2 // chunk) == c
            in_cols = (c2 // chunk) == c
            o_h = o_h + _dot(jnp.where(in_rows, qd_h, 0.0).astype(BF16), s_cur.astype(BF16))
            dl = jnp.exp(b_last[c][:, sl])
            dcol = jnp.sum(jnp.where(eye, dl, 0.0), axis=1, keepdims=True)
            s_new = dcol * s_cur + _dot(jnp.where(in_cols, kdec_t, 0.0).astype(BF16), v_h)
            if sequential:
                s_cur = s_new
            else:
                state_out(c, h, s_new)
        if sequential:
            state_out(0, h, s_cur)
        gate = hgate[:, sl]
        outs.append((_head_rms(o_h, g_out) * (gate * _sigmoid(gate))).astype(BF16))
    return jnp.concatenate(outs, axis=-1)


def _hgrn_prompt_kernel(hq_ref, hf_ref, hi_ref, hg_ref, lb_ref, g_ref, o_ref, sfin_ref, s_ref):
    i = pl.program_id(0)

    @pl.when(i == 0)
    def _():
        s_ref[...] = jnp.zeros_like(s_ref)

    def state_in(c, h):
        return s_ref[h]

    def state_out(c, h, s):
        s_ref[h] = s

    o_ref[...] = _hgrn_block(hq_ref[...], hf_ref[...], hi_ref[...], hg_ref[...], lb_ref[...], g_ref[...],
                             state_in, state_out, chunk=HG_CHUNK, last=HG_CHUNK - 1, sequential=True)

    @pl.when(i == pl.num_programs(0) - 1)
    def _():
        sfin_ref[...] = s_ref[...]


def _hgrn_prompt(hgin, lb, g):
    def col(j):
        return pl.BlockSpec((HG_ROWS, WIDTH), lambda i: (i, j))
    return pl.pallas_call(
        _hgrn_prompt_kernel,
        grid=(SEQ // HG_ROWS,),
        in_specs=[col(0), col(1), col(2), col(3),
                  pl.BlockSpec((1, WIDTH), lambda i: (0, 0)),
                  pl.BlockSpec((1, HD), lambda i: (0, 0))],
        out_specs=[pl.BlockSpec((HG_ROWS, WIDTH), lambda i: (i, 0)),
                   pl.BlockSpec((HEADS, HD, HD), lambda i: (0, 0, 0))],
        out_shape=[jax.ShapeDtypeStruct((SEQ, WIDTH), BF16),
                   jax.ShapeDtypeStruct((HEADS, HD, HD), F32)],
        scratch_shapes=[pltpu.VMEM((HEADS, HD, HD), F32)],
        compiler_params=_cparams(1, 32),
        name="hgrn_prompt",
    )(hgin, hgin, hgin, hgin, lb, g)


def _hgrn_sample_kernel(hq_ref, hf_ref, hi_ref, hg_ref, lb_ref, g_ref, s0_ref, o_ref, sout_ref):
    def state_in(c, h):
        return s0_ref[c, h]

    def state_out(c, h, s):
        sout_ref[c, h] = s

    o_ref[...] = _hgrn_block(hq_ref[...], hf_ref[...], hi_ref[...], hg_ref[...], lb_ref[...], g_ref[...],
                             state_in, state_out, chunk=SAMPLE_PAD, last=DEC_SEQ - 1, sequential=False)


def _hgrn_sample(hgin_pad, lb, g, s0):
    bpb = HG_ROWS // SAMPLE_PAD

    def col(j):
        return pl.BlockSpec((HG_ROWS, WIDTH), lambda i: (i, j))
    return pl.pallas_call(
        _hgrn_sample_kernel,
        grid=(DEC_BATCH // bpb,),
        in_specs=[col(0), col(1), col(2), col(3),
                  pl.BlockSpec((1, WIDTH), lambda i: (0, 0)),
                  pl.BlockSpec((1, HD), lambda i: (0, 0)),
                  pl.BlockSpec((bpb, HEADS, HD, HD), lambda i: (i, 0, 0, 0))],
        out_specs=[pl.BlockSpec((HG_ROWS, WIDTH), lambda i: (i, 0)),
                   pl.BlockSpec((bpb, HEADS, HD, HD), lambda i: (i, 0, 0, 0))],
        out_shape=[jax.ShapeDtypeStruct((DEC_BATCH * SAMPLE_PAD, WIDTH), BF16),
                   jax.ShapeDtypeStruct((DEC_BATCH, HEADS, HD, HD), F32)],
        compiler_params=_cparams(1, 48),
        name="hgrn_sample",
    )(hgin_pad, hgin_pad, hgin_pad, hgin_pad, lb, g, s0)


def _post_mixer_kernel(ap_ref, as_ref, bp_ref, bs_ref, xp_ref, xs_ref, wo_ref, gx_ref, wq_ref, h1_ref, qx_ref, *,
                       n_prompt_blocks):
    i = pl.program_id(0)

    def finish(a_ref, b_ref, x_ref):
        y = _dot(a_ref[...], wo_ref[0:WIDTH, :]) + _dot(b_ref[...], wo_ref[WIDTH:2 * WIDTH, :])
        h1 = x_ref[...] + y
        h1_ref[...] = h1
        qx_ref[...] = (_dot(_rms(h1, gx_ref[...]).astype(BF16), wq_ref[...]) * ATT_SCALE).astype(BF16)

    @pl.when(i < n_prompt_blocks)
    def _():
        finish(ap_ref, bp_ref, xp_ref)

    @pl.when(i >= n_prompt_blocks)
    def _():
        finish(as_ref, bs_ref, xs_ref)


def _post_mixer(a_p, a_s, b_p, b_s, xp, xs, wo_bf, gx, wq_bf):
    npb = SEQ // TM_TOK

    def prompt_rows(w):
        return pl.BlockSpec((TM_TOK, w), lambda i: (jnp.minimum(i, npb - 1), 0))

    def sample_rows(w):
        return pl.BlockSpec((TM_TOK, w), lambda i: (jnp.maximum(i - npb, 0), 0))

    return pl.pallas_call(
        functools.partial(_post_mixer_kernel, n_prompt_blocks=npb),
        grid=(N_TOK // TM_TOK,),
        in_specs=[
            prompt_rows(WIDTH), sample_rows(WIDTH), prompt_rows(WIDTH), sample_rows(WIDTH),
            prompt_rows(D_MODEL), sample_rows(D_MODEL),
            pl.BlockSpec((2 * WIDTH, D_MODEL), lambda i: (0, 0)),
            pl.BlockSpec((1, D_MODEL), lambda i: (0, 0)),
            pl.BlockSpec((D_MODEL, XA_WIDTH), lambda i: (0, 0)),
        ],
        out_specs=[pl.BlockSpec((TM_TOK, D_MODEL), lambda i: (i, 0)),
                   pl.BlockSpec((TM_TOK, XA_WIDTH), lambda i: (i, 0))],
        out_shape=[jax.ShapeDtypeStruct((N_TOK, D_MODEL), F32),
                   jax.ShapeDtypeStruct((N_TOK, XA_WIDTH), BF16)],
        compiler_params=_cparams(1, 48),
        name="post_mixer",
    )(a_p, a_s, b_p, b_s, xp, xs, wo_bf, gx, wq_bf)


def _mem_kv_kernel(m_ref, g_ref, w_ref, mk_ref, mv_ref):
    kv = _dot(_rms(m_ref[...], g_ref[...]).astype(BF16), w_ref[...])
    mk_ref[...] = kv[:, 0:XA_WIDTH]
    mv_ref[...] = kv[:, XA_WIDTH:2 * XA_WIDTH]


def _mem_kv(mem, g, w_bf):
    return pl.pallas_call(
        _mem_kv_kernel,
        out_shape=[jax.ShapeDtypeStruct((N_MEM, XA_WIDTH), F32)] * 2,
        compiler_params=pltpu.CompilerParams(vmem_limit_bytes=32 * MIB),
        name="mem_kv",
    )(mem, g, w_bf)


def _softmax_rows(s):
    m = jnp.max(s, axis=-1, keepdims=True)
    e = jnp.exp(s - m)
    return e / jnp.sum(e, axis=-1, keepdims=True)


def _xattn_prompt_kernel(q_ref, mk_ref, mv_ref, o_ref):
    mk = mk_ref[...].astype(BF16)
    mv = mv_ref[...].astype(BF16)
    outs = []
    for h in range(XA_HEADS):
        sl = slice(h * HD, (h + 1) * HD)
        p = _softmax_rows(_dot_nt(q_ref[:, sl], mk[:, sl]))
        outs.append(_dot(p.astype(BF16), mv[:, sl]))
    o_ref[...] = jnp.concatenate(outs, axis=-1).astype(BF16)


def _xattn_prompt(qx, mk, mv):
    return pl.pallas_call(
        _xattn_prompt_kernel,
        grid=(SEQ // TM_TOK,),
        in_specs=[pl.BlockSpec((TM_TOK, XA_WIDTH), lambda i: (i, 0)),
                  pl.BlockSpec((N_MEM, XA_WIDTH), lambda i: (0, 0)),
                  pl.BlockSpec((N_MEM, XA_WIDTH), lambda i: (0, 0))],
        out_specs=pl.BlockSpec((TM_TOK, XA_WIDTH), lambda i: (i, 0)),
        out_shape=jax.ShapeDtypeStruct((SEQ, XA_WIDTH), BF16),
        compiler_params=_cparams(1, 32),
        name="xattn_prompt",
    )(qx, mk, mv)


XA_BATCHES_PER_STEP = 8


def _xattn_sample_kernel(qrep_ref, mk_ref, mv_ref, o_ref):
    rows = XA_HEADS * SAMPLE_PAD
    row = lax.broadcasted_iota(I32, (rows, XA_WIDTH), 0)
    col = lax.broadcasted_iota(I32, (rows, XA_WIDTH), 1)
    diag = (row // SAMPLE_PAD) == (col // HD)
    for bb in range(XA_BATCHES_PER_STEP):
        qbd = jnp.where(diag, qrep_ref[bb], jnp.zeros((), BF16))
        mk = jnp.concatenate([mk_ref[bb, :, h, :] for h in range(XA_HEADS)], axis=-1).astype(BF16)
        mv = jnp.concatenate([mv_ref[bb, :, h, :] for h in range(XA_HEADS)], axis=-1).astype(BF16)
        p = _softmax_rows(_dot_nt(qbd, mk))
        o = jnp.where(diag, _dot(p.astype(BF16), mv), 0.0)
        out = o[0:SAMPLE_PAD]
        for h in range(1, XA_HEADS):
            out = out + o[h * SAMPLE_PAD:(h + 1) * SAMPLE_PAD]
        o_ref[bb] = out.astype(BF16).astype(F32)


def _xattn_sample(qrep, mk, mv):
    rows = XA_HEADS * SAMPLE_PAD
    nb = XA_BATCHES_PER_STEP
    return pl.pallas_call(
        _xattn_sample_kernel,
        grid=(DEC_BATCH // nb,),
        in_specs=[pl.BlockSpec((nb, rows, XA_WIDTH), lambda i: (i, 0, 0)),
                  pl.BlockSpec((nb, N_MEM, XA_HEADS, HD), lambda i: (i, 0, 0, 0)),
                  pl.BlockSpec((nb, N_MEM, XA_HEADS, HD), lambda i: (i, 0, 0, 0))],
        out_specs=pl.BlockSpec((nb, SAMPLE_PAD, XA_WIDTH), lambda i: (i, 0, 0)),
        out_shape=jax.ShapeDtypeStruct((DEC_BATCH, SAMPLE_PAD, XA_WIDTH), F32),
        compiler_params=_cparams(1, 40),
        name="xattn_sample",
    )(qrep, mk, mv)


ROUTE_LANES = LANES


def _router_kernel(ox_ref, h1_ref, wxo_ref, gf_ref, wr_hi_ref, wr_lo_ref, h2_ref, xn_ref, route_ref):
    h2 = h1_ref[...] + _dot(ox_ref[...], wxo_ref[...])
    h2_ref[...] = h2
    xn = _rms(h2, gf_ref[...])
    xn_ref[...] = xn
    x_hi = xn.astype(BF16)
    x_lo = (xn - x_hi.astype(F32)).astype(BF16)
    logits = _dot(x_hi, wr_hi_ref[...]) + (_dot(x_lo, wr_hi_ref[...]) + _dot(x_hi, wr_lo_ref[...]))

    lane = lax.broadcasted_iota(I32, logits.shape, 1).astype(F32)
    neg = jnp.float32(-jnp.inf)
    big = jnp.float32(ROUTE_LANES)
    lg = jnp.where(lane < N_GROUPS, logits, neg)
    gmax = jnp.max(lg, axis=-1, keepdims=True)
    g_idx = jnp.min(jnp.where(lg == gmax, lane, big), axis=-1, keepdims=True)
    g_w = 1.0 / jnp.sum(jnp.exp(lg - gmax), axis=-1, keepdims=True)
    lo_lane = N_GROUPS + g_idx * EXP_PER_GROUP
    le = jnp.where((lane >= lo_lane) & (lane < lo_lane + EXP_PER_GROUP), logits, neg)
    v1 = jnp.max(le, axis=-1, keepdims=True)
    i1 = jnp.min(jnp.where(le == v1, lane, big), axis=-1, keepdims=True)
    le2 = jnp.where(lane == i1, neg, le)
    v2 = jnp.max(le2, axis=-1, keepdims=True)
    i2 = jnp.min(jnp.where(le2 == v2, lane, big), axis=-1, keepdims=True)
    e2 = jnp.exp(v2 - v1)
    w1 = g_w / (1.0 + e2)
    w2 = g_w * e2 / (1.0 + e2)
    route = jnp.where(lane == 0, (i1 - N_GROUPS).astype(F32),
                      jnp.where(lane == 1, (i2 - N_GROUPS).astype(F32),
                                jnp.where(lane == 2, w1, jnp.where(lane == 3, w2, 0.0))))
    route_ref[...] = route


def _router(ox, h1, wxo_bf, gf, wr_hi, wr_lo):
    return pl.pallas_call(
        _router_kernel,
        grid=(N_TOK // TM_TOK,),
        in_specs=[pl.BlockSpec((TM_TOK, XA_WIDTH), lambda i: (i, 0)),
                  pl.BlockSpec((TM_TOK, D_MODEL), lambda i: (i, 0)),
                  pl.BlockSpec((XA_WIDTH, D_MODEL), lambda i: (0, 0)),
                  pl.BlockSpec((1, D_MODEL), lambda i: (0, 0)),
                  pl.BlockSpec((D_MODEL, ROUTE_LANES), lambda i: (0, 0)),
                  pl.BlockSpec((D_MODEL, ROUTE_LANES), lambda i: (0, 0))],
        out_specs=[pl.BlockSpec((TM_TOK, D_MODEL), lambda i: (i, 0)),
                   pl.BlockSpec((TM_TOK, D_MODEL), lambda i: (i, 0)),
                   pl.BlockSpec((TM_TOK, ROUTE_LANES), lambda i: (i, 0))],
        out_shape=[jax.ShapeDtypeStruct((N_TOK, D_MODEL), F32),
                   jax.ShapeDtypeStruct((N_TOK, D_MODEL), F32),
                   jax.ShapeDtypeStruct((N_TOK, ROUTE_LANES), F32)],
        compiler_params=_cparams(1, 40),
        name="router",
    )(ox, h1, wxo_bf, gf, wr_hi, wr_lo)


def _dispatch_kernel(slot_ref, cnt_ref, pad_ref, start_ref, nused_ref,
                     x_ref, xs_hbm, zrow, sems):
    i = pl.program_id(0)
    sem = sems.at[0]
    zsem = sems.at[1]

    @pl.when(i == 0)
    def _():
        zrow[...] = jnp.zeros_like(zrow)

        def pad_rows(fn):
            def per_expert(e, carry):
                base = start_ref[e]

                def body(r, c2):
                    fn(base + r)
                    return c2
                lax.fori_loop(cnt_ref[e], pad_ref[e], body, 0)
                return carry
            lax.fori_loop(0, N_EXPERTS, per_expert, 0)

        pad_rows(lambda row: pltpu.make_async_copy(zrow.at[pl.ds(0, 1)], xs_hbm.at[pl.ds(row, 1)], zsem).start())
        pad_rows(lambda row: pltpu.make_async_copy(zrow.at[pl.ds(0, 1)], xs_hbm.at[pl.ds(0, 1)], zsem).wait())

        def unused_tiles(fn):
            def body(t, carry):
                fn(pl.multiple_of(t * TS, TS))
                return carry
            lax.fori_loop(nused_ref[0], N_TILES, body, 0)

        unused_tiles(lambda r0: pltpu.make_async_copy(zrow, xs_hbm.at[pl.ds(r0, TS)], zsem).start())
        unused_tiles(lambda r0: pltpu.make_async_copy(zrow, xs_hbm.at[pl.ds(0, TS)], zsem).wait())

    base = i * (2 * TM_TOK)

    for r in range(2 * TM_TOK):
        pltpu.make_async_copy(x_ref.at[pl.ds(r // 2, 1)], xs_hbm.at[pl.ds(slot_ref[base + r], 1)],
                              sem).start(priority=r % 2)

    for _ in range(2):
        pltpu.make_async_copy(x_ref, xs_hbm.at[pl.ds(0, TM_TOK)], sem).wait()


def _dispatch(slot, counts, padded, seg_start, n_used, xn):
    grid_spec = pltpu.PrefetchScalarGridSpec(
        num_scalar_prefetch=5,
        grid=(N_TOK // TM_TOK,),
        in_specs=[pl.BlockSpec((TM_TOK, D_MODEL), lambda i, *_: (i, 0))],
        out_specs=pl.BlockSpec(memory_space=pl.ANY),
        scratch_shapes=[pltpu.VMEM((TS, D_MODEL), F32),
                        pltpu.SemaphoreType.DMA((2,))],
    )
    return pl.pallas_call(
        _dispatch_kernel,
        grid_spec=grid_spec,
        out_shape=jax.ShapeDtypeStruct((N_SLOTS, D_MODEL), F32),
        compiler_params=_cparams(1, 16),
        name="dispatch",
    )(slot, counts, padded, seg_start, n_used, xn)


def _moe_kernel(te_ref, nused_ref, first_ref, nxt_ref, par_ref,
                xs_ref, wg_hbm, wu_hbm, wd_hbm, ys_ref,
                wg_f, wu_f, wd_f, wg_bf, wu_bf, wd_bf, wsem):
    i = pl.program_id(0)
    active = i < nused_ref[0]

    def weight_copies(e, b):
        return (pltpu.make_async_copy(wg_hbm.at[e], wg_f.at[b], wsem.at[b]),
                pltpu.make_async_copy(wu_hbm.at[e], wu_f.at[b], wsem.at[b]),
                pltpu.make_async_copy(wd_hbm.at[e], wd_f.at[b], wsem.at[b]))

    @pl.when(active)
    def _():
        @pl.when(first_ref[i] == 1)
        def _():
            b = par_ref[i]

            @pl.when(i == 0)
            def _():
                for c in weight_copies(te_ref[0], b):
                    c.start()

            for c in weight_copies(te_ref[i], b):
                c.wait()

            @pl.when(nxt_ref[i] >= 0)
            def _():
                for c in weight_copies(nxt_ref[i], 1 - b):
                    c.start()

            wg_bf[...] = wg_f[b].astype(BF16)
            wu_bf[...] = wu_f[b].astype(BF16)
            wd_bf[...] = wd_f[b].astype(BF16)

        x = xs_ref[...].astype(BF16)
        a = _dot(x, wg_bf[...])
        u = _dot(x, wu_bf[...])
        hm = (a * _sigmoid(a)) * u
        ys_ref[...] = _dot(hm.astype(BF16), wd_bf[...])

    @pl.when(jnp.logical_not(active))
    def _():
        ys_ref[...] = jnp.zeros_like(ys_ref)


def _moe(tile_expert, n_used, first, nxt, parity, xs, wg, wu, wd):
    grid_spec = pltpu.PrefetchScalarGridSpec(
        num_scalar_prefetch=5,
        grid=(N_TILES,),
        in_specs=[
            pl.BlockSpec((TS, D_MODEL), lambda i, te, nu, *_: (jnp.minimum(i, nu[0] - 1), 0)),
            pl.BlockSpec(memory_space=pl.ANY),
            pl.BlockSpec(memory_space=pl.ANY),
            pl.BlockSpec(memory_space=pl.ANY),
        ],
        out_specs=pl.BlockSpec((TS, D_MODEL), lambda i, *_: (i, 0)),
        scratch_shapes=[
            pltpu.VMEM((2, D_MODEL, EXP_FF), F32),
            pltpu.VMEM((2, D_MODEL, EXP_FF), F32),
            pltpu.VMEM((2, EXP_FF, D_MODEL), F32),
            pltpu.VMEM((D_MODEL, EXP_FF), BF16),
            pltpu.VMEM((D_MODEL, EXP_FF), BF16),
            pltpu.VMEM((EXP_FF, D_MODEL), BF16),
            pltpu.SemaphoreType.DMA((2,)),
        ],
    )
    return pl.pallas_call(
        _moe_kernel,
        grid_spec=grid_spec,
        out_shape=jax.ShapeDtypeStruct((N_SLOTS, D_MODEL), F32),
        compiler_params=_cparams(1, 48),
        name="moe",
    )(tile_expert, n_used, first, nxt, parity, xs, wg, wu, wd)


def _final_kernel(slot_ref, h2_ref, route_ref, g_ref, ys_hbm, yp_ref, ysm_ref, ybuf, sem, *, n_prompt_blocks):
    i = pl.program_id(0)
    n = pl.num_programs(0)
    buf = i % 2

    def gather(blk, b):
        base = blk * (2 * TM_TOK)
        for r in range(2 * TM_TOK):
            pltpu.make_async_copy(ys_hbm.at[pl.ds(slot_ref[base + r], 1)], ybuf.at[b, r % 2, pl.ds(r // 2, 1)],
                                  sem.at[b]).start(priority=r % 2)

    @pl.when(i == 0)
    def _():
        gather(0, 0)

    for b in range(2):
        @pl.when((i + 1 < n) & ((i + 1) % 2 == b))
        def _():
            gather(i + 1, b)

    for k in range(2):
        pltpu.make_async_copy(ys_hbm.at[pl.ds(0, TM_TOK)], ybuf.at[buf, k], sem.at[buf]).wait()

    lane = lax.broadcasted_iota(I32, (TM_TOK, ROUTE_LANES), 1)
    route = route_ref[...]
    w0 = jnp.sum(jnp.where(lane == 2, route, 0.0), axis=-1, keepdims=True)
    w1 = jnp.sum(jnp.where(lane == 3, route, 0.0), axis=-1, keepdims=True)
    y = _rms(h2_ref[...] + (w0 * ybuf[buf, 0] + w1 * ybuf[buf, 1]), g_ref[...])

    @pl.when(i < n_prompt_blocks)
    def _():
        yp_ref[...] = y

    @pl.when(i >= n_prompt_blocks)
    def _():
        ysm_ref[...] = y


def _final(slot, h2, route, g, ys):
    npb = SEQ // TM_TOK
    grid_spec = pltpu.PrefetchScalarGridSpec(
        num_scalar_prefetch=1,
        grid=(N_TOK // TM_TOK,),
        in_specs=[pl.BlockSpec((TM_TOK, D_MODEL), lambda i, s: (i, 0)),
                  pl.BlockSpec((TM_TOK, ROUTE_LANES), lambda i, s: (i, 0)),
                  pl.BlockSpec((1, D_MODEL), lambda i, s: (0, 0)),
                  pl.BlockSpec(memory_space=pl.ANY)],
        out_specs=[pl.BlockSpec((TM_TOK, D_MODEL), lambda i, s: (jnp.minimum(i, npb - 1), 0)),
                   pl.BlockSpec((TM_TOK, D_MODEL), lambda i, s: (jnp.maximum(i - npb, 0), 0))],
        scratch_shapes=[pltpu.VMEM((2, 2, TM_TOK, D_MODEL), F32),
                        pltpu.SemaphoreType.DMA((2,))],
    )
    return pl.pallas_call(
        functools.partial(_final_kernel, n_prompt_blocks=npb),
        grid_spec=grid_spec,
        out_shape=[jax.ShapeDtypeStruct((SEQ, D_MODEL), F32),
                   jax.ShapeDtypeStruct((N_SAMPLE, D_MODEL), F32)],
        compiler_params=_cparams(1, 40),
        name="final",
    )(slot, h2, route, g, ys)


def _routing_tables(route):
    eid = route[:, 0:2].astype(I32).reshape(-1)
    onehot = (eid[:, None] == jnp.arange(N_EXPERTS, dtype=I32)[None, :]).astype(I32)
    csum = jnp.cumsum(onehot, axis=0)
    rank = jnp.sum(onehot * (csum - onehot), axis=1)
    counts = csum[-1]
    padded = (counts + TS - 1) // TS * TS
    seg_end = jnp.cumsum(padded)
    seg_start = seg_end - padded
    slot = jnp.sum(onehot * seg_start[None, :], axis=1) + rank
    n_used = (seg_end[-1] // TS).astype(I32)
    tile_start = jnp.arange(N_TILES, dtype=I32) * TS
    tile_expert = jnp.sum((seg_end[None, :] <= tile_start[:, None]).astype(I32), axis=1)
    used = jnp.arange(N_TILES) < n_used
    last_expert = jnp.max(jnp.where(used, tile_expert, 0))
    tile_expert = jnp.where(used, tile_expert, last_expert).astype(I32)
    prev = jnp.concatenate([jnp.full((1,), -1, I32), tile_expert[:-1]])
    first = (used & (tile_expert != prev)).astype(I32)
    parity = ((jnp.cumsum(first) - 1) % 2).astype(I32)
    e_ids = jnp.arange(N_EXPERTS, dtype=I32)
    later = (e_ids[None, :] > e_ids[:, None]) & (counts[None, :] > 0)
    next_nonempty = jnp.min(jnp.where(later, e_ids[None, :], N_EXPERTS), axis=1)
    next_nonempty = jnp.where(next_nonempty == N_EXPERTS, -1, next_nonempty)
    nxt = jnp.sum((tile_expert[:, None] == e_ids[None, :]).astype(I32) * next_nonempty[None, :], axis=1)
    return (slot.astype(I32), counts, padded, seg_start, tile_expert, n_used.reshape(1),
            first, nxt.astype(I32), parity)


def _rep_heads(x, n_heads):
    xpad = jnp.pad(x, ((0, 0), (0, SAMPLE_PAD - DEC_SEQ), (0, 0)))
    return jnp.tile(xpad, (1, n_heads, 1))


def kernel(x_prompt, x_sample, mem_prompt, cache_sb_k, cache_sb_v, state_hgrn, cache_mem_k, cache_mem_v,
           page_table, norm_mix, w_in, sb_logit_bias, lb_param, g_sb_out, g_hg_out, w_out, norm_xa, norm_mem,
           w_xq, w_xkv, w_xo, norm_ffn, w_route_group, w_route_expert, w_e_gate, w_e_up, w_e_down, norm_final):
    xp = x_prompt.reshape(SEQ, D_MODEL)
    xs = x_sample.reshape(N_SAMPLE, D_MODEL)

    lb = jnp.cumsum(jax.nn.softmax(lb_param.astype(F32), axis=0), axis=0)[0].reshape(1, WIDTH)
    w_in_bf = w_in[0].astype(BF16)
    w_out_bf = w_out[0].astype(BF16)
    w_xq_bf = w_xq[0].astype(BF16)
    w_xkv_bf = w_xkv[0].astype(BF16)
    w_xo_bf = w_xo[0].astype(BF16)
    w_r = jnp.concatenate([w_route_group[0],
                           jnp.transpose(w_route_expert[0], (1, 0, 2)).reshape(D_MODEL, N_EXPERTS)], axis=1)
    w_r = jnp.pad(w_r, ((0, 0), (0, ROUTE_LANES - w_r.shape[1])))
    w_r_hi = w_r.astype(BF16)
    w_r_lo = (w_r - w_r_hi.astype(F32)).astype(BF16)
    bias = sb_logit_bias[0].astype(F32)

    q_bf, k_bf, v_bf, kf_p, vf_p, kf_s, vf_s, hgin = _inproj(xp, xs, norm_mix[0].reshape(1, D_MODEL), w_in_bf)
    a_p = _sb_prompt(bias, q_bf, k_bf, v_bf, g_sb_out[0].reshape(1, HD))
    q_rows = jnp.transpose(q_bf[SEQ:].reshape(DEC_BATCH, DEC_SEQ, HEADS, HD), (0, 2, 1, 3))
    q_rows = q_rows.reshape(DEC_BATCH, SB_ROWS, HD)
    brow = jnp.broadcast_to(jnp.repeat(bias, DEC_SEQ)[:, None], (SB_ROWS, PAGE_COLS))
    a_s = _sb_sample(page_table.reshape(-1).astype(I32), q_rows, brow,
                     kf_s.reshape(DEC_BATCH, SB_ROWS, HD), vf_s.reshape(DEC_BATCH, SB_ROWS, HD),
                     g_sb_out[0].reshape(1, HD), cache_sb_k[0], cache_sb_v[0])
    a_s = jnp.transpose(a_s.reshape(DEC_BATCH, HEADS, DEC_SEQ, HD), (0, 2, 1, 3)).reshape(N_SAMPLE, WIDTH)
    g_hg = g_hg_out[0].reshape(1, HD)
    b_p, s_p = _hgrn_prompt(hgin, lb, g_hg)
    hgin_s = jnp.pad(hgin[SEQ:].reshape(DEC_BATCH, DEC_SEQ, 4 * WIDTH),
                     ((0, 0), (0, SAMPLE_PAD - DEC_SEQ), (0, 0))).reshape(DEC_BATCH * SAMPLE_PAD, 4 * WIDTH)
    b_s, s_s = _hgrn_sample(hgin_s, lb, g_hg, state_hgrn[0])
    b_s = b_s.reshape(DEC_BATCH, SAMPLE_PAD, WIDTH)[:, :DEC_SEQ].reshape(N_SAMPLE, WIDTH)
    h1, qx = _post_mixer(a_p, a_s.astype(BF16), b_p, b_s, xp, xs, w_out_bf,
                         norm_xa[0].reshape(1, D_MODEL), w_xq_bf)

    mk_p, mv_p = _mem_kv(mem_prompt[0], norm_mem[0].reshape(1, D_MODEL), w_xkv_bf)
    ox_p = _xattn_prompt(qx, mk_p, mv_p)
    qxrep = _rep_heads(qx[SEQ:].reshape(DEC_BATCH, DEC_SEQ, XA_WIDTH), XA_HEADS)
    ox_s = _xattn_sample(qxrep, cache_mem_k[0], cache_mem_v[0])
    ox = jnp.concatenate([ox_p, ox_s[:, :DEC_SEQ].reshape(N_SAMPLE, XA_WIDTH).astype(BF16)], axis=0)
    h2, xn2, route = _router(ox, h1, w_xo_bf, norm_ffn[0].reshape(1, D_MODEL), w_r_hi, w_r_lo)

    slot, counts, padded, seg_start, tile_expert, n_used, first, nxt, parity = _routing_tables(route)
    xs_sorted = _dispatch(slot, counts, padded, seg_start, n_used, xn2)
    ys_sorted = _moe(tile_expert, n_used, first, nxt, parity, xs_sorted, w_e_gate[0], w_e_up[0], w_e_down[0])
    y_p, y_s = _final(slot, h2, route, norm_final.reshape(1, D_MODEL), ys_sorted)

    sb_shape_p = (1, 1, SEQ, HEADS, HD)
    sb_shape_s = (1, DEC_BATCH, DEC_SEQ, HEADS, HD)
    return (y_p.reshape(1, SEQ, D_MODEL), y_s.reshape(DEC_BATCH, DEC_SEQ, D_MODEL),
            kf_p.reshape(sb_shape_p), vf_p.reshape(sb_shape_p), s_p.reshape(1, 1, HEADS, HD, HD),
            mk_p.reshape(1, 1, N_MEM, XA_HEADS, HD), mv_p.reshape(1, 1, N_MEM, XA_HEADS, HD),
            kf_s.reshape(sb_shape_s), vf_s.reshape(sb_shape_s), s_s.reshape(1, DEC_BATCH, HEADS, HD, HD))
```

```python
import functools
import math

import jax
import jax.numpy as jnp
from jax import lax
from jax.experimental import pallas as pl
from jax.experimental.pallas import tpu as pltpu

F32 = jnp.float32
BF16 = jnp.bfloat16
I32 = jnp.int32

D_MODEL = 2048
SEQ = 8192
DEC_BATCH = 128
DEC_SEQ = 4
N_SAMPLE = DEC_BATCH * DEC_SEQ
N_TOK = SEQ + N_SAMPLE
PAGE = 128
N_PAGES = 16
HEADS = 8
HD = 128
WIDTH = HEADS * HD
XA_HEADS = 4
XA_WIDTH = XA_HEADS * HD
N_MEM = 256
N_GROUPS = 4
EXP_PER_GROUP = 8
N_EXPERTS = 32
EXP_FF = 512
HG_CHUNK = 32
EPS = 1e-6
ATT_SCALE = 1.0 / math.sqrt(HD)

LANES = 128
SUBLANES = 8
MIB = 1024 * 1024

TM_IN = 512
TQ = 512
SB_HEADS_PER_STEP = 4
TM_TOK = 256
HG_ROWS = 128
TS = 256
SAMPLE_PAD = 8
PAGES_PER_STEP = 16
N_SLOTS = ((2 * N_TOK + N_EXPERTS * (TS - 1)) + TS - 1) // TS * TS
N_TILES = N_SLOTS // TS


def _cparams(n_axes, vmem_mib):
    return pltpu.CompilerParams(dimension_semantics=("arbitrary",) * n_axes,
                                vmem_limit_bytes=vmem_mib * MIB)


def _dot(a, b):
    return jnp.dot(a, b, preferred_element_type=F32)


def _dot_nt(a, b):
    return lax.dot_general(a, b, (((1,), (1,)), ((), ())), preferred_element_type=F32)


def _rms(x, g):
    ms = jnp.mean(x * x, axis=-1, keepdims=True)
    return x * lax.rsqrt(ms + EPS) * g


def _sigmoid(x):
    return 1.0 / (1.0 + jnp.exp(-x))


def _softplus(z):
    return jnp.maximum(z, 0.0) + jnp.log(1.0 + jnp.exp(-jnp.abs(z)))


def _suffix_matrix(n):
    r = lax.broadcasted_iota(I32, (n, n), 0)
    c = lax.broadcasted_iota(I32, (n, n), 1)
    return jnp.where(r >= c, 1.0, 0.0).astype(BF16)


def _split_bf16(x):
    hi = x.astype(BF16)
    lo = (x - hi.astype(F32)).astype(BF16)
    return hi, lo


def _inproj_kernel(xp_ref, xs_ref, g_ref, w_ref,
                   q_ref, kb_ref, vb_ref, kfp_ref, vfp_ref, kfs_ref, vfs_ref, hg_ref,
                   xn_ref, *, n_prompt_blocks):
    i = pl.program_id(0)
    j = pl.program_id(1)
    is_prompt = i < n_prompt_blocks

    @pl.when((j == 0) & is_prompt)
    def _():
        xn_ref[...] = _rms(xp_ref[...], g_ref[...]).astype(BF16)

    @pl.when((j == 0) & jnp.logical_not(is_prompt))
    def _():
        xn_ref[...] = _rms(xs_ref[...], g_ref[...]).astype(BF16)

    def proj():
        return _dot(xn_ref[...], w_ref[...])

    @pl.when(j == 0)
    def _():
        q_ref[...] = (proj() * ATT_SCALE).astype(BF16)

    for col, bf_ref, fp_ref, fs_ref in ((1, kb_ref, kfp_ref, kfs_ref), (2, vb_ref, vfp_ref, vfs_ref)):
        @pl.when((j == col) & is_prompt)
        def _():
            fp_ref[...] = proj()
            bf_ref[...] = fp_ref[...].astype(BF16)

        @pl.when((j == col) & jnp.logical_not(is_prompt))
        def _():
            fs_ref[...] = proj()
            bf_ref[...] = fs_ref[...].astype(BF16)

    @pl.when(j >= 3)
    def _():
        hg_ref[...] = proj()


def _inproj(xp, xs, g, w_bf):
    npb = SEQ // TM_IN
    nblk = N_TOK // TM_IN
    ncol = w_bf.shape[1] // WIDTH
    tok_spec = pl.BlockSpec((TM_IN, WIDTH), lambda i, j: (i, 0))
    return pl.pallas_call(
        functools.partial(_inproj_kernel, n_prompt_blocks=npb),
        grid=(nblk, ncol),
        in_specs=[
            pl.BlockSpec((TM_IN, D_MODEL), lambda i, j: (jnp.minimum(i, npb - 1), 0)),
            pl.BlockSpec((TM_IN, D_MODEL), lambda i, j: (jnp.maximum(i - npb, 0), 0)),
            pl.BlockSpec((1, D_MODEL), lambda i, j: (0, 0)),
            pl.BlockSpec((D_MODEL, WIDTH), lambda i, j: (0, j)),
        ],
        out_specs=[
            tok_spec, tok_spec, tok_spec,
            pl.BlockSpec((TM_IN, WIDTH), lambda i, j: (jnp.minimum(i, npb - 1), 0)),
            pl.BlockSpec((TM_IN, WIDTH), lambda i, j: (jnp.minimum(i, npb - 1), 0)),
            pl.BlockSpec((TM_IN, WIDTH), lambda i, j: (jnp.maximum(i - npb, 0), 0)),
            pl.BlockSpec((TM_IN, WIDTH), lambda i, j: (jnp.maximum(i - npb, 0), 0)),
            pl.BlockSpec((TM_IN, WIDTH), lambda i, j: (i, jnp.clip(j - 3, 0, 3))),
        ],
        out_shape=[
            jax.ShapeDtypeStruct((N_TOK, WIDTH), BF16),
            jax.ShapeDtypeStruct((N_TOK, WIDTH), BF16),
            jax.ShapeDtypeStruct((N_TOK, WIDTH), BF16),
            jax.ShapeDtypeStruct((SEQ, WIDTH), F32),
            jax.ShapeDtypeStruct((SEQ, WIDTH), F32),
            jax.ShapeDtypeStruct((N_SAMPLE, WIDTH), F32),
            jax.ShapeDtypeStruct((N_SAMPLE, WIDTH), F32),
            jax.ShapeDtypeStruct((N_TOK, 4 * WIDTH), F32),
        ],
        scratch_shapes=[pltpu.VMEM((TM_IN, D_MODEL), BF16)],
        compiler_params=_cparams(2, 48),
        name="inproj",
    )(xp, xs, g, w_bf)


def _sb_tiles(qs, ks, vs, biases, valids, u, r_refs, acc_refs):
    n = u.shape[0]
    zs = [_dot_nt(q, k) + b for q, k, b in zip(qs, ks, biases)]
    sps = []
    for z, valid in zip(zs, valids):
        sp = _softplus(z)
        sps.append(sp if valid is None else jnp.where(valid, sp, 0.0))
    sums = [[_dot(sp[:, c:c + n].astype(BF16), u) for c in range(0, sp.shape[1], n)] for sp in sps]

    carried = {}
    ws = []
    for z, valid, blocks, r_ref in zip(zs, valids, sums, r_refs):
        r = carried[id(r_ref)][1] if id(r_ref) in carried else r_ref[...]
        parts = [None] * len(blocks)
        for blk in range(len(blocks) - 1, -1, -1):
            parts[blk] = blocks[blk] + r
            r = r + blocks[blk][:, 0:1]
        carried[id(r_ref)] = (r_ref, r)
        s_all = parts[0] if len(parts) == 1 else jnp.concatenate(parts, axis=-1)
        w = jnp.exp(z - s_all)
        ws.append(w if valid is None else jnp.where(valid, w, 0.0))
    for r_ref, r in carried.values():
        r_ref[...] = r

    totals = {}
    for w, v, acc_ref in zip(ws, vs, acc_refs):
        d = _dot(w.astype(BF16), v)
        totals[id(acc_ref)] = (acc_ref, d + totals[id(acc_ref)][1] if id(acc_ref) in totals else d)
    for acc_ref, d in totals.values():
        acc_ref[...] += d


def _head_rms(o, g):
    return o * lax.rsqrt(jnp.mean(o * o, axis=-1, keepdims=True) + EPS) * g


def _sb_prompt_kernel(bias_ref, q_ref, k_ref, v_ref, g_ref, o_ref, acc_ref, r_ref):
    hg = pl.program_id(0)
    i = pl.program_id(1)
    u = _suffix_matrix(256)
    acc_ref[...] = jnp.zeros_like(acc_ref)
    r_ref[...] = jnp.zeros_like(r_ref)

    heads = range(SB_HEADS_PER_STEP)
    lanes = [slice(g * HD, (g + 1) * HD) for g in heads]
    biases = [bias_ref[hg * SB_HEADS_PER_STEP + g] for g in heads]
    r_refs = [r_ref.at[g] for g in heads]
    acc_refs = [acc_ref.at[g] for g in heads]

    def tiles(j, valid):
        start = pl.multiple_of(j * TQ, TQ)
        _sb_tiles([q_ref[:, sl] for sl in lanes],
                  [k_ref[pl.ds(start, TQ), sl] for sl in lanes],
                  [v_ref[pl.ds(start, TQ), sl] for sl in lanes],
                  biases, [valid] * SB_HEADS_PER_STEP, u, r_refs, acc_refs)

    row = lax.broadcasted_iota(I32, (TQ, TQ), 0)
    col = lax.broadcasted_iota(I32, (TQ, TQ), 1)
    tiles(i, col < row)

    def body(it, carry):
        tiles(i - 1 - it, None)
        return carry

    lax.fori_loop(0, i, body, 0)
    for g in range(SB_HEADS_PER_STEP):
        o_ref[:, g * HD:(g + 1) * HD] = _head_rms(acc_ref[g], g_ref[...]).astype(BF16)


def _sb_prompt(bias, q_bf, k_bf, v_bf, g):
    gw = SB_HEADS_PER_STEP * HD
    return pl.pallas_call(
        _sb_prompt_kernel,
        grid=(HEADS // SB_HEADS_PER_STEP, SEQ // TQ),
        in_specs=[
            pl.BlockSpec(memory_space=pltpu.SMEM),
            pl.BlockSpec((TQ, gw), lambda h, i: (i, h)),
            pl.BlockSpec((SEQ, gw), lambda h, i: (0, h), pipeline_mode=pl.Buffered(1)),
            pl.BlockSpec((SEQ, gw), lambda h, i: (0, h), pipeline_mode=pl.Buffered(1)),
            pl.BlockSpec((1, HD), lambda h, i: (0, 0)),
        ],
        out_specs=pl.BlockSpec((TQ, gw), lambda h, i: (i, h)),
        out_shape=jax.ShapeDtypeStruct((SEQ, WIDTH), BF16),
        scratch_shapes=[pltpu.VMEM((SB_HEADS_PER_STEP, TQ, HD), F32),
                        pltpu.VMEM((SB_HEADS_PER_STEP, TQ, 1), F32)],
        compiler_params=_cparams(2, 48),
        name="sb_prompt",
    )(bias, q_bf, k_bf, v_bf, g)


SB_ROWS = HEADS * DEC_SEQ
PAGE_COLS = PAGE * HEADS
NEW_COLS = LANES
SUFFIX_BLOCK = 256


def _sb_sample_kernel(pt_ref, q_ref, brow_ref, knew_ref, vnew_ref, g_ref, *rest):
    k_refs = rest[:PAGES_PER_STEP]
    v_refs = rest[PAGES_PER_STEP:2 * PAGES_PER_STEP]
    o_ref, acc_ref, r_ref, kn_ref, vn_ref = rest[2 * PAGES_PER_STEP:]
    c = pl.program_id(1)
    q = q_ref[0]
    row = lax.broadcasted_iota(I32, (SB_ROWS, PAGE_COLS), 0)
    col = lax.broadcasted_iota(I32, (SB_ROWS, PAGE_COLS), 1)
    own = (col % HEADS) == (row // DEC_SEQ)
    bias = brow_ref[...]

    @pl.when(c == 0)
    def _():
        acc_ref[...] = jnp.zeros_like(acc_ref)
        r_ref[...] = jnp.zeros_like(r_ref)
        kn_ref[...] = jnp.zeros_like(kn_ref)
        vn_ref[...] = jnp.zeros_like(vn_ref)
        kn_ref[0:SB_ROWS, :] = knew_ref[0]
        vn_ref[0:SB_ROWS, :] = vnew_ref[0]
        rn = lax.broadcasted_iota(I32, (SB_ROWS, NEW_COLS), 0)
        cn = lax.broadcasted_iota(I32, (SB_ROWS, NEW_COLS), 1)
        valid = ((cn % HEADS) == (rn // DEC_SEQ)) & ((cn // HEADS) < (rn % DEC_SEQ))
        _sb_tiles([q], [kn_ref[...].astype(BF16)], [vn_ref[...].astype(BF16)], [bias[:, 0:NEW_COLS]], [valid],
                  _suffix_matrix(NEW_COLS), [r_ref], [acc_ref])

    order = range(PAGES_PER_STEP - 1, -1, -1)
    _sb_tiles([q] * PAGES_PER_STEP,
              [k_refs[p][0].reshape(PAGE_COLS, HD).astype(BF16) for p in order],
              [v_refs[p][0].reshape(PAGE_COLS, HD).astype(BF16) for p in order],
              [bias] * PAGES_PER_STEP, [own] * PAGES_PER_STEP, _suffix_matrix(SUFFIX_BLOCK),
              [r_ref] * PAGES_PER_STEP, [acc_ref] * PAGES_PER_STEP)

    @pl.when(c == pl.num_programs(1) - 1)
    def _():
        o_ref[0] = _head_rms(acc_ref[...], g_ref[...]).astype(BF16).astype(F32)


def _sb_sample(page_table, q_rows, brow, knew, vnew, g, kcache, vcache):
    n_steps = N_PAGES // PAGES_PER_STEP

    def page_spec(p):
        def imap(b, c, pt):
            return (pt[b * N_PAGES + (n_steps - 1 - c) * PAGES_PER_STEP + p], 0, 0, 0)
        return pl.BlockSpec((1, PAGE, HEADS, HD), imap)

    grid_spec = pltpu.PrefetchScalarGridSpec(
        num_scalar_prefetch=1,
        grid=(DEC_BATCH, n_steps),
        in_specs=[
            pl.BlockSpec((1, SB_ROWS, HD), lambda b, c, pt: (b, 0, 0)),
            pl.BlockSpec((SB_ROWS, PAGE_COLS), lambda b, c, pt: (0, 0)),
            pl.BlockSpec((1, SB_ROWS, HD), lambda b, c, pt: (b, 0, 0)),
            pl.BlockSpec((1, SB_ROWS, HD), lambda b, c, pt: (b, 0, 0)),
            pl.BlockSpec((1, HD), lambda b, c, pt: (0, 0)),
        ] + [page_spec(p) for p in range(PAGES_PER_STEP)] * 2,
        out_specs=pl.BlockSpec((1, SB_ROWS, HD), lambda b, c, pt: (b, 0, 0)),
        scratch_shapes=[
            pltpu.VMEM((SB_ROWS, HD), F32),
            pltpu.VMEM((SB_ROWS, 1), F32),
            pltpu.VMEM((NEW_COLS, HD), F32),
            pltpu.VMEM((NEW_COLS, HD), F32),
        ],
    )
    return pl.pallas_call(
        _sb_sample_kernel,
        grid_spec=grid_spec,
        out_shape=jax.ShapeDtypeStruct((DEC_BATCH, SB_ROWS, HD), F32),
        compiler_params=_cparams(2, 48),
        name="sb_sample",
    )(page_table, q_rows, brow, knew, vnew, g,
      *([kcache] * PAGES_PER_STEP), *([vcache] * PAGES_PER_STEP))


def _hgrn_block(hq, hf, hi, hgate, lb, g_out, state_in, state_out, *, chunk, last, sequential):
    n_chunks = HG_ROWS // chunk
    one_m_lb = 1.0 - lb
    e = jnp.exp(-jnp.abs(hf))
    inv = 1.0 / (1.0 + e)
    sig_pos = jnp.where(hf >= 0, inv, e * inv)
    sig_neg = jnp.where(hf >= 0, e * inv, inv)
    f = lb + one_m_lb * sig_pos
    k = one_m_lb * sig_neg
    g = jnp.log(f)
    q = hq * _sigmoid(hq)

    rin = lax.broadcasted_iota(I32, (HG_ROWS, WIDTH), 0) % chunk
    b = g
    step = 1
    while step < chunk:
        b = b + jnp.where(rin >= step, pltpu.roll(b, step, axis=0), 0.0)
        step *= 2
    cid = lax.broadcasted_iota(I32, (HG_ROWS, WIDTH), 0) // chunk
    b_last_rows = jnp.zeros_like(b)
    b_last = []
    for c in range(n_chunks):
        bl = b[c * chunk + last:c * chunk + last + 1, :]
        b_last.append(bl)
        b_last_rows = jnp.where(cid == c, bl, b_last_rows)

    q_dec = q * jnp.exp(b)
    k_inv = k * jnp.exp(-b)
    k_dec = k * jnp.exp(b_last_rows - b)

    r2 = lax.broadcasted_iota(I32, (HG_ROWS, HG_ROWS), 0)
    c2 = lax.broadcasted_iota(I32, (HG_ROWS, HG_ROWS), 1)
    intra = ((r2 // chunk) == (c2 // chunk)) & (c2 <= r2)
    eye = r2 == c2
    outs = []
    for h in range(HEADS):
        sl = slice(h * HD, (h + 1) * HD)
        qd_h = q_dec[:, sl]
        v_h = hi[:, sl].astype(BF16)
        a = jnp.where(intra, _dot_nt(qd_h.astype(BF16), k_inv[:, sl].astype(BF16)), 0.0)
        o_h = _dot(a.astype(BF16), v_h)
        kdec_t = jnp.transpose(k_dec[:, sl])
        s_cur = state_in(0, h) if sequential else None
        for c in range(n_chunks):
            if not sequential:
                s_cur = state_in(c, h)
            in_rows = (r2 // chunk) == c
            in_cols = (c2 // chunk) == c
            o_h = o_h + _dot(jnp.where(in_rows, qd_h, 0.0).astype(BF16), s_cur.astype(BF16))
            dl = jnp.exp(b_last[c][:, sl])
            dcol = jnp.sum(jnp.where(eye, dl, 0.0), axis=1, keepdims=True)
            s_new = dcol * s_cur + _dot(jnp.where(in_cols, kdec_t, 0.0).astype(BF16), v_h)
            if sequential:
                s_cur = s_new
            else:
                state_out(c, h, s_new)
        if sequential:
            state_out(0, h, s_cur)
        gate = hgate[:, sl]
        outs.append((_head_rms(o_h, g_out) * (gate * _sigmoid(gate))).astype(BF16))
    return jnp.concatenate(outs, axis=-1)


def _hgrn_prompt_kernel(hq_ref, hf_ref, hi_ref, hg_ref, lb_ref, g_ref, o_ref, sfin_ref, s_ref):
    i = pl.program_id(0)

    @pl.when(i == 0)
    def _():
        s_ref[...] = jnp.zeros_like(s_ref)

    def state_in(c, h):
        return s_ref[h]

    def state_out(c, h, s):
        s_ref[h] = s

    o_ref[...] = _hgrn_block(hq_ref[...], hf_ref[...], hi_ref[...], hg_ref[...], lb_ref[...], g_ref[...],
                             state_in, state_out, chunk=HG_CHUNK, last=HG_CHUNK - 1, sequential=True)

    @pl.when(i == pl.num_programs(0) - 1)
    def _():
        sfin_ref[...] = s_ref[...]


def _hgrn_prompt(hgin, lb, g):
    def col(j):
        return pl.BlockSpec((HG_ROWS, WIDTH), lambda i: (i, j))
    return pl.pallas_call(
        _hgrn_prompt_kernel,
        grid=(SEQ // HG_ROWS,),
        in_specs=[col(0), col(1), col(2), col(3),
                  pl.BlockSpec((1, WIDTH), lambda i: (0, 0)),
                  pl.BlockSpec((1, HD), lambda i: (0, 0))],
        out_specs=[pl.BlockSpec((HG_ROWS, WIDTH), lambda i: (i, 0)),
                   pl.BlockSpec((HEADS, HD, HD), lambda i: (0, 0, 0))],
        out_shape=[jax.ShapeDtypeStruct((SEQ, WIDTH), BF16),
                   jax.ShapeDtypeStruct((HEADS, HD, HD), F32)],
        scratch_shapes=[pltpu.VMEM((HEADS, HD, HD), F32)],
        compiler_params=_cparams(1, 32),
        name="hgrn_prompt",
    )(hgin, hgin, hgin, hgin, lb, g)


def _hgrn_sample_kernel(hq_ref, hf_ref, hi_ref, hg_ref, lb_ref, g_ref, s0_ref, o_ref, sout_ref):
    def state_in(c, h):
        return s0_ref[c, h]

    def state_out(c, h, s):
        sout_ref[c, h] = s

    o_ref[...] = _hgrn_block(hq_ref[...], hf_ref[...], hi_ref[...], hg_ref[...], lb_ref[...], g_ref[...],
                             state_in, state_out, chunk=SAMPLE_PAD, last=DEC_SEQ - 1, sequential=False)


def _hgrn_sample(hgin_pad, lb, g, s0):
    bpb = HG_ROWS // SAMPLE_PAD

    def col(j):
        return pl.BlockSpec((HG_ROWS, WIDTH), lambda i: (i, j))
    return pl.pallas_call(
        _hgrn_sample_kernel,
        grid=(DEC_BATCH // bpb,),
        in_specs=[col(0), col(1), col(2), col(3),
                  pl.BlockSpec((1, WIDTH), lambda i: (0, 0)),
                  pl.BlockSpec((1, HD), lambda i: (0, 0)),
                  pl.BlockSpec((bpb, HEADS, HD, HD), lambda i: (i, 0, 0, 0))],
        out_specs=[pl.BlockSpec((HG_ROWS, WIDTH), lambda i: (i, 0)),
                   pl.BlockSpec((bpb, HEADS, HD, HD), lambda i: (i, 0, 0, 0))],
        out_shape=[jax.ShapeDtypeStruct((DEC_BATCH * SAMPLE_PAD, WIDTH), BF16),
                   jax.ShapeDtypeStruct((DEC_BATCH, HEADS, HD, HD), F32)],
        compiler_params=_cparams(1, 48),
        name="hgrn_sample",
    )(hgin_pad, hgin_pad, hgin_pad, hgin_pad, lb, g, s0)


def _post_mixer_kernel(ap_ref, as_ref, bp_ref, bs_ref, xp_ref, xs_ref, wo_ref, gx_ref, wq_ref, h1_ref, qx_ref, *,
                       n_prompt_blocks):
    i = pl.program_id(0)

    def finish(a_ref, b_ref, x_ref):
        y = _dot(a_ref[...], wo_ref[0:WIDTH, :]) + _dot(b_ref[...], wo_ref[WIDTH:2 * WIDTH, :])
        h1 = x_ref[...] + y
        h1_ref[...] = h1
        qx_ref[...] = (_dot(_rms(h1, gx_ref[...]).astype(BF16), wq_ref[...]) * ATT_SCALE).astype(BF16)

    @pl.when(i < n_prompt_blocks)
    def _():
        finish(ap_ref, bp_ref, xp_ref)

    @pl.when(i >= n_prompt_blocks)
    def _():
        finish(as_ref, bs_ref, xs_ref)


def _post_mixer(a_p, a_s, b_p, b_s, xp, xs, wo_bf, gx, wq_bf):
    npb = SEQ // TM_TOK

    def prompt_rows(w):
        return pl.BlockSpec((TM_TOK, w), lambda i: (jnp.minimum(i, npb - 1), 0))

    def sample_rows(w):
        return pl.BlockSpec((TM_TOK, w), lambda i: (jnp.maximum(i - npb, 0), 0))

    return pl.pallas_call(
        functools.partial(_post_mixer_kernel, n_prompt_blocks=npb),
        grid=(N_TOK // TM_TOK,),
        in_specs=[
            prompt_rows(WIDTH), sample_rows(WIDTH), prompt_rows(WIDTH), sample_rows(WIDTH),
            prompt_rows(D_MODEL), sample_rows(D_MODEL),
            pl.BlockSpec((2 * WIDTH, D_MODEL), lambda i: (0, 0)),
            pl.BlockSpec((1, D_MODEL), lambda i: (0, 0)),
            pl.BlockSpec((D_MODEL, XA_WIDTH), lambda i: (0, 0)),
        ],
        out_specs=[pl.BlockSpec((TM_TOK, D_MODEL), lambda i: (i, 0)),
                   pl.BlockSpec((TM_TOK, XA_WIDTH), lambda i: (i, 0))],
        out_shape=[jax.ShapeDtypeStruct((N_TOK, D_MODEL), F32),
                   jax.ShapeDtypeStruct((N_TOK, XA_WIDTH), BF16)],
        compiler_params=_cparams(1, 48),
        name="post_mixer",
    )(a_p, a_s, b_p, b_s, xp, xs, wo_bf, gx, wq_bf)


def _mem_kv_kernel(m_ref, g_ref, w_ref, mk_ref, mv_ref):
    kv = _dot(_rms(m_ref[...], g_ref[...]).astype(BF16), w_ref[...])
    mk_ref[...] = kv[:, 0:XA_WIDTH]
    mv_ref[...] = kv[:, XA_WIDTH:2 * XA_WIDTH]


def _mem_kv(mem, g, w_bf):
    return pl.pallas_call(
        _mem_kv_kernel,
        out_shape=[jax.ShapeDtypeStruct((N_MEM, XA_WIDTH), F32)] * 2,
        compiler_params=pltpu.CompilerParams(vmem_limit_bytes=32 * MIB),
        name="mem_kv",
    )(mem, g, w_bf)


def _softmax_rows(s):
    m = jnp.max(s, axis=-1, keepdims=True)
    e = jnp.exp(s - m)
    return e / jnp.sum(e, axis=-1, keepdims=True)


def _xattn_prompt_kernel(q_ref, mk_ref, mv_ref, o_ref):
    mk = mk_ref[...].astype(BF16)
    mv = mv_ref[...].astype(BF16)
    outs = []
    for h in range(XA_HEADS):
        sl = slice(h * HD, (h + 1) * HD)
        p = _softmax_rows(_dot_nt(q_ref[:, sl], mk[:, sl]))
        outs.append(_dot(p.astype(BF16), mv[:, sl]))
    o_ref[...] = jnp.concatenate(outs, axis=-1).astype(BF16)


def _xattn_prompt(qx, mk, mv):
    return pl.pallas_call(
        _xattn_prompt_kernel,
        grid=(SEQ // TM_TOK,),
        in_specs=[pl.BlockSpec((TM_TOK, XA_WIDTH), lambda i: (i, 0)),
                  pl.BlockSpec((N_MEM, XA_WIDTH), lambda i: (0, 0)),
                  pl.BlockSpec((N_MEM, XA_WIDTH), lambda i: (0, 0))],
        out_specs=pl.BlockSpec((TM_TOK, XA_WIDTH), lambda i: (i, 0)),
        out_shape=jax.ShapeDtypeStruct((SEQ, XA_WIDTH), BF16),
        compiler_params=_cparams(1, 32),
        name="xattn_prompt",
    )(qx, mk, mv)


XA_BATCHES_PER_STEP = 8


def _xattn_sample_kernel(qrep_ref, mk_ref, mv_ref, o_ref):
    rows = XA_HEADS * SAMPLE_PAD
    row = lax.broadcasted_iota(I32, (rows, XA_WIDTH), 0)
    col = lax.broadcasted_iota(I32, (rows, XA_WIDTH), 1)
    diag = (row // SAMPLE_PAD) == (col // HD)
    for bb in range(XA_BATCHES_PER_STEP):
        qbd = jnp.where(diag, qrep_ref[bb], jnp.zeros((), BF16))
        mk = jnp.concatenate([mk_ref[bb, :, h, :] for h in range(XA_HEADS)], axis=-1).astype(BF16)
        mv = jnp.concatenate([mv_ref[bb, :, h, :] for h in range(XA_HEADS)], axis=-1).astype(BF16)
        p = _softmax_rows(_dot_nt(qbd, mk))
        o = jnp.where(diag, _dot(p.astype(BF16), mv), 0.0)
        out = o[0:SAMPLE_PAD]
        for h in range(1, XA_HEADS):
            out = out + o[h * SAMPLE_PAD:(h + 1) * SAMPLE_PAD]
        o_ref[bb] = out.astype(BF16).astype(F32)


def _xattn_sample(qrep, mk, mv):
    rows = XA_HEADS * SAMPLE_PAD
    nb = XA_BATCHES_PER_STEP
    return pl.pallas_call(
        _xattn_sample_kernel,
        grid=(DEC_BATCH // nb,),
        in_specs=[pl.BlockSpec((nb, rows, XA_WIDTH), lambda i: (i, 0, 0)),
                  pl.BlockSpec((nb, N_MEM, XA_HEADS, HD), lambda i: (i, 0, 0, 0)),
                  pl.BlockSpec((nb, N_MEM, XA_HEADS, HD), lambda i: (i, 0, 0, 0))],
        out_specs=pl.BlockSpec((nb, SAMPLE_PAD, XA_WIDTH), lambda i: (i, 0, 0)),
        out_shape=jax.ShapeDtypeStruct((DEC_BATCH, SAMPLE_PAD, XA_WIDTH), F32),
        compiler_params=_cparams(1, 40),
        name="xattn_sample",
    )(qrep, mk, mv)


ROUTE_LANES = LANES


def _router_kernel(ox_ref, h1_ref, wxo_ref, gf_ref, wr_hi_ref, wr_lo_ref, h2_ref, xn_ref, route_ref):
    h2 = h1_ref[...] + _dot(ox_ref[...], wxo_ref[...])
    h2_ref[...] = h2
    xn = _rms(h2, gf_ref[...])
    xn_ref[...] = xn
    x_hi = xn.astype(BF16)
    x_lo = (xn - x_hi.astype(F32)).astype(BF16)
    logits = _dot(x_hi, wr_hi_ref[...]) + (_dot(x_lo, wr_hi_ref[...]) + _dot(x_hi, wr_lo_ref[...]))

    lane = lax.broadcasted_iota(I32, logits.shape, 1).astype(F32)
    neg = jnp.float32(-jnp.inf)
    big = jnp.float32(ROUTE_LANES)
    lg = jnp.where(lane < N_GROUPS, logits, neg)
    gmax = jnp.max(lg, axis=-1, keepdims=True)
    g_idx = jnp.min(jnp.where(lg == gmax, lane, big), axis=-1, keepdims=True)
    g_w = 1.0 / jnp.sum(jnp.exp(lg - gmax), axis=-1, keepdims=True)
    lo_lane = N_GROUPS + g_idx * EXP_PER_GROUP
    le = jnp.where((lane >= lo_lane) & (lane < lo_lane + EXP_PER_GROUP), logits, neg)
    v1 = jnp.max(le, axis=-1, keepdims=True)
    i1 = jnp.min(jnp.where(le == v1, lane, big), axis=-1, keepdims=True)
    le2 = jnp.where(lane == i1, neg, le)
    v2 = jnp.max(le2, axis=-1, keepdims=True)
    i2 = jnp.min(jnp.where(le2 == v2, lane, big), axis=-1, keepdims=True)
    e2 = jnp.exp(v2 - v1)
    w1 = g_w / (1.0 + e2)
    w2 = g_w * e2 / (1.0 + e2)
    route = jnp.where(lane == 0, (i1 - N_GROUPS).astype(F32),
                      jnp.where(lane == 1, (i2 - N_GROUPS).astype(F32),
                                jnp.where(lane == 2, w1, jnp.where(lane == 3, w2, 0.0))))
    route_ref[...] = route


def _router(ox, h1, wxo_bf, gf, wr_hi, wr_lo):
    return pl.pallas_call(
        _router_kernel,
        grid=(N_TOK // TM_TOK,),
        in_specs=[pl.BlockSpec((TM_TOK, XA_WIDTH), lambda i: (i, 0)),
                  pl.BlockSpec((TM_TOK, D_MODEL), lambda i: (i, 0)),
                  pl.BlockSpec((XA_WIDTH, D_MODEL), lambda i: (0, 0)),
                  pl.BlockSpec((1, D_MODEL), lambda i: (0, 0)),
                  pl.BlockSpec((D_MODEL, ROUTE_LANES), lambda i: (0, 0)),
                  pl.BlockSpec((D_MODEL, ROUTE_LANES), lambda i: (0, 0))],
        out_specs=[pl.BlockSpec((TM_TOK, D_MODEL), lambda i: (i, 0)),
                   pl.BlockSpec((TM_TOK, D_MODEL), lambda i: (i, 0)),
                   pl.BlockSpec((TM_TOK, ROUTE_LANES), lambda i: (i, 0))],
        out_shape=[jax.ShapeDtypeStruct((N_TOK, D_MODEL), F32),
                   jax.ShapeDtypeStruct((N_TOK, D_MODEL), F32),
                   jax.ShapeDtypeStruct((N_TOK, ROUTE_LANES), F32)],
        compiler_params=_cparams(1, 40),
        name="router",
    )(ox, h1, wxo_bf, gf, wr_hi, wr_lo)


def _dispatch_kernel(slot_ref, cnt_ref, pad_ref, start_ref, nused_ref,
                     x_ref, xs_hbm, zrow, sems):
    i = pl.program_id(0)
    sem = sems.at[0]
    zsem = sems.at[1]

    @pl.when(i == 0)
    def _():
        zrow[...] = jnp.zeros_like(zrow)

        def pad_rows(fn):
            def per_expert(e, carry):
                base = start_ref[e]

                def body(r, c2):
                    fn(base + r)
                    return c2
                lax.fori_loop(cnt_ref[e], pad_ref[e], body, 0)
                return carry
            lax.fori_loop(0, N_EXPERTS, per_expert, 0)

        pad_rows(lambda row: pltpu.make_async_copy(zrow.at[pl.ds(0, 1)], xs_hbm.at[pl.ds(row, 1)], zsem).start())
        pad_rows(lambda row: pltpu.make_async_copy(zrow.at[pl.ds(0, 1)], xs_hbm.at[pl.ds(0, 1)], zsem).wait())

        def unused_tiles(fn):
            def body(t, carry):
                fn(pl.multiple_of(t * TS, TS))
                return carry
            lax.fori_loop(nused_ref[0], N_TILES, body, 0)

        unused_tiles(lambda r0: pltpu.make_async_copy(zrow, xs_hbm.at[pl.ds(r0, TS)], zsem).start())
        unused_tiles(lambda r0: pltpu.make_async_copy(zrow, xs_hbm.at[pl.ds(0, TS)], zsem).wait())

    base = i * (2 * TM_TOK)

    for r in range(2 * TM_TOK):
        pltpu.make_async_copy(x_ref.at[pl.ds(r // 2, 1)], xs_hbm.at[pl.ds(slot_ref[base + r], 1)],
                              sem).start(priority=r % 2)

    for _ in range(2):
        pltpu.make_async_copy(x_ref, xs_hbm.at[pl.ds(0, TM_TOK)], sem).wait()


def _dispatch(slot, counts, padded, seg_start, n_used, xn):
    grid_spec = pltpu.PrefetchScalarGridSpec(
        num_scalar_prefetch=5,
        grid=(N_TOK // TM_TOK,),
        in_specs=[pl.BlockSpec((TM_TOK, D_MODEL), lambda i, *_: (i, 0))],
        out_specs=pl.BlockSpec(memory_space=pl.ANY),
        scratch_shapes=[pltpu.VMEM((TS, D_MODEL), F32),
                        pltpu.SemaphoreType.DMA((2,))],
    )
    return pl.pallas_call(
        _dispatch_kernel,
        grid_spec=grid_spec,
        out_shape=jax.ShapeDtypeStruct((N_SLOTS, D_MODEL), F32),
        compiler_params=_cparams(1, 16),
        name="dispatch",
    )(slot, counts, padded, seg_start, n_used, xn)


def _moe_kernel(te_ref, nused_ref, first_ref, nxt_ref, par_ref,
                xs_ref, wg_hbm, wu_hbm, wd_hbm, ys_ref,
                wg_f, wu_f, wd_f, wg_bf, wu_bf, wd_bf, wsem):
    i = pl.program_id(0)
    active = i < nused_ref[0]

    def weight_copies(e, b):
        return (pltpu.make_async_copy(wg_hbm.at[e], wg_f.at[b], wsem.at[b]),
                pltpu.make_async_copy(wu_hbm.at[e], wu_f.at[b], wsem.at[b]),
                pltpu.make_async_copy(wd_hbm.at[e], wd_f.at[b], wsem.at[b]))

    @pl.when(active)
    def _():
        @pl.when(first_ref[i] == 1)
        def _():
            b = par_ref[i]

            @pl.when(i == 0)
            def _():
                for c in weight_copies(te_ref[0], b):
                    c.start()

            for c in weight_copies(te_ref[i], b):
                c.wait()

            @pl.when(nxt_ref[i] >= 0)
            def _():
                for c in weight_copies(nxt_ref[i], 1 - b):
                    c.start()

            wg_bf[...] = wg_f[b].astype(BF16)
            wu_bf[...] = wu_f[b].astype(BF16)
            wd_bf[...] = wd_f[b].astype(BF16)

        x = xs_ref[...].astype(BF16)
        a = _dot(x, wg_bf[...])
        u = _dot(x, wu_bf[...])
        hm = (a * _sigmoid(a)) * u
        ys_ref[...] = _dot(hm.astype(BF16), wd_bf[...])

    @pl.when(jnp.logical_not(active))
    def _():
        ys_ref[...] = jnp.zeros_like(ys_ref)


def _moe(tile_expert, n_used, first, nxt, parity, xs, wg, wu, wd):
    grid_spec = pltpu.PrefetchScalarGridSpec(
        num_scalar_prefetch=5,
        grid=(N_TILES,),
        in_specs=[
            pl.BlockSpec((TS, D_MODEL), lambda i, te, nu, *_: (jnp.minimum(i, nu[0] - 1), 0)),
            pl.BlockSpec(memory_space=pl.ANY),
            pl.BlockSpec(memory_space=pl.ANY),
            pl.BlockSpec(memory_space=pl.ANY),
        ],
        out_specs=pl.BlockSpec((TS, D_MODEL), lambda i, *_: (i, 0)),
        scratch_shapes=[
            pltpu.VMEM((2, D_MODEL, EXP_FF), F32),
            pltpu.VMEM((2, D_MODEL, EXP_FF), F32),
            pltpu.VMEM((2, EXP_FF, D_MODEL), F32),
            pltpu.VMEM((D_MODEL, EXP_FF), BF16),
            pltpu.VMEM((D_MODEL, EXP_FF), BF16),
            pltpu.VMEM((EXP_FF, D_MODEL), BF16),
            pltpu.SemaphoreType.DMA((2,)),
        ],
    )
    return pl.pallas_call(
        _moe_kernel,
        grid_spec=grid_spec,
        out_shape=jax.ShapeDtypeStruct((N_SLOTS, D_MODEL), F32),
        compiler_params=_cparams(1, 48),
        name="moe",
    )(tile_expert, n_used, first, nxt, parity, xs, wg, wu, wd)


def _final_kernel(slot_ref, h2_ref, route_ref, g_ref, ys_hbm, yp_ref, ysm_ref, ybuf, sem, *, n_prompt_blocks):
    i = pl.program_id(0)
    n = pl.num_programs(0)
    buf = i % 2

    def gather(blk, b):
        base = blk * (2 * TM_TOK)
        for r in range(2 * TM_TOK):
            pltpu.make_async_copy(ys_hbm.at[pl.ds(slot_ref[base + r], 1)], ybuf.at[b, r % 2, pl.ds(r // 2, 1)],
                                  sem.at[b]).start(priority=r % 2)

    @pl.when(i == 0)
    def _():
        gather(0, 0)

    for b in range(2):
        @pl.when((i + 1 < n) & ((i + 1) % 2 == b))
        def _():
            gather(i + 1, b)

    for k in range(2):
        pltpu.make_async_copy(ys_hbm.at[pl.ds(0, TM_TOK)], ybuf.at[buf, k], sem.at[buf]).wait()

    lane = lax.broadcasted_iota(I32, (TM_TOK, ROUTE_LANES), 1)
    route = route_ref[...]
    w0 = jnp.sum(jnp.where(lane == 2, route, 0.0), axis=-1, keepdims=True)
    w1 = jnp.sum(jnp.where(lane == 3, route, 0.0), axis=-1, keepdims=True)
    y = _rms(h2_ref[...] + (w0 * ybuf[buf, 0] + w1 * ybuf[buf, 1]), g_ref[...])

    @pl.when(i < n_prompt_blocks)
    def _():
        yp_ref[...] = y

    @pl.when(i >= n_prompt_blocks)
    def _():
        ysm_ref[...] = y


def _final(slot, h2, route, g, ys):
    npb = SEQ // TM_TOK
    grid_spec = pltpu.PrefetchScalarGridSpec(
        num_scalar_prefetch=1,
        grid=(N_TOK // TM_TOK,),
        in_specs=[pl.BlockSpec((TM_TOK, D_MODEL), lambda i, s: (i, 0)),
                  pl.BlockSpec((TM_TOK, ROUTE_LANES), lambda i, s: (i, 0)),
                  pl.BlockSpec((1, D_MODEL), lambda i, s: (0, 0)),
                  pl.BlockSpec(memory_space=pl.ANY)],
        out_specs=[pl.BlockSpec((TM_TOK, D_MODEL), lambda i, s: (jnp.minimum(i, npb - 1), 0)),
                   pl.BlockSpec((TM_TOK, D_MODEL), lambda i, s: (jnp.maximum(i - npb, 0), 0))],
        scratch_shapes=[pltpu.VMEM((2, 2, TM_TOK, D_MODEL), F32),
                        pltpu.SemaphoreType.DMA((2,))],
    )
    return pl.pallas_call(
        functools.partial(_final_kernel, n_prompt_blocks=npb),
        grid_spec=grid_spec,
        out_shape=[jax.ShapeDtypeStruct((SEQ, D_MODEL), F32),
                   jax.ShapeDtypeStruct((N_SAMPLE, D_MODEL), F32)],
        compiler_params=_cparams(1, 40),
        name="final",
    )(slot, h2, route, g, ys)


def _routing_tables(route):
    eid = route[:, 0:2].astype(I32).reshape(-1)
    onehot = (eid[:, None] == jnp.arange(N_EXPERTS, dtype=I32)[None, :]).astype(I32)
    csum = jnp.cumsum(onehot, axis=0)
    rank = jnp.sum(onehot * (csum - onehot), axis=1)
    counts = csum[-1]
    padded = (counts + TS - 1) // TS * TS
    seg_end = jnp.cumsum(padded)
    seg_start = seg_end - padded
    slot = jnp.sum(onehot * seg_start[None, :], axis=1) + rank
    n_used = (seg_end[-1] // TS).astype(I32)
    tile_start = jnp.arange(N_TILES, dtype=I32) * TS
    tile_expert = jnp.sum((seg_end[None, :] <= tile_start[:, None]).astype(I32), axis=1)
    used = jnp.arange(N_TILES) < n_used
    last_expert = jnp.max(jnp.where(used, tile_expert, 0))
    tile_expert = jnp.where(used, tile_expert, last_expert).astype(I32)
    prev = jnp.concatenate([jnp.full((1,), -1, I32), tile_expert[:-1]])
    first = (used & (tile_expert != prev)).astype(I32)
    parity = ((jnp.cumsum(first) - 1) % 2).astype(I32)
    e_ids = jnp.arange(N_EXPERTS, dtype=I32)
    later = (e_ids[None, :] > e_ids[:, None]) & (counts[None, :] > 0)
    next_nonempty = jnp.min(jnp.where(later, e_ids[None, :], N_EXPERTS), axis=1)
    next_nonempty = jnp.where(next_nonempty == N_EXPERTS, -1, next_nonempty)
    nxt = jnp.sum((tile_expert[:, None] == e_ids[None, :]).astype(I32) * next_nonempty[None, :], axis=1)
    return (slot.astype(I32), counts, padded, seg_start, tile_expert, n_used.reshape(1),
            first, nxt.astype(I32), parity)


def _rep_heads(x, n_heads):
    xpad = jnp.pad(x, ((0, 0), (0, SAMPLE_PAD - DEC_SEQ), (0, 0)))
    return jnp.tile(xpad, (1, n_heads, 1))


def kernel(x_prompt, x_sample, mem_prompt, cache_sb_k, cache_sb_v, state_hgrn, cache_mem_k, cache_mem_v,
           page_table, norm_mix, w_in, sb_logit_bias, lb_param, g_sb_out, g_hg_out, w_out, norm_xa, norm_mem,
           w_xq, w_xkv, w_xo, norm_ffn, w_route_group, w_route_expert, w_e_gate, w_e_up, w_e_down, norm_final):
    xp = x_prompt.reshape(SEQ, D_MODEL)
    xs = x_sample.reshape(N_SAMPLE, D_MODEL)

    lb = jnp.cumsum(jax.nn.softmax(lb_param.astype(F32), axis=0), axis=0)[0].reshape(1, WIDTH)
    w_in_bf = w_in[0].astype(BF16)
    w_out_bf = w_out[0].astype(BF16)
    w_xq_bf = w_xq[0].astype(BF16)
    w_xkv_bf = w_xkv[0].astype(BF16)
    w_xo_bf = w_xo[0].astype(BF16)
    w_r = jnp.concatenate([w_route_group[0],
                           jnp.transpose(w_route_expert[0], (1, 0, 2)).reshape(D_MODEL, N_EXPERTS)], axis=1)
    w_r = jnp.pad(w_r, ((0, 0), (0, ROUTE_LANES - w_r.shape[1])))
    w_r_hi = w_r.astype(BF16)
    w_r_lo = (w_r - w_r_hi.astype(F32)).astype(BF16)
    bias = sb_logit_bias[0].astype(F32)

    q_bf, k_bf, v_bf, kf_p, vf_p, kf_s, vf_s, hgin = _inproj(xp, xs, norm_mix[0].reshape(1, D_MODEL), w_in_bf)
    a_p = _sb_prompt(bias, q_bf, k_bf, v_bf, g_sb_out[0].reshape(1, HD))
    q_rows = jnp.transpose(q_bf[SEQ:].reshape(DEC_BATCH, DEC_SEQ, HEADS, HD), (0, 2, 1, 3))
    q_rows = q_rows.reshape(DEC_BATCH, SB_ROWS, HD)
    brow = jnp.broadcast_to(jnp.repeat(bias, DEC_SEQ)[:, None], (SB_ROWS, PAGE_COLS))
    a_s = _sb_sample(page_table.reshape(-1).astype(I32), q_rows, brow,
                     kf_s.reshape(DEC_BATCH, SB_ROWS, HD), vf_s.reshape(DEC_BATCH, SB_ROWS, HD),
                     g_sb_out[0].reshape(1, HD), cache_sb_k[0], cache_sb_v[0])
    a_s = jnp.transpose(a_s.reshape(DEC_BATCH, HEADS, DEC_SEQ, HD), (0, 2, 1, 3)).reshape(N_SAMPLE, WIDTH)
    g_hg = g_hg_out[0].reshape(1, HD)
    b_p, s_p = _hgrn_prompt(hgin, lb, g_hg)
    hgin_s = jnp.pad(hgin[SEQ:].reshape(DEC_BATCH, DEC_SEQ, 4 * WIDTH),
                     ((0, 0), (0, SAMPLE_PAD - DEC_SEQ), (0, 0))).reshape(DEC_BATCH * SAMPLE_PAD, 4 * WIDTH)
    b_s, s_s = _hgrn_sample(hgin_s, lb, g_hg, state_hgrn[0])
    b_s = b_s.reshape(DEC_BATCH, SAMPLE_PAD, WIDTH)[:, :DEC_SEQ].reshape(N_SAMPLE, WIDTH)
    h1, qx = _post_mixer(a_p, a_s.astype(BF16), b_p, b_s, xp, xs, w_out_bf,
                         norm_xa[0].reshape(1, D_MODEL), w_xq_bf)

    mk_p, mv_p = _mem_kv(mem_prompt[0], norm_mem[0].reshape(1, D_MODEL), w_xkv_bf)
    ox_p = _xattn_prompt(qx, mk_p, mv_p)
    qxrep = _rep_heads(qx[SEQ:].reshape(DEC_BATCH, DEC_SEQ, XA_WIDTH), XA_HEADS)
    ox_s = _xattn_sample(qxrep, cache_mem_k[0], cache_mem_v[0])
    ox = jnp.concatenate([ox_p, ox_s[:, :DEC_SEQ].reshape(N_SAMPLE, XA_WIDTH).astype(BF16)], axis=0)
    h2, xn2, route = _router(ox, h1, w_xo_bf, norm_ffn[0].reshape(1, D_MODEL), w_r_hi, w_r_lo)

    slot, counts, padded, seg_start, tile_expert, n_used, first, nxt, parity = _routing_tables(route)
    xs_sorted = _dispatch(slot, counts, padded, seg_start, n_used, xn2)
    ys_sorted = _moe(tile_expert, n_used, first, nxt, parity, xs_sorted, w_e_gate[0], w_e_up[0], w_e_down[0])
    y_p, y_s = _final(slot, h2, route, norm_final.reshape(1, D_MODEL), ys_sorted)

    sb_shape_p = (1, 1, SEQ, HEADS, HD)
    sb_shape_s = (1, DEC_BATCH, DEC_SEQ, HEADS, HD)
    return (y_p.reshape(1, SEQ, D_MODEL), y_s.reshape(DEC_BATCH, DEC_SEQ, D_MODEL),
            kf_p.reshape(sb_shape_p), vf_p.reshape(sb_shape_p), s_p.reshape(1, 1, HEADS, HD, HD),
            mk_p.reshape(1, 1, N_MEM, XA_HEADS, HD), mv_p.reshape(1, 1, N_MEM, XA_HEADS, HD),
            kf_s.reshape(sb_shape_s), vf_s.reshape(sb_shape_s), s_s.reshape(1, DEC_BATCH, HEADS, HD, HD))
```

```python
import functools
import math

import jax
import jax.numpy as jnp
from jax import lax
from jax.experimental import pallas as pl
from jax.experimental.pallas import tpu as pltpu

F32 = jnp.float32
BF16 = jnp.bfloat16
I32 = jnp.int32

D_MODEL = 2048
SEQ = 8192
DEC_BATCH = 128
DEC_SEQ = 4
N_SAMPLE = DEC_BATCH * DEC_SEQ
N_TOK = SEQ + N_SAMPLE
PAGE = 128
N_PAGES = 16
HEADS = 8
HD = 128
WIDTH = HEADS * HD
XA_HEADS = 4
XA_WIDTH = XA_HEADS * HD
N_MEM = 256
N_GROUPS = 4
EXP_PER_GROUP = 8
N_EXPERTS = 32
EXP_FF = 512
HG_CHUNK = 32
EPS = 1e-6
ATT_SCALE = 1.0 / math.sqrt(HD)

LANES = 128
SUBLANES = 8
MIB = 1024 * 1024

TM_IN = 512
TQ = 512
SB_HEADS_PER_STEP = 4
TM_TOK = 256
HG_ROWS = 128
TS = 256
SAMPLE_PAD = 8
PAGES_PER_STEP = 16
N_SLOTS = ((2 * N_TOK + N_EXPERTS * (TS - 1)) + TS - 1) // TS * TS
N_TILES = N_SLOTS // TS


def _cparams(n_axes, vmem_mib):
    return pltpu.CompilerParams(dimension_semantics=("arbitrary",) * n_axes,
                                vmem_limit_bytes=vmem_mib * MIB)


def _dot(a, b):
    return jnp.dot(a, b, preferred_element_type=F32)


def _dot_nt(a, b):
    return lax.dot_general(a, b, (((1,), (1,)), ((), ())), preferred_element_type=F32)


def _rms(x, g):
    ms = jnp.mean(x * x, axis=-1, keepdims=True)
    return x * lax.rsqrt(ms + EPS) * g


def _sigmoid(x):
    return 1.0 / (1.0 + jnp.exp(-x))


def _softplus(z):
    return jnp.maximum(z, 0.0) + jnp.log(1.0 + jnp.exp(-jnp.abs(z)))


def _suffix_matrix(n):
    r = lax.broadcasted_iota(I32, (n, n), 0)
    c = lax.broadcasted_iota(I32, (n, n), 1)
    return jnp.where(r >= c, 1.0, 0.0).astype(BF16)


def _split_bf16(x):
    hi = x.astype(BF16)
    lo = (x - hi.astype(F32)).astype(BF16)
    return hi, lo


def _inproj_kernel(xp_ref, xs_ref, g_ref, w_ref,
                   q_ref, kb_ref, vb_ref, kfp_ref, vfp_ref, kfs_ref, vfs_ref, hg_ref,
                   xn_ref, *, n_prompt_blocks):
    i = pl.program_id(0)
    j = pl.program_id(1)
    is_prompt = i < n_prompt_blocks

    @pl.when((j == 0) & is_prompt)
    def _():
        xn_ref[...] = _rms(xp_ref[...], g_ref[...]).astype(BF16)

    @pl.when((j == 0) & jnp.logical_not(is_prompt))
    def _():
        xn_ref[...] = _rms(xs_ref[...], g_ref[...]).astype(BF16)

    def proj():
        return _dot(xn_ref[...], w_ref[...])

    @pl.when(j == 0)
    def _():
        q_ref[...] = (proj() * ATT_SCALE).astype(BF16)

    for col, bf_ref, fp_ref, fs_ref in ((1, kb_ref, kfp_ref, kfs_ref), (2, vb_ref, vfp_ref, vfs_ref)):
        @pl.when((j == col) & is_prompt)
        def _():
            fp_ref[...] = proj()
            bf_ref[...] = fp_ref[...].astype(BF16)

        @pl.when((j == col) & jnp.logical_not(is_prompt))
        def _():
            fs_ref[...] = proj()
            bf_ref[...] = fs_ref[...].astype(BF16)

    @pl.when(j >= 3)
    def _():
        hg_ref[...] = proj()


def _inproj(xp, xs, g, w_bf):
    npb = SEQ // TM_IN
    nblk = N_TOK // TM_IN
    ncol = w_bf.shape[1] // WIDTH
    tok_spec = pl.BlockSpec((TM_IN, WIDTH), lambda i, j: (i, 0))
    return pl.pallas_call(
        functools.partial(_inproj_kernel, n_prompt_blocks=npb),
        grid=(nblk, ncol),
        in_specs=[
            pl.BlockSpec((TM_IN, D_MODEL), lambda i, j: (jnp.minimum(i, npb - 1), 0)),
            pl.BlockSpec((TM_IN, D_MODEL), lambda i, j: (jnp.maximum(i - npb, 0), 0)),
            pl.BlockSpec((1, D_MODEL), lambda i, j: (0, 0)),
            pl.BlockSpec((D_MODEL, WIDTH), lambda i, j: (0, j)),
        ],
        out_specs=[
            tok_spec, tok_spec, tok_spec,
            pl.BlockSpec((TM_IN, WIDTH), lambda i, j: (jnp.minimum(i, npb - 1), 0)),
            pl.BlockSpec((TM_IN, WIDTH), lambda i, j: (jnp.minimum(i, npb - 1), 0)),
            pl.BlockSpec((TM_IN, WIDTH), lambda i, j: (jnp.maximum(i - npb, 0), 0)),
            pl.BlockSpec((TM_IN, WIDTH), lambda i, j: (jnp.maximum(i - npb, 0), 0)),
            pl.BlockSpec((TM_IN, WIDTH), lambda i, j: (i, jnp.clip(j - 3, 0, 3))),
        ],
        out_shape=[
            jax.ShapeDtypeStruct((N_TOK, WIDTH), BF16),
            jax.ShapeDtypeStruct((N_TOK, WIDTH), BF16),
            jax.ShapeDtypeStruct((N_TOK, WIDTH), BF16),
            jax.ShapeDtypeStruct((SEQ, WIDTH), F32),
            jax.ShapeDtypeStruct((SEQ, WIDTH), F32),
            jax.ShapeDtypeStruct((N_SAMPLE, WIDTH), F32),
            jax.ShapeDtypeStruct((N_SAMPLE, WIDTH), F32),
            jax.ShapeDtypeStruct((N_TOK, 4 * WIDTH), F32),
        ],
        scratch_shapes=[pltpu.VMEM((TM_IN, D_MODEL), BF16)],
        compiler_params=_cparams(2, 48),
        name="inproj",
    )(xp, xs, g, w_bf)


def _sb_tiles(qs, ks, vs, biases, valids, u, r_refs, acc_refs):
    n = u.shape[0]
    zs = [_dot_nt(q, k) + b for q, k, b in zip(qs, ks, biases)]
    sps = []
    for z, valid in zip(zs, valids):
        sp = _softplus(z)
        sps.append(sp if valid is None else jnp.where(valid, sp, 0.0))
    sums = [[_dot(sp[:, c:c + n].astype(BF16), u) for c in range(0, sp.shape[1], n)] for sp in sps]

    carried = {}
    ws = []
    for z, valid, blocks, r_ref in zip(zs, valids, sums, r_refs):
        r = carried[id(r_ref)][1] if id(r_ref) in carried else r_ref[...]
        parts = [None] * len(blocks)
        for blk in range(len(blocks) - 1, -1, -1):
            parts[blk] = blocks[blk] + r
            r = r + blocks[blk][:, 0:1]
        carried[id(r_ref)] = (r_ref, r)
        s_all = parts[0] if len(parts) == 1 else jnp.concatenate(parts, axis=-1)
        w = jnp.exp(z - s_all)
        ws.append(w if valid is None else jnp.where(valid, w, 0.0))
    for r_ref, r in carried.values():
        r_ref[...] = r

    totals = {}
    for w, v, acc_ref in zip(ws, vs, acc_refs):
        d = _dot(w.astype(BF16), v)
        totals[id(acc_ref)] = (acc_ref, d + totals[id(acc_ref)][1] if id(acc_ref) in totals else d)
    for acc_ref, d in totals.values():
        acc_ref[...] += d


def _head_rms(o, g):
    return o * lax.rsqrt(jnp.mean(o * o, axis=-1, keepdims=True) + EPS) * g


def _sb_prompt_kernel(bias_ref, q_ref, k_ref, v_ref, g_ref, o_ref, acc_ref, r_ref):
    hg = pl.program_id(0)
    i = pl.program_id(1)
    u = _suffix_matrix(256)
    acc_ref[...] = jnp.zeros_like(acc_ref)
    r_ref[...] = jnp.zeros_like(r_ref)

    heads = range(SB_HEADS_PER_STEP)
    lanes = [slice(g * HD, (g + 1) * HD) for g in heads]
    biases = [bias_ref[hg * SB_HEADS_PER_STEP + g] for g in heads]
    r_refs = [r_ref.at[g] for g in heads]
    acc_refs = [acc_ref.at[g] for g in heads]

    def tiles(j, valid):
        start = pl.multiple_of(j * TQ, TQ)
        _sb_tiles([q_ref[:, sl] for sl in lanes],
                  [k_ref[pl.ds(start, TQ), sl] for sl in lanes],
                  [v_ref[pl.ds(start, TQ), sl] for sl in lanes],
                  biases, [valid] * SB_HEADS_PER_STEP, u, r_refs, acc_refs)

    row = lax.broadcasted_iota(I32, (TQ, TQ), 0)
    col = lax.broadcasted_iota(I32, (TQ, TQ), 1)
    tiles(i, col < row)

    def body(it, carry):
        tiles(i - 1 - it, None)
        return carry

    lax.fori_loop(0, i, body, 0)
    for g in range(SB_HEADS_PER_STEP):
        o_ref[:, g * HD:(g + 1) * HD] = _head_rms(acc_ref[g], g_ref[...]).astype(BF16)


def _sb_prompt(bias, q_bf, k_bf, v_bf, g):
    gw = SB_HEADS_PER_STEP * HD
    return pl.pallas_call(
        _sb_prompt_kernel,
        grid=(HEADS // SB_HEADS_PER_STEP, SEQ // TQ),
        in_specs=[
            pl.BlockSpec(memory_space=pltpu.SMEM),
            pl.BlockSpec((TQ, gw), lambda h, i: (i, h)),
            pl.BlockSpec((SEQ, gw), lambda h, i: (0, h), pipeline_mode=pl.Buffered(1)),
            pl.BlockSpec((SEQ, gw), lambda h, i: (0, h), pipeline_mode=pl.Buffered(1)),
            pl.BlockSpec((1, HD), lambda h, i: (0, 0)),
        ],
        out_specs=pl.BlockSpec((TQ, gw), lambda h, i: (i, h)),
        out_shape=jax.ShapeDtypeStruct((SEQ, WIDTH), BF16),
        scratch_shapes=[pltpu.VMEM((SB_HEADS_PER_STEP, TQ, HD), F32),
                        pltpu.VMEM((SB_HEADS_PER_STEP, TQ, 1), F32)],
        compiler_params=_cparams(2, 48),
        name="sb_prompt",
    )(bias, q_bf, k_bf, v_bf, g)


SB_ROWS = HEADS * DEC_SEQ
PAGE_COLS = PAGE * HEADS
NEW_COLS = LANES
SUFFIX_BLOCK = 256


def _sb_sample_kernel(pt_ref, q_ref, brow_ref, knew_ref, vnew_ref, g_ref, *rest):
    k_refs = rest[:PAGES_PER_STEP]
    v_refs = rest[PAGES_PER_STEP:2 * PAGES_PER_STEP]
    o_ref, acc_ref, r_ref, kn_ref, vn_ref = rest[2 * PAGES_PER_STEP:]
    c = pl.program_id(1)
    q = q_ref[0]
    row = lax.broadcasted_iota(I32, (SB_ROWS, PAGE_COLS), 0)
    col = lax.broadcasted_iota(I32, (SB_ROWS, PAGE_COLS), 1)
    own = (col % HEADS) == (row // DEC_SEQ)
    bias = brow_ref[...]

    @pl.when(c == 0)
    def _():
        acc_ref[...] = jnp.zeros_like(acc_ref)
        r_ref[...] = jnp.zeros_like(r_ref)
        kn_ref[...] = jnp.zeros_like(kn_ref)
        vn_ref[...] = jnp.zeros_like(vn_ref)
        kn_ref[0:SB_ROWS, :] = knew_ref[0]
        vn_ref[0:SB_ROWS, :] = vnew_ref[0]
        rn = lax.broadcasted_iota(I32, (SB_ROWS, NEW_COLS), 0)
        cn = lax.broadcasted_iota(I32, (SB_ROWS, NEW_COLS), 1)
        valid = ((cn % HEADS) == (rn // DEC_SEQ)) & ((cn // HEADS) < (rn % DEC_SEQ))
        _sb_tiles([q], [kn_ref[...].astype(BF16)], [vn_ref[...].astype(BF16)], [bias[:, 0:NEW_COLS]], [valid],
                  _suffix_matrix(NEW_COLS), [r_ref], [acc_ref])

    order = range(PAGES_PER_STEP - 1, -1, -1)
    _sb_tiles([q] * PAGES_PER_STEP,
              [k_refs[p][0].reshape(PAGE_COLS, HD).astype(BF16) for p in order],
              [v_refs[p][0].reshape(PAGE_COLS, HD).astype(BF16) for p in order],
              [bias] * PAGES_PER_STEP, [own] * PAGES_PER_STEP, _suffix_matrix(SUFFIX_BLOCK),
              [r_ref] * PAGES_PER_STEP, [acc_ref] * PAGES_PER_STEP)

    @pl.when(c == pl.num_programs(1) - 1)
    def _():
        o_ref[0] = _head_rms(acc_ref[...], g_ref[...]).astype(BF16).astype(F32)


def _sb_sample(page_table, q_rows, brow, knew, vnew, g, kcache, vcache):
    n_steps = N_PAGES // PAGES_PER_STEP

    def page_spec(p):
        def imap(b, c, pt):
            return (pt[b * N_PAGES + (n_steps - 1 - c) * PAGES_PER_STEP + p], 0, 0, 0)
        return pl.BlockSpec((1, PAGE, HEADS, HD), imap)

    grid_spec = pltpu.PrefetchScalarGridSpec(
        num_scalar_prefetch=1,
        grid=(DEC_BATCH, n_steps),
        in_specs=[
            pl.BlockSpec((1, SB_ROWS, HD), lambda b, c, pt: (b, 0, 0)),
            pl.BlockSpec((SB_ROWS, PAGE_COLS), lambda b, c, pt: (0, 0)),
            pl.BlockSpec((1, SB_ROWS, HD), lambda b, c, pt: (b, 0, 0)),
            pl.BlockSpec((1, SB_ROWS, HD), lambda b, c, pt: (b, 0, 0)),
            pl.BlockSpec((1, HD), lambda b, c, pt: (0, 0)),
        ] + [page_spec(p) for p in range(PAGES_PER_STEP)] * 2,
        out_specs=pl.BlockSpec((1, SB_ROWS, HD), lambda b, c, pt: (b, 0, 0)),
        scratch_shapes=[
            pltpu.VMEM((SB_ROWS, HD), F32),
            pltpu.VMEM((SB_ROWS, 1), F32),
            pltpu.VMEM((NEW_COLS, HD), F32),
            pltpu.VMEM((NEW_COLS, HD), F32),
        ],
    )
    return pl.pallas_call(
        _sb_sample_kernel,
        grid_spec=grid_spec,
        out_shape=jax.ShapeDtypeStruct((DEC_BATCH, SB_ROWS, HD), F32),
        compiler_params=_cparams(2, 48),
        name="sb_sample",
    )(page_table, q_rows, brow, knew, vnew, g,
      *([kcache] * PAGES_PER_STEP), *([vcache] * PAGES_PER_STEP))


def _hgrn_block(hq, hf, hi, hgate, lb, g_out, state_in, state_out, *, chunk, last, sequential):
    n_chunks = HG_ROWS // chunk
    one_m_lb = 1.0 - lb
    e = jnp.exp(-jnp.abs(hf))
    inv = 1.0 / (1.0 + e)
    sig_pos = jnp.where(hf >= 0, inv, e * inv)
    sig_neg = jnp.where(hf >= 0, e * inv, inv)
    f = lb + one_m_lb * sig_pos
    k = one_m_lb * sig_neg
    g = jnp.log(f)
    q = hq * _sigmoid(hq)

    rin = lax.broadcasted_iota(I32, (HG_ROWS, WIDTH), 0) % chunk
    b = g
    step = 1
    while step < chunk:
        b = b + jnp.where(rin >= step, pltpu.roll(b, step, axis=0), 0.0)
        step *= 2
    cid = lax.broadcasted_iota(I32, (HG_ROWS, WIDTH), 0) // chunk
    b_last_rows = jnp.zeros_like(b)
    b_last = []
    for c in range(n_chunks):
        bl = b[c * chunk + last:c * chunk + last + 1, :]
        b_last.append(bl)
        b_last_rows = jnp.where(cid == c, bl, b_last_rows)

    q_dec = q * jnp.exp(b)
    k_inv = k * jnp.exp(-b)
    k_dec = k * jnp.exp(b_last_rows - b)

    r2 = lax.broadcasted_iota(I32, (HG_ROWS, HG_ROWS), 0)
    c2 = lax.broadcasted_iota(I32, (HG_ROWS, HG_ROWS), 1)
    intra = ((r2 // chunk) == (c2 // chunk)) & (c2 <= r2)
    eye = r2 == c2
    outs = []
    for h in range(HEADS):
        sl = slice(h * HD, (h + 1) * HD)
        qd_h = q_dec[:, sl]
        v_h = hi[:, sl].astype(BF16)
        a = jnp.where(intra, _dot_nt(qd_h.astype(BF16), k_inv[:, sl].astype(BF16)), 0.0)
        o_h = _dot(a.astype(BF16), v_h)
        kdec_t = jnp.transpose(k_dec[:, sl])
        s_cur = state_in(0, h) if sequential else None
        for c in range(n_chunks):
            if not sequential:
                s_cur = state_in(c, h)
            in_rows = (r2 // chunk) == c
            in_cols = (c2 // chunk) == c
            o_h = o_h + _dot(jnp.where(in_rows, qd_h, 0.0).astype(BF16), s_cur.astype(BF16))
            dl = jnp.exp(b_last[c][:, sl])
            dcol = jnp.sum(jnp.where(eye, dl, 0.0), axis=1, keepdims=True)
            s_new = dcol * s_cur + _dot(jnp.where(in_cols, kdec_t, 0.0).astype(BF16), v_h)
            if sequential:
                s_cur = s_new
            else:
                state_out(c, h, s_new)
        if sequential:
            state_out(0, h, s_cur)
        gate = hgate[:, sl]
        outs.append((_head_rms(o_h, g_out) * (gate * _sigmoid(gate))).astype(BF16))
    return jnp.concatenate(outs, axis=-1)


def _hgrn_prompt_kernel(hq_ref, hf_ref, hi_ref, hg_ref, lb_ref, g_ref, o_ref, sfin_ref, s_ref):
    i = pl.program_id(0)

    @pl.when(i == 0)
    def _():
        s_ref[...] = jnp.zeros_like(s_ref)

    def state_in(c, h):
        return s_ref[h]

    def state_out(c, h, s):
        s_ref[h] = s

    o_ref[...] = _hgrn_block(hq_ref[...], hf_ref[...], hi_ref[...], hg_ref[...], lb_ref[...], g_ref[...],
                             state_in, state_out, chunk=HG_CHUNK, last=HG_CHUNK - 1, sequential=True)

    @pl.when(i == pl.num_programs(0) - 1)
    def _():
        sfin_ref[...] = s_ref[...]


def _hgrn_prompt(hgin, lb, g):
    def col(j):
        return pl.BlockSpec((HG_ROWS, WIDTH), lambda i: (i, j))
    return pl.pallas_call(
        _hgrn_prompt_kernel,
        grid=(SEQ // HG_ROWS,),
        in_specs=[col(0), col(1), col(2), col(3),
                  pl.BlockSpec((1, WIDTH), lambda i: (0, 0)),
                  pl.BlockSpec((1, HD), lambda i: (0, 0))],
        out_specs=[pl.BlockSpec((HG_ROWS, WIDTH), lambda i: (i, 0)),
                   pl.BlockSpec((HEADS, HD, HD), lambda i: (0, 0, 0))],
        out_shape=[jax.ShapeDtypeStruct((SEQ, WIDTH), BF16),
                   jax.ShapeDtypeStruct((HEADS, HD, HD), F32)],
        scratch_shapes=[pltpu.VMEM((HEADS, HD, HD), F32)],
        compiler_params=_cparams(1, 32),
        name="hgrn_prompt",
    )(hgin, hgin, hgin, hgin, lb, g)


def _hgrn_sample_kernel(hq_ref, hf_ref, hi_ref, hg_ref, lb_ref, g_ref, s0_ref, o_ref, sout_ref):
    def state_in(c, h):
        return s0_ref[c, h]

    def state_out(c, h, s):
        sout_ref[c, h] = s

    o_ref[...] = _hgrn_block(hq_ref[...], hf_ref[...], hi_ref[...], hg_ref[...], lb_ref[...], g_ref[...],
                             state_in, state_out, chunk=SAMPLE_PAD, last=DEC_SEQ - 1, sequential=False)


def _hgrn_sample(hgin_pad, lb, g, s0):
    bpb = HG_ROWS // SAMPLE_PAD

    def col(j):
        return pl.BlockSpec((HG_ROWS, WIDTH), lambda i: (i, j))
    return pl.pallas_call(
        _hgrn_sample_kernel,
        grid=(DEC_BATCH // bpb,),
        in_specs=[col(0), col(1), col(2), col(3),
                  pl.BlockSpec((1, WIDTH), lambda i: (0, 0)),
                  pl.BlockSpec((1, HD), lambda i: (0, 0)),
                  pl.BlockSpec((bpb, HEADS, HD, HD), lambda i: (i, 0, 0, 0))],
        out_specs=[pl.BlockSpec((HG_ROWS, WIDTH), lambda i: (i, 0)),
                   pl.BlockSpec((bpb, HEADS, HD, HD), lambda i: (i, 0, 0, 0))],
        out_shape=[jax.ShapeDtypeStruct((DEC_BATCH * SAMPLE_PAD, WIDTH), BF16),
                   jax.ShapeDtypeStruct((DEC_BATCH, HEADS, HD, HD), F32)],
        compiler_params=_cparams(1, 48),
        name="hgrn_sample",
    )(hgin_pad, hgin_pad, hgin_pad, hgin_pad, lb, g, s0)


def _post_mixer_kernel(ap_ref, as_ref, bp_ref, bs_ref, xp_ref, xs_ref, wo_ref, gx_ref, wq_ref, h1_ref, qx_ref, *,
                       n_prompt_blocks):
    i = pl.program_id(0)

    def finish(a_ref, b_ref, x_ref):
        y = _dot(a_ref[...], wo_ref[0:WIDTH, :]) + _dot(b_ref[...], wo_ref[WIDTH:2 * WIDTH, :])
        h1 = x_ref[...] + y
        h1_ref[...] = h1
        qx_ref[...] = (_dot(_rms(h1, gx_ref[...]).astype(BF16), wq_ref[...]) * ATT_SCALE).astype(BF16)

    @pl.when(i < n_prompt_blocks)
    def _():
        finish(ap_ref, bp_ref, xp_ref)

    @pl.when(i >= n_prompt_blocks)
    def _():
        finish(as_ref, bs_ref, xs_ref)


def _post_mixer(a_p, a_s, b_p, b_s, xp, xs, wo_bf, gx, wq_bf):
    npb = SEQ // TM_TOK

    def prompt_rows(w):
        return pl.BlockSpec((TM_TOK, w), lambda i: (jnp.minimum(i, npb - 1), 0))

    def sample_rows(w):
        return pl.BlockSpec((TM_TOK, w), lambda i: (jnp.maximum(i - npb, 0), 0))

    return pl.pallas_call(
        functools.partial(_post_mixer_kernel, n_prompt_blocks=npb),
        grid=(N_TOK // TM_TOK,),
        in_specs=[
            prompt_rows(WIDTH), sample_rows(WIDTH), prompt_rows(WIDTH), sample_rows(WIDTH),
            prompt_rows(D_MODEL), sample_rows(D_MODEL),
            pl.BlockSpec((2 * WIDTH, D_MODEL), lambda i: (0, 0)),
            pl.BlockSpec((1, D_MODEL), lambda i: (0, 0)),
            pl.BlockSpec((D_MODEL, XA_WIDTH), lambda i: (0, 0)),
        ],
        out_specs=[pl.BlockSpec((TM_TOK, D_MODEL), lambda i: (i, 0)),
                   pl.BlockSpec((TM_TOK, XA_WIDTH), lambda i: (i, 0))],
        out_shape=[jax.ShapeDtypeStruct((N_TOK, D_MODEL), F32),
                   jax.ShapeDtypeStruct((N_TOK, XA_WIDTH), BF16)],
        compiler_params=_cparams(1, 48),
        name="post_mixer",
    )(a_p, a_s, b_p, b_s, xp, xs, wo_bf, gx, wq_bf)


def _mem_kv_kernel(m_ref, g_ref, w_ref, mk_ref, mv_ref):
    kv = _dot(_rms(m_ref[...], g_ref[...]).astype(BF16), w_ref[...])
    mk_ref[...] = kv[:, 0:XA_WIDTH]
    mv_ref[...] = kv[:, XA_WIDTH:2 * XA_WIDTH]


def _mem_kv(mem, g, w_bf):
    return pl.pallas_call(
        _mem_kv_kernel,
        out_shape=[jax.ShapeDtypeStruct((N_MEM, XA_WIDTH), F32)] * 2,
        compiler_params=pltpu.CompilerParams(vmem_limit_bytes=32 * MIB),
        name="mem_kv",
    )(mem, g, w_bf)


def _softmax_rows(s):
    m = jnp.max(s, axis=-1, keepdims=True)
    e = jnp.exp(s - m)
    return e / jnp.sum(e, axis=-1, keepdims=True)


def _xattn_prompt_kernel(q_ref, mk_ref, mv_ref, o_ref):
    mk = mk_ref[...].astype(BF16)
    mv = mv_ref[...].astype(BF16)
    outs = []
    for h in range(XA_HEADS):
        sl = slice(h * HD, (h + 1) * HD)
        p = _softmax_rows(_dot_nt(q_ref[:, sl], mk[:, sl]))
        outs.append(_dot(p.astype(BF16), mv[:, sl]))
    o_ref[...] = jnp.concatenate(outs, axis=-1).astype(BF16)


def _xattn_prompt(qx, mk, mv):
    return pl.pallas_call(
        _xattn_prompt_kernel,
        grid=(SEQ // TM_TOK,),
        in_specs=[pl.BlockSpec((TM_TOK, XA_WIDTH), lambda i: (i, 0)),
                  pl.BlockSpec((N_MEM, XA_WIDTH), lambda i: (0, 0)),
                  pl.BlockSpec((N_MEM, XA_WIDTH), lambda i: (0, 0))],
        out_specs=pl.BlockSpec((TM_TOK, XA_WIDTH), lambda i: (i, 0)),
        out_shape=jax.ShapeDtypeStruct((SEQ, XA_WIDTH), BF16),
        compiler_params=_cparams(1, 32),
        name="xattn_prompt",
    )(qx, mk, mv)


XA_BATCHES_PER_STEP = 8


def _xattn_sample_kernel(qrep_ref, mk_ref, mv_ref, o_ref):
    rows = XA_HEADS * SAMPLE_PAD
    row = lax.broadcasted_iota(I32, (rows, XA_WIDTH), 0)
    col = lax.broadcasted_iota(I32, (rows, XA_WIDTH), 1)
    diag = (row // SAMPLE_PAD) == (col // HD)
    r2 = lax.broadcasted_iota(I32, (XA_HEADS * DEC_SEQ, N_MEM * XA_HEADS), 0)
    c2 = lax.broadcasted_iota(I32, (XA_HEADS * DEC_SEQ, N_MEM * XA_HEADS), 1)
    own = (c2 % XA_HEADS) == (r2 // DEC_SEQ)
    for bb in range(XA_BATCHES_PER_STEP):
        s = jnp.where(own, _dot_nt(qrep_ref[bb], mk_ref[bb].astype(BF16)), -jnp.inf)
        p = jnp.where(own, _softmax_rows(s), 0.0)
        o_ref[bb] = _dot(p.astype(BF16), mv_ref[bb].astype(BF16)).astype(BF16).astype(F32)


def _xattn_sample(qrep, mk, mv):
    rows = XA_HEADS * SAMPLE_PAD
    nb = XA_BATCHES_PER_STEP
    return pl.pallas_call(
        _xattn_sample_kernel,
        grid=(DEC_BATCH // nb,),
        in_specs=[pl.BlockSpec((nb, XA_HEADS * DEC_SEQ, HD), lambda i: (i, 0, 0)),
                  pl.BlockSpec((nb, N_MEM * XA_HEADS, HD), lambda i: (i, 0, 0)),
                  pl.BlockSpec((nb, N_MEM * XA_HEADS, HD), lambda i: (i, 0, 0))],
        out_specs=pl.BlockSpec((nb, XA_HEADS * DEC_SEQ, HD), lambda i: (i, 0, 0)),
        out_shape=jax.ShapeDtypeStruct((DEC_BATCH, XA_HEADS * DEC_SEQ, HD), F32),
        compiler_params=_cparams(1, 40),
        name="xattn_sample",
    )(qrep, mk, mv)


ROUTE_LANES = LANES


def _router_kernel(ox_ref, h1_ref, wxo_ref, gf_ref, wr_hi_ref, wr_lo_ref, h2_ref, xn_ref, route_ref):
    h2 = h1_ref[...] + _dot(ox_ref[...], wxo_ref[...])
    h2_ref[...] = h2
    xn = _rms(h2, gf_ref[...])
    xn_ref[...] = xn
    x_hi = xn.astype(BF16)
    x_lo = (xn - x_hi.astype(F32)).astype(BF16)
    logits = _dot(x_hi, wr_hi_ref[...]) + (_dot(x_lo, wr_hi_ref[...]) + _dot(x_hi, wr_lo_ref[...]))

    lane = lax.broadcasted_iota(I32, logits.shape, 1).astype(F32)
    neg = jnp.float32(-jnp.inf)
    big = jnp.float32(ROUTE_LANES)
    lg = jnp.where(lane < N_GROUPS, logits, neg)
    gmax = jnp.max(lg, axis=-1, keepdims=True)
    g_idx = jnp.min(jnp.where(lg == gmax, lane, big), axis=-1, keepdims=True)
    g_w = 1.0 / jnp.sum(jnp.exp(lg - gmax), axis=-1, keepdims=True)
    lo_lane = N_GROUPS + g_idx * EXP_PER_GROUP
    le = jnp.where((lane >= lo_lane) & (lane < lo_lane + EXP_PER_GROUP), logits, neg)
    v1 = jnp.max(le, axis=-1, keepdims=True)
    i1 = jnp.min(jnp.where(le == v1, lane, big), axis=-1, keepdims=True)
    le2 = jnp.where(lane == i1, neg, le)
    v2 = jnp.max(le2, axis=-1, keepdims=True)
    i2 = jnp.min(jnp.where(le2 == v2, lane, big), axis=-1, keepdims=True)
    e2 = jnp.exp(v2 - v1)
    w1 = g_w / (1.0 + e2)
    w2 = g_w * e2 / (1.0 + e2)
    route = jnp.where(lane == 0, (i1 - N_GROUPS).astype(F32),
                      jnp.where(lane == 1, (i2 - N_GROUPS).astype(F32),
                                jnp.where(lane == 2, w1, jnp.where(lane == 3, w2, 0.0))))
    route_ref[...] = route


def _router(ox, h1, wxo_bf, gf, wr_hi, wr_lo):
    return pl.pallas_call(
        _router_kernel,
        grid=(N_TOK // TM_TOK,),
        in_specs=[pl.BlockSpec((TM_TOK, XA_WIDTH), lambda i: (i, 0)),
                  pl.BlockSpec((TM_TOK, D_MODEL), lambda i: (i, 0)),
                  pl.BlockSpec((XA_WIDTH, D_MODEL), lambda i: (0, 0)),
                  pl.BlockSpec((1, D_MODEL), lambda i: (0, 0)),
                  pl.BlockSpec((D_MODEL, ROUTE_LANES), lambda i: (0, 0)),
                  pl.BlockSpec((D_MODEL, ROUTE_LANES), lambda i: (0, 0))],
        out_specs=[pl.BlockSpec((TM_TOK, D_MODEL), lambda i: (i, 0)),
                   pl.BlockSpec((TM_TOK, D_MODEL), lambda i: (i, 0)),
                   pl.BlockSpec((TM_TOK, ROUTE_LANES), lambda i: (i, 0))],
        out_shape=[jax.ShapeDtypeStruct((N_TOK, D_MODEL), F32),
                   jax.ShapeDtypeStruct((N_TOK, D_MODEL), F32),
                   jax.ShapeDtypeStruct((N_TOK, ROUTE_LANES), F32)],
        compiler_params=_cparams(1, 40),
        name="router",
    )(ox, h1, wxo_bf, gf, wr_hi, wr_lo)


def _dispatch_kernel(slot_ref, cnt_ref, pad_ref, start_ref, nused_ref,
                     x_ref, xs_hbm, zrow, sems):
    i = pl.program_id(0)
    sem = sems.at[0]
    zsem = sems.at[1]

    @pl.when(i == 0)
    def _():
        zrow[...] = jnp.zeros_like(zrow)

        def pad_rows(fn):
            def per_expert(e, carry):
                base = start_ref[e]

                def body(r, c2):
                    fn(base + r)
                    return c2
                lax.fori_loop(cnt_ref[e], pad_ref[e], body, 0)
                return carry
            lax.fori_loop(0, N_EXPERTS, per_expert, 0)

        pad_rows(lambda row: pltpu.make_async_copy(zrow.at[pl.ds(0, 1)], xs_hbm.at[pl.ds(row, 1)], zsem).start())
        pad_rows(lambda row: pltpu.make_async_copy(zrow.at[pl.ds(0, 1)], xs_hbm.at[pl.ds(0, 1)], zsem).wait())

        def unused_tiles(fn):
            def body(t, carry):
                fn(pl.multiple_of(t * TS, TS))
                return carry
            lax.fori_loop(nused_ref[0], N_TILES, body, 0)

        unused_tiles(lambda r0: pltpu.make_async_copy(zrow, xs_hbm.at[pl.ds(r0, TS)], zsem).start())
        unused_tiles(lambda r0: pltpu.make_async_copy(zrow, xs_hbm.at[pl.ds(0, TS)], zsem).wait())

    base = i * (2 * TM_TOK)

    for r in range(2 * TM_TOK):
        pltpu.make_async_copy(x_ref.at[pl.ds(r // 2, 1)], xs_hbm.at[pl.ds(slot_ref[base + r], 1)],
                              sem).start(priority=r % 2)

    for _ in range(2):
        pltpu.make_async_copy(x_ref, xs_hbm.at[pl.ds(0, TM_TOK)], sem).wait()


def _dispatch(slot, counts, padded, seg_start, n_used, xn):
    grid_spec = pltpu.PrefetchScalarGridSpec(
        num_scalar_prefetch=5,
        grid=(N_TOK // TM_TOK,),
        in_specs=[pl.BlockSpec((TM_TOK, D_MODEL), lambda i, *_: (i, 0))],
        out_specs=pl.BlockSpec(memory_space=pl.ANY),
        scratch_shapes=[pltpu.VMEM((TS, D_MODEL), F32),
                        pltpu.SemaphoreType.DMA((2,))],
    )
    return pl.pallas_call(
        _dispatch_kernel,
        grid_spec=grid_spec,
        out_shape=jax.ShapeDtypeStruct((N_SLOTS, D_MODEL), F32),
        compiler_params=_cparams(1, 16),
        name="dispatch",
    )(slot, counts, padded, seg_start, n_used, xn)


def _moe_kernel(te_ref, nused_ref, first_ref, nxt_ref, par_ref,
                xs_ref, wg_hbm, wu_hbm, wd_hbm, ys_ref,
                wg_f, wu_f, wd_f, wg_bf, wu_bf, wd_bf, wsem):
    i = pl.program_id(0)
    active = i < nused_ref[0]

    def weight_copies(e, b):
        return (pltpu.make_async_copy(wg_hbm.at[e], wg_f.at[b], wsem.at[b]),
                pltpu.make_async_copy(wu_hbm.at[e], wu_f.at[b], wsem.at[b]),
                pltpu.make_async_copy(wd_hbm.at[e], wd_f.at[b], wsem.at[b]))

    @pl.when(active)
    def _():
        @pl.when(first_ref[i] == 1)
        def _():
            b = par_ref[i]

            @pl.when(i == 0)
            def _():
                for c in weight_copies(te_ref[0], b):
                    c.start()

            for c in weight_copies(te_ref[i], b):
                c.wait()

            @pl.when(nxt_ref[i] >= 0)
            def _():
                for c in weight_copies(nxt_ref[i], 1 - b):
                    c.start()

            wg_bf[...] = wg_f[b].astype(BF16)
            wu_bf[...] = wu_f[b].astype(BF16)
            wd_bf[...] = wd_f[b].astype(BF16)

        x = xs_ref[...].astype(BF16)
        a = _dot(x, wg_bf[...])
        u = _dot(x, wu_bf[...])
        hm = (a * _sigmoid(a)) * u
        ys_ref[...] = _dot(hm.astype(BF16), wd_bf[...])

    @pl.when(jnp.logical_not(active))
    def _():
        ys_ref[...] = jnp.zeros_like(ys_ref)


def _moe(tile_expert, n_used, first, nxt, parity, xs, wg, wu, wd):
    grid_spec = pltpu.PrefetchScalarGridSpec(
        num_scalar_prefetch=5,
        grid=(N_TILES,),
        in_specs=[
            pl.BlockSpec((TS, D_MODEL), lambda i, te, nu, *_: (jnp.minimum(i, nu[0] - 1), 0)),
            pl.BlockSpec(memory_space=pl.ANY),
            pl.BlockSpec(memory_space=pl.ANY),
            pl.BlockSpec(memory_space=pl.ANY),
        ],
        out_specs=pl.BlockSpec((TS, D_MODEL), lambda i, *_: (i, 0)),
        scratch_shapes=[
            pltpu.VMEM((2, D_MODEL, EXP_FF), F32),
            pltpu.VMEM((2, D_MODEL, EXP_FF), F32),
            pltpu.VMEM((2, EXP_FF, D_MODEL), F32),
            pltpu.VMEM((D_MODEL, EXP_FF), BF16),
            pltpu.VMEM((D_MODEL, EXP_FF), BF16),
            pltpu.VMEM((EXP_FF, D_MODEL), BF16),
            pltpu.SemaphoreType.DMA((2,)),
        ],
    )
    return pl.pallas_call(
        _moe_kernel,
        grid_spec=grid_spec,
        out_shape=jax.ShapeDtypeStruct((N_SLOTS, D_MODEL), F32),
        compiler_params=_cparams(1, 48),
        name="moe",
    )(tile_expert, n_used, first, nxt, parity, xs, wg, wu, wd)


def _final_kernel(slot_ref, h2_ref, route_ref, g_ref, ys_hbm, yp_ref, ysm_ref, ybuf, sem, *, n_prompt_blocks):
    i = pl.program_id(0)
    n = pl.num_programs(0)
    buf = i % 2

    def gather(blk, b):
        base = blk * (2 * TM_TOK)
        for r in range(2 * TM_TOK):
            pltpu.make_async_copy(ys_hbm.at[pl.ds(slot_ref[base + r], 1)], ybuf.at[b, r % 2, pl.ds(r // 2, 1)],
                                  sem.at[b]).start(priority=r % 2)

    @pl.when(i == 0)
    def _():
        gather(0, 0)

    for b in range(2):
        @pl.when((i + 1 < n) & ((i + 1) % 2 == b))
        def _():
            gather(i + 1, b)

    for k in range(2):
        pltpu.make_async_copy(ys_hbm.at[pl.ds(0, TM_TOK)], ybuf.at[buf, k], sem.at[buf]).wait()

    lane = lax.broadcasted_iota(I32, (TM_TOK, ROUTE_LANES), 1)
    route = route_ref[...]
    w0 = jnp.sum(jnp.where(lane == 2, route, 0.0), axis=-1, keepdims=True)
    w1 = jnp.sum(jnp.where(lane == 3, route, 0.0), axis=-1, keepdims=True)
    y = _rms(h2_ref[...] + (w0 * ybuf[buf, 0] + w1 * ybuf[buf, 1]), g_ref[...])

    @pl.when(i < n_prompt_blocks)
    def _():
        yp_ref[...] = y

    @pl.when(i >= n_prompt_blocks)
    def _():
        ysm_ref[...] = y


def _final(slot, h2, route, g, ys):
    npb = SEQ // TM_TOK
    grid_spec = pltpu.PrefetchScalarGridSpec(
        num_scalar_prefetch=1,
        grid=(N_TOK // TM_TOK,),
        in_specs=[pl.BlockSpec((TM_TOK, D_MODEL), lambda i, s: (i, 0)),
                  pl.BlockSpec((TM_TOK, ROUTE_LANES), lambda i, s: (i, 0)),
                  pl.BlockSpec((1, D_MODEL), lambda i, s: (0, 0)),
                  pl.BlockSpec(memory_space=pl.ANY)],
        out_specs=[pl.BlockSpec((TM_TOK, D_MODEL), lambda i, s: (jnp.minimum(i, npb - 1), 0)),
                   pl.BlockSpec((TM_TOK, D_MODEL), lambda i, s: (jnp.maximum(i - npb, 0), 0))],
        scratch_shapes=[pltpu.VMEM((2, 2, TM_TOK, D_MODEL), F32),
                        pltpu.SemaphoreType.DMA((2,))],
    )
    return pl.pallas_call(
        functools.partial(_final_kernel, n_prompt_blocks=npb),
        grid_spec=grid_spec,
        out_shape=[jax.ShapeDtypeStruct((SEQ, D_MODEL), F32),
                   jax.ShapeDtypeStruct((N_SAMPLE, D_MODEL), F32)],
        compiler_params=_cparams(1, 40),
        name="final",
    )(slot, h2, route, g, ys)


def _routing_tables(route):
    eid = route[:, 0:2].astype(I32).reshape(-1)
    onehot = (eid[:, None] == jnp.arange(N_EXPERTS, dtype=I32)[None, :]).astype(I32)
    csum = jnp.cumsum(onehot, axis=0)
    rank = jnp.sum(onehot * (csum - onehot), axis=1)
    counts = csum[-1]
    padded = (counts + TS - 1) // TS * TS
    seg_end = jnp.cumsum(padded)
    seg_start = seg_end - padded
    slot = jnp.sum(onehot * seg_start[None, :], axis=1) + rank
    n_used = (seg_end[-1] // TS).astype(I32)
    tile_start = jnp.arange(N_TILES, dtype=I32) * TS
    tile_expert = jnp.sum((seg_end[None, :] <= tile_start[:, None]).astype(I32), axis=1)
    used = jnp.arange(N_TILES) < n_used
    last_expert = jnp.max(jnp.where(used, tile_expert, 0))
    tile_expert = jnp.where(used, tile_expert, last_expert).astype(I32)
    prev = jnp.concatenate([jnp.full((1,), -1, I32), tile_expert[:-1]])
    first = (used & (tile_expert != prev)).astype(I32)
    parity = ((jnp.cumsum(first) - 1) % 2).astype(I32)
    e_ids = jnp.arange(N_EXPERTS, dtype=I32)
    later = (e_ids[None, :] > e_ids[:, None]) & (counts[None, :] > 0)
    next_nonempty = jnp.min(jnp.where(later, e_ids[None, :], N_EXPERTS), axis=1)
    next_nonempty = jnp.where(next_nonempty == N_EXPERTS, -1, next_nonempty)
    nxt = jnp.sum((tile_expert[:, None] == e_ids[None, :]).astype(I32) * next_nonempty[None, :], axis=1)
    return (slot.astype(I32), counts, padded, seg_start, tile_expert, n_used.reshape(1),
            first, nxt.astype(I32), parity)


def _rep_heads(x, n_heads):
    xpad = jnp.pad(x, ((0, 0), (0, SAMPLE_PAD - DEC_SEQ), (0, 0)))
    return jnp.tile(xpad, (1, n_heads, 1))


def kernel(x_prompt, x_sample, mem_prompt, cache_sb_k, cache_sb_v, state_hgrn, cache_mem_k, cache_mem_v,
           page_table, norm_mix, w_in, sb_logit_bias, lb_param, g_sb_out, g_hg_out, w_out, norm_xa, norm_mem,
           w_xq, w_xkv, w_xo, norm_ffn, w_route_group, w_route_expert, w_e_gate, w_e_up, w_e_down, norm_final):
    xp = x_prompt.reshape(SEQ, D_MODEL)
    xs = x_sample.reshape(N_SAMPLE, D_MODEL)

    lb = jnp.cumsum(jax.nn.softmax(lb_param.astype(F32), axis=0), axis=0)[0].reshape(1, WIDTH)
    w_in_bf = w_in[0].astype(BF16)
    w_out_bf = w_out[0].astype(BF16)
    w_xq_bf = w_xq[0].astype(BF16)
    w_xkv_bf = w_xkv[0].astype(BF16)
    w_xo_bf = w_xo[0].astype(BF16)
    w_r = jnp.concatenate([w_route_group[0],
                           jnp.transpose(w_route_expert[0], (1, 0, 2)).reshape(D_MODEL, N_EXPERTS)], axis=1)
    w_r = jnp.pad(w_r, ((0, 0), (0, ROUTE_LANES - w_r.shape[1])))
    w_r_hi = w_r.astype(BF16)
    w_r_lo = (w_r - w_r_hi.astype(F32)).astype(BF16)
    bias = sb_logit_bias[0].astype(F32)

    q_bf, k_bf, v_bf, kf_p, vf_p, kf_s, vf_s, hgin = _inproj(xp, xs, norm_mix[0].reshape(1, D_MODEL), w_in_bf)
    a_p = _sb_prompt(bias, q_bf, k_bf, v_bf, g_sb_out[0].reshape(1, HD))
    q_rows = jnp.transpose(q_bf[SEQ:].reshape(DEC_BATCH, DEC_SEQ, HEADS, HD), (0, 2, 1, 3))
    q_rows = q_rows.reshape(DEC_BATCH, SB_ROWS, HD)
    brow = jnp.broadcast_to(jnp.repeat(bias, DEC_SEQ)[:, None], (SB_ROWS, PAGE_COLS))
    a_s = _sb_sample(page_table.reshape(-1).astype(I32), q_rows, brow,
                     kf_s.reshape(DEC_BATCH, SB_ROWS, HD), vf_s.reshape(DEC_BATCH, SB_ROWS, HD),
                     g_sb_out[0].reshape(1, HD), cache_sb_k[0], cache_sb_v[0])
    a_s = jnp.transpose(a_s.reshape(DEC_BATCH, HEADS, DEC_SEQ, HD), (0, 2, 1, 3)).reshape(N_SAMPLE, WIDTH)
    g_hg = g_hg_out[0].reshape(1, HD)
    b_p, s_p = _hgrn_prompt(hgin, lb, g_hg)
    hgin_s = jnp.pad(hgin[SEQ:].reshape(DEC_BATCH, DEC_SEQ, 4 * WIDTH),
                     ((0, 0), (0, SAMPLE_PAD - DEC_SEQ), (0, 0))).reshape(DEC_BATCH * SAMPLE_PAD, 4 * WIDTH)
    b_s, s_s = _hgrn_sample(hgin_s, lb, g_hg, state_hgrn[0])
    b_s = b_s.reshape(DEC_BATCH, SAMPLE_PAD, WIDTH)[:, :DEC_SEQ].reshape(N_SAMPLE, WIDTH)
    h1, qx = _post_mixer(a_p, a_s.astype(BF16), b_p, b_s, xp, xs, w_out_bf,
                         norm_xa[0].reshape(1, D_MODEL), w_xq_bf)

    mk_p, mv_p = _mem_kv(mem_prompt[0], norm_mem[0].reshape(1, D_MODEL), w_xkv_bf)
    ox_p = _xattn_prompt(qx, mk_p, mv_p)
    qx_rows = jnp.transpose(qx[SEQ:].reshape(DEC_BATCH, DEC_SEQ, XA_HEADS, HD), (0, 2, 1, 3))
    ox_s = _xattn_sample(qx_rows.reshape(DEC_BATCH, XA_HEADS * DEC_SEQ, HD),
                         cache_mem_k[0].reshape(DEC_BATCH, N_MEM * XA_HEADS, HD),
                         cache_mem_v[0].reshape(DEC_BATCH, N_MEM * XA_HEADS, HD))
    ox_s = jnp.transpose(ox_s.reshape(DEC_BATCH, XA_HEADS, DEC_SEQ, HD), (0, 2, 1, 3)).reshape(N_SAMPLE, XA_WIDTH)
    ox = jnp.concatenate([ox_p, ox_s.astype(BF16)], axis=0)
    h2, xn2, route = _router(ox, h1, w_xo_bf, norm_ffn[0].reshape(1, D_MODEL), w_r_hi, w_r_lo)

    slot, counts, padded, seg_start, tile_expert, n_used, first, nxt, parity = _routing_tables(route)
    xs_sorted = _dispatch(slot, counts, padded, seg_start, n_used, xn2)
    ys_sorted = _moe(tile_expert, n_used, first, nxt, parity, xs_sorted, w_e_gate[0], w_e_up[0], w_e_down[0])
    y_p, y_s = _final(slot, h2, route, norm_final.reshape(1, D_MODEL), ys_sorted)

    sb_shape_p = (1, 1, SEQ, HEADS, HD)
    sb_shape_s = (1, DEC_BATCH, DEC_SEQ, HEADS, HD)
    return (y_p.reshape(1, SEQ, D_MODEL), y_s.reshape(DEC_BATCH, DEC_SEQ, D_MODEL),
            kf_p.reshape(sb_shape_p), vf_p.reshape(sb_shape_p), s_p.reshape(1, 1, HEADS, HD, HD),
            mk_p.reshape(1, 1, N_MEM, XA_HEADS, HD), mv_p.reshape(1, 1, N_MEM, XA_HEADS, HD),
            kf_s.reshape(sb_shape_s), vf_s.reshape(sb_shape_s), s_s.reshape(1, DEC_BATCH, HEADS, HD, HD))
```
